```python
import jax, jax.numpy as jnp
from jax import lax
import numpy as np

D_MODEL = 1024
BATCH = 8
SEQ = 2048
DEPTH = 4

GRID_W = 64
CTX_LEN = 256
N_MIXERS = 3
MOD_CHUNKS = 6
NORM_EPS = 1e-6

SGU_CHUNK = 128
SGU_WIDTH = 2 * D_MODEL
SGU_GROUPS = 8
SGU_GROUP_DIM = SGU_WIDTH // SGU_GROUPS

POOL_WINDOWS = (2, 4, 8, 16)
POOL_WIDTH = D_MODEL
POOL_GROUPS = len(POOL_WINDOWS)
POOL_GROUP_DIM = POOL_WIDTH // POOL_GROUPS

HEAD_DIM = 128
N_Q_HEADS = D_MODEL // HEAD_DIM
N_KV_HEADS = 2
Q_PER_KV = N_Q_HEADS // N_KV_HEADS
AXIS_DIM = HEAD_DIM // 2
ROPE_THETA = 10000.0
ATTN_BLOCK = 128

PEER_HEADS = 8
PEER_KEYS = 128
PEER_EXPERTS = PEER_KEYS * PEER_KEYS
PEER_TOPK = 16
PEER_QDIM = 256
PEER_HALF = PEER_QDIM // 2
PEER_CHUNK = 128

kernel_name = 'hybrid_diffusion_sgu_pool_gqa_peer'


def _rmsnorm(x, gain):
    xf = x.astype(jnp.float32)
    y = xf * lax.rsqrt(jnp.mean(xf * xf, axis=-1, keepdims=True) + NORM_EPS)
    return (y * gain.astype(jnp.float32)).astype(x.dtype)


def _modulate(x, gain, shift, scale):
    return _rmsnorm(x, gain) * (1 + scale) + shift


def _gelu(x):
    return jax.nn.gelu(x, approximate=False)


def _sgu_mixer(h, w_in, v_gain, w_s, b_s, w_out):
    bsz, length, _ = h.shape
    z = _gelu(h @ w_in)
    u, v = z[..., :SGU_WIDTH], z[..., SGU_WIDTH:]
    v = _rmsnorm(v, v_gain).reshape(bsz, length // SGU_CHUNK, SGU_CHUNK, SGU_GROUPS, SGU_GROUP_DIM)
    sv = jnp.einsum('gpq,bnqgd->bnpgd', w_s, v) + b_s.T[:, :, None]
    return (u * sv.reshape(bsz, length, SGU_WIDTH)) @ w_out


def _pool_mixer(h, w_in, w_grp, scale, w_out):
    bsz, length, _ = h.shape
    z = (h @ w_in).astype(jnp.float32).reshape(bsz, length, POOL_GROUPS, POOL_GROUP_DIM)
    cs = jnp.concatenate([jnp.zeros_like(z[:, :1]), jnp.cumsum(z, axis=1)], axis=1)
    t = jnp.arange(length)[:, None]
    half = jnp.array(POOL_WINDOWS, dtype=jnp.int32)[None, :] // 2
    lo = jnp.clip(t - half, 0, length)
    hi = jnp.clip(t + half, 0, length)
    gidx = jnp.arange(POOL_GROUPS)[None, :]
    cnt = (hi - lo).astype(jnp.float32)[None, :, :, None]
    pooled = (cs[:, hi, gidx] - cs[:, lo, gidx]) / cnt - z
    y = jnp.einsum('blgc,gcd->blgd', pooled, w_grp.astype(jnp.float32)).reshape(bsz, length, POOL_WIDTH)
    y = (y * scale.astype(jnp.float32)).astype(h.dtype)
    return y @ w_out


def _axial_rope_tables(length):
    rows = length // GRID_W
    r, col = jnp.meshgrid(jnp.arange(rows), jnp.arange(GRID_W), indexing='ij')
    pos = jnp.stack([r.reshape(-1), col.reshape(-1)], axis=-1).astype(jnp.float32)
    freqs = ROPE_THETA ** (-jnp.arange(0, AXIS_DIM, 2, dtype=jnp.float32) / AXIS_DIM)
    ang = pos[:, :, None] * freqs
    return jnp.cos(ang), jnp.sin(ang)


def _apply_axial_rope(x, cos, sin):
    xf = x.astype(jnp.float32).reshape(*x.shape[:-1], 2, AXIS_DIM)
    x1, x2 = xf[..., :AXIS_DIM // 2], xf[..., AXIS_DIM // 2:]
    c = cos[None, :, None]
    s = sin[None, :, None]
    out = jnp.concatenate([x1 * c - x2 * s, x2 * c + x1 * s], axis=-1)
    return out.reshape(x.shape).astype(x.dtype)


def _attend(q, k, v):
    s = jnp.einsum('bqhgd,bkhd->bhgqk', q, k).astype(jnp.float32) * (HEAD_DIM ** -0.5)
    p = jax.nn.softmax(s, axis=-1).astype(v.dtype)
    return jnp.einsum('bhgqk,bkhd->bqhgd', p, v)


def _attn_mixer(h_lat, h_ctx, w_qkv, q_gain, k_gain, w_o, need_ctx):
    def project(h):
        b, l, _ = h.shape
        qkv = h @ w_qkv
        nq, nk = N_Q_HEADS * HEAD_DIM, N_KV_HEADS * HEAD_DIM
        q = _rmsnorm(qkv[..., :nq].reshape(b, l, N_Q_HEADS, HEAD_DIM), q_gain)
        k = _rmsnorm(qkv[..., nq:nq + nk].reshape(b, l, N_KV_HEADS, HEAD_DIM), k_gain)
        v = qkv[..., nq + nk:].reshape(b, l, N_KV_HEADS, HEAD_DIM)
        return q, k, v

    bsz, length, _ = h_lat.shape
    ctx_len = h_ctx.shape[1]
    q_lat, k_lat, v_lat = project(h_lat)
    q_ctx, k_ctx, v_ctx = project(h_ctx)
    cos, sin = _axial_rope_tables(length)
    q_lat = _apply_axial_rope(q_lat, cos, sin).reshape(bsz, length, N_KV_HEADS, Q_PER_KV, HEAD_DIM)
    k_lat = _apply_axial_rope(k_lat, cos, sin)
    keys = jnp.concatenate([k_lat, k_ctx], axis=1)
    vals = jnp.concatenate([v_lat, v_ctx], axis=1)
    qb = q_lat.reshape(bsz, length // ATTN_BLOCK, ATTN_BLOCK, N_KV_HEADS, Q_PER_KV, HEAD_DIM).swapaxes(0, 1)
    o = lax.map(lambda blk: _attend(blk, keys, vals), qb)
    y_lat = o.swapaxes(0, 1).reshape(bsz, length, N_Q_HEADS * HEAD_DIM) @ w_o
    y_ctx = None
    if need_ctx:
        qc = q_ctx.reshape(bsz, ctx_len, N_KV_HEADS, Q_PER_KV, HEAD_DIM)
        y_ctx = _attend(qc, k_ctx, v_ctx).reshape(bsz, ctx_len, N_Q_HEADS * HEAD_DIM) @ w_o
    return y_lat, y_ctx


def _peer(h, w_q, k1, k2, u_tab, v_tab):
    n_tok = h.shape[0]
    q = (h @ w_q).reshape(n_tok, PEER_HEADS, PEER_QDIM)
    s1 = jnp.einsum('thd,hnd->thn', q[..., :PEER_HALF], k1).astype(jnp.float32)
    s2 = jnp.einsum('thd,hnd->thn', q[..., PEER_HALF:], k2).astype(jnp.float32)
    v1, i1 = lax.top_k(s1, PEER_TOPK)
    v2, i2 = lax.top_k(s2, PEER_TOPK)
    cand = (v1[..., :, None] + v2[..., None, :]).reshape(n_tok, PEER_HEADS, PEER_TOPK * PEER_TOPK)
    cidx = (i1[..., :, None] * PEER_KEYS + i2[..., None, :]).reshape(n_tok, PEER_HEADS, PEER_TOPK * PEER_TOPK)
    best, pos = lax.top_k(cand, PEER_TOPK)
    eidx = jnp.take_along_axis(cidx, pos, axis=-1)
    gate = jax.nn.softmax(best, axis=-1).astype(h.dtype)

    def block(args):
        hc, ic, gc = args
        act = jnp.einsum('chkd,cd->chk', jnp.take(u_tab, ic, axis=0), hc)
        return jnp.einsum('chk,chkd->cd', _gelu(act) * gc, jnp.take(v_tab, ic, axis=0))

    nb = n_tok // PEER_CHUNK
    out = lax.map(block, (h.reshape(nb, PEER_CHUNK, -1),
                          eidx.reshape(nb, PEER_CHUNK, PEER_HEADS, PEER_TOPK),
                          gate.reshape(nb, PEER_CHUNK, PEER_HEADS, PEER_TOPK)))
    return out.reshape(n_tok, -1)


def setup_inputs(seed: int = 0) -> dict:
    key = jax.random.key(seed)
    ks = iter(jax.random.split(key, 32))

    def nrm(shape, scale):
        return jax.random.normal(next(ks), shape, jnp.float32) * scale

    d = D_MODEL
    n_sgu = len(range(0, DEPTH, N_MIXERS))
    n_pool = len(range(1, DEPTH, N_MIXERS))
    n_attn = len(range(2, DEPTH, N_MIXERS))
    qkv_w = (N_Q_HEADS + 2 * N_KV_HEADS) * HEAD_DIM
    return {
        'x': nrm((BATCH, SEQ, d), 1.0),
        'c': nrm((BATCH, d), 1.0),
        'ctx': nrm((BATCH, CTX_LEN, d), 1.0),
        'c_ctx': nrm((d,), 1.0),
        'ada_w': nrm((DEPTH, d, MOD_CHUNKS * d), 0.5 * d ** -0.5),
        'ada_b': nrm((DEPTH, MOD_CHUNKS * d), 0.02),
        'norm1_g': 1.0 + nrm((DEPTH, d), 0.05),
        'norm2_g': 1.0 + nrm((DEPTH, d), 0.05),
        'sgu_w_in': nrm((n_sgu, d, 2 * SGU_WIDTH), d ** -0.5),
        'sgu_v_gain': 1.0 + nrm((n_sgu, SGU_WIDTH), 0.05),
        'sgu_w_s': nrm((n_sgu, SGU_GROUPS, SGU_CHUNK, SGU_CHUNK), SGU_CHUNK ** -0.5),
        'sgu_b_s': 1.0 + nrm((n_sgu, SGU_GROUPS, SGU_CHUNK), 0.05),
        'sgu_w_out': nrm((n_sgu, SGU_WIDTH, d), SGU_WIDTH ** -0.5),
        'pool_w_in': nrm((n_pool, d, POOL_WIDTH), d ** -0.5),
        'pool_w_grp': nrm((n_pool, POOL_GROUPS, POOL_GROUP_DIM, POOL_GROUP_DIM), POOL_GROUP_DIM ** -0.5),
        'pool_scale': 1.0 + nrm((n_pool, POOL_WIDTH), 0.05),
        'pool_w_out': nrm((n_pool, POOL_WIDTH, d), POOL_WIDTH ** -0.5),
        'attn_w_qkv': nrm((n_attn, d, qkv_w), d ** -0.5),
        'attn_q_gain': 1.0 + nrm((n_attn, HEAD_DIM), 0.05),
        'attn_k_gain': 1.0 + nrm((n_attn, HEAD_DIM), 0.05),
        'attn_w_o': nrm((n_attn, N_Q_HEADS * HEAD_DIM, d), (N_Q_HEADS * HEAD_DIM) ** -0.5),
        'peer_w_q': nrm((DEPTH, d, PEER_HEADS * PEER_QDIM), d ** -0.5),
        'peer_k1': nrm((DEPTH, PEER_HEADS, PEER_KEYS, PEER_HALF), PEER_HALF ** -0.5),
        'peer_k2': nrm((DEPTH, PEER_HEADS, PEER_KEYS, PEER_HALF), PEER_HALF ** -0.5),
        'peer_u': nrm((DEPTH, PEER_EXPERTS, d), d ** -0.5),
        'peer_v': nrm((DEPTH, PEER_EXPERTS, d), PEER_HEADS ** -0.5),
        'final_gain': 1.0 + nrm((d,), 0.05),
    }


def reference(x, c, ctx, c_ctx, ada_w, ada_b, norm1_g, norm2_g, sgu_w_in, sgu_v_gain, sgu_w_s, sgu_b_s,
              sgu_w_out, pool_w_in, pool_w_grp, pool_scale, pool_w_out, attn_w_qkv, attn_q_gain, attn_k_gain,
              attn_w_o, peer_w_q, peer_k1, peer_k2, peer_u, peer_v, final_gain):
    bsz, seq, d = x.shape
    ctx_len = ctx.shape[1]
    x_lat, x_ctx = x, ctx
    for i in range(DEPTH):
        kind, j = i % N_MIXERS, i // N_MIXERS
        need_ctx = i < DEPTH - 1
        mod_lat = jnp.split((jax.nn.silu(c) @ ada_w[i] + ada_b[i])[:, None, :], MOD_CHUNKS, axis=-1)
        mod_ctx = jnp.split((jax.nn.silu(c_ctx) @ ada_w[i] + ada_b[i])[None, None, :], MOD_CHUNKS, axis=-1)

        h_lat = _modulate(x_lat, norm1_g[i], mod_lat[0], mod_lat[1])
        h_ctx = _modulate(x_ctx, norm1_g[i], mod_ctx[0], mod_ctx[1]) if (need_ctx or kind == 2) else None
        y_ctx = None
        if kind == 0:
            y_lat = _sgu_mixer(h_lat, sgu_w_in[j], sgu_v_gain[j], sgu_w_s[j], sgu_b_s[j], sgu_w_out[j])
            if need_ctx:
                y_ctx = _sgu_mixer(h_ctx, sgu_w_in[j], sgu_v_gain[j], sgu_w_s[j], sgu_b_s[j], sgu_w_out[j])
        elif kind == 1:
            y_lat = _pool_mixer(h_lat, pool_w_in[j], pool_w_grp[j], pool_scale[j], pool_w_out[j])
            if need_ctx:
                y_ctx = _pool_mixer(h_ctx, pool_w_in[j], pool_w_grp[j], pool_scale[j], pool_w_out[j])
        else:
            y_lat, y_ctx = _attn_mixer(h_lat, h_ctx, attn_w_qkv[j], attn_q_gain[j], attn_k_gain[j],
                                       attn_w_o[j], need_ctx)
        x_lat = x_lat + mod_lat[2] * y_lat
        if need_ctx:
            x_ctx = x_ctx + mod_ctx[2] * y_ctx

        f_lat_in = _modulate(x_lat, norm2_g[i], mod_lat[3], mod_lat[4]).reshape(bsz * seq, d)
        if need_ctx:
            f_ctx_in = _modulate(x_ctx, norm2_g[i], mod_ctx[3], mod_ctx[4]).reshape(bsz * ctx_len, d)
            f = _peer(jnp.concatenate([f_lat_in, f_ctx_in], axis=0), peer_w_q[i], peer_k1[i], peer_k2[i],
                      peer_u[i], peer_v[i])
            f_lat = f[:bsz * seq]
            x_ctx = x_ctx + mod_ctx[5] * f[bsz * seq:].reshape(bsz, ctx_len, d)
        else:
            f_lat = _peer(f_lat_in, peer_w_q[i], peer_k1[i], peer_k2[i], peer_u[i], peer_v[i])
        x_lat = x_lat + mod_lat[5] * f_lat.reshape(bsz, seq, d)
    return _rmsnorm(x_lat, final_gain)
```

```python
import functools
import math

import jax
import jax.numpy as jnp
from jax import lax
from jax.experimental import pallas as pl
from jax.experimental.pallas import tpu as pltpu

F32 = jnp.float32
BF16 = jnp.bfloat16

NORM_EPS = 1e-6
MOD_CHUNKS = 6
GRID_W = 64
SGU_CHUNK = 128
SGU_GROUPS = 8
POOL_WINDOWS = (2, 4, 8, 16)
HEAD_DIM = 128
N_KV_HEADS = 2
ROPE_THETA = 10000.0
PEER_HEADS = 8
PEER_KEYS = 128
PEER_TOPK = 16
N_MIXERS = 3

LANES = 128
MOD_ROWS = 16
VMEM_LIMIT = 56 * 1024 * 1024

TOKEN_TILE = 256
PEER_TILE = 512
PEER_EXPERT_CHUNK = 1024


def _cparams(*sem):
    return pltpu.CompilerParams(dimension_semantics=sem, vmem_limit_bytes=VMEM_LIMIT)


def _rms(x):
    return x * lax.rsqrt(jnp.mean(x * x, axis=-1, keepdims=True) + NORM_EPS)


def _modulate(x, gain, shift, scale):
    return _rms(x) * gain * (1.0 + scale) + shift


def _gelu(x):
    return 0.5 * x * (1.0 + lax.erf(x * (1.0 / math.sqrt(2.0))))


def _dot(a, b):
    return jnp.dot(a, b, preferred_element_type=F32)


def _dot_nt(a, b):
    return lax.dot_general(a, b, (((1,), (1,)), ((), ())), preferred_element_type=F32)


def _ada_kernel(c_ref, w_ref, b_ref, o_ref):
    c = c_ref[...]
    a = c * (1.0 / (1.0 + jnp.exp(-c)))
    o_ref[0] = _dot(a.astype(BF16), w_ref[0].astype(BF16)) + b_ref[0]


def _ada_mods(cond, ada_w, ada_b):
    depth, d, n = ada_w.shape
    tn = n // 4
    out = pl.pallas_call(
        _ada_kernel,
        grid=(depth, n // tn),
        in_specs=[
            pl.BlockSpec((MOD_ROWS, d), lambda l, j: (0, 0)),
            pl.BlockSpec((1, d, tn), lambda l, j: (l, 0, j)),
            pl.BlockSpec((1, 1, tn), lambda l, j: (l, 0, j)),
        ],
        out_specs=pl.BlockSpec((1, MOD_ROWS, tn), lambda l, j: (l, 0, j)),
        out_shape=jax.ShapeDtypeStruct((depth, MOD_ROWS, n), F32),
        compiler_params=_cparams("parallel", "parallel"),
        name="ada_mods",
    )(cond, ada_w, ada_b.reshape(depth, 1, n))
    return out.reshape(depth, MOD_ROWS, MOD_CHUNKS, d)


class _Stream:
    def __init__(self, bsz, seq, ctx_len, d):
        self.bsz, self.seq, self.ctx_len, self.d = bsz, seq, ctx_len, d
        self.n_lat = bsz * seq
        self.n_tok = bsz * (seq + ctx_len)

    def tiles(self, tm, with_ctx):
        return (self.n_tok if with_ctx else self.n_lat) // tm

    def mod_row(self, tm):
        n_lat_tiles, per_seq, bsz = self.n_lat // tm, self.seq // tm, self.bsz
        return lambda t: jnp.where(t < n_lat_tiles, t // per_seq, bsz)

    def mod_spec(self, tm):
        row = self.mod_row(tm)
        return pl.BlockSpec((1, MOD_CHUNKS, self.d), lambda t, *_: (row(t), 0, 0))


def _tok_spec(tm, width):
    return pl.BlockSpec((tm, width), lambda t, *_: (t, 0))


def _full_spec(shape):
    zeros = (0,) * len(shape)
    return pl.BlockSpec(shape, lambda *_: zeros)


def _sgu_kernel(x_ref, mod_ref, g_ref, win_ref, vg_ref, ws_ref, bs_ref, wout_ref, o_ref):
    x = x_ref[...]
    mod = mod_ref[0]
    tm = x.shape[0]
    width = vg_ref.shape[1]
    gdim = width // SGU_GROUPS
    h = _modulate(x, g_ref[...], mod[0:1], mod[1:2]).astype(BF16)
    u = _gelu(_dot(h, win_ref[:, :width]))
    v = _gelu(_dot(h, win_ref[:, width:]))
    vn = (_rms(v) * vg_ref[...]).astype(BF16)
    rows = []
    for c in range(tm // SGU_CHUNK):
        cols = []
        for g in range(SGU_GROUPS):
            vv = vn[c * SGU_CHUNK:(c + 1) * SGU_CHUNK, g * gdim:(g + 1) * gdim]
            cols.append(_dot(ws_ref[g], vv) + bs_ref[:, g:g + 1])
        rows.append(jnp.concatenate(cols, axis=1))
    sv = jnp.concatenate(rows, axis=0)
    y = _dot((u * sv).astype(BF16), wout_ref[...])
    o_ref[...] = x + mod[2:3] * y


def _sgu_layer(st, xs, mods, gain, w_in, v_gain, w_s, b_s, w_out, with_ctx):
    tm, d = TOKEN_TILE, st.d
    width = v_gain.shape[0]
    return pl.pallas_call(
        _sgu_kernel,
        grid=(st.tiles(tm, with_ctx),),
        in_specs=[
            _tok_spec(tm, d), st.mod_spec(tm), _full_spec((1, d)),
            _full_spec((d, 2 * width)), _full_spec((1, width)),
            _full_spec((SGU_GROUPS, SGU_CHUNK, SGU_CHUNK)), _full_spec((SGU_CHUNK, SGU_GROUPS)),
            _full_spec((width, d)),
        ],
        out_specs=_tok_spec(tm, d),
        out_shape=jax.ShapeDtypeStruct((st.tiles(tm, with_ctx) * tm, d), F32),
        compiler_params=_cparams("parallel"),
        name="sgu_mixer",
    )(xs, mods, gain.reshape(1, d), w_in.astype(BF16), v_gain.reshape(1, width),
      w_s.astype(BF16), b_s.T, w_out.astype(BF16))


POOL_HALO = 8


def _pool_in_kernel(x_ref, mod_ref, g_ref, win_ref, z_ref):
    mod = mod_ref[0]
    h = _modulate(x_ref[...], g_ref[...], mod[0:1], mod[1:2]).astype(BF16)
    z_ref[...] = _dot(h, win_ref[...])


def _pool_out_kernel(z_ref, zp_ref, zn_ref, x_ref, mod_ref, wg_ref, sc_ref, wout_ref, o_ref,
                     *, n_lat_tiles, lat_tiles_per_seq, ctx_tiles_per_seq, seq, ctx_len):
    t = pl.program_id(0)
    tm = z_ref.shape[0]
    is_lat = t < n_lat_tiles
    per_seq = jnp.where(is_lat, lat_tiles_per_seq, ctx_tiles_per_seq)
    pos_tile = jnp.where(is_lat, t, t - n_lat_tiles) % per_seq
    length = jnp.where(is_lat, seq, ctx_len)
    has_prev = (pos_tile > 0).astype(F32)
    has_next = (pos_tile < per_seq - 1).astype(F32)
    z = z_ref[...]
    zext = jnp.concatenate([zp_ref[...] * has_prev, z, zn_ref[...] * has_next], axis=0)
    pos = (pos_tile * tm + lax.broadcasted_iota(jnp.int32, (tm, 1), 0))
    gdim = wg_ref.shape[1]
    n_ext = tm + 2 * POOL_HALO

    def shift_up(a, k):
        return a if k == 0 else pltpu.roll(a, n_ext - k, 0)

    outs = []
    for g, w in enumerate(POOL_WINDOWS):
        half = w // 2
        acc = zext[:, g * gdim:(g + 1) * gdim]
        span = 1
        while span < w:
            acc = acc + shift_up(acc, span)
            span *= 2
        win_sum = shift_up(acc, POOL_HALO - half)[:tm]
        cnt = (jnp.minimum(pos + half, length) - jnp.maximum(pos - half, 0)).astype(F32)
        pooled = win_sum / cnt - z[:, g * gdim:(g + 1) * gdim]
        outs.append(_dot(pooled.astype(BF16), wg_ref[g]))
    y = (jnp.concatenate(outs, axis=1) * sc_ref[...]).astype(BF16)
    o_ref[...] = x_ref[...] + mod_ref[0][2:3] * _dot(y, wout_ref[...])


def _pool_layer(st, xs, mods, gain, w_in, w_grp, scale, w_out, with_ctx):
    tm, d = TOKEN_TILE, st.d
    width = w_in.shape[1]
    n_tiles = st.tiles(tm, with_ctx)
    z = pl.pallas_call(
        _pool_in_kernel,
        grid=(n_tiles,),
        in_specs=[_tok_spec(tm, d), st.mod_spec(tm), _full_spec((1, d)), _full_spec((d, width))],
        out_specs=_tok_spec(tm, width),
        out_shape=jax.ShapeDtypeStruct((n_tiles * tm, width), F32),
        compiler_params=_cparams("parallel"),
        name="pool_in",
    )(xs, mods, gain.reshape(1, d), w_in.astype(BF16))
    halo_per_tile = tm // POOL_HALO
    last_halo = n_tiles * halo_per_tile - 1
    kern = functools.partial(
        _pool_out_kernel, n_lat_tiles=st.n_lat // tm, lat_tiles_per_seq=st.seq // tm,
        ctx_tiles_per_seq=st.ctx_len // tm, seq=st.seq, ctx_len=st.ctx_len)
    return pl.pallas_call(
        kern,
        grid=(n_tiles,),
        in_specs=[
            _tok_spec(tm, width),
            pl.BlockSpec((POOL_HALO, width), lambda t: (jnp.maximum(t * halo_per_tile - 1, 0), 0)),
            pl.BlockSpec((POOL_HALO, width), lambda t: (jnp.minimum((t + 1) * halo_per_tile, last_halo), 0)),
            _tok_spec(tm, d), st.mod_spec(tm),
            _full_spec(w_grp.shape), _full_spec((1, width)), _full_spec((width, d)),
        ],
        out_specs=_tok_spec(tm, d),
        out_shape=jax.ShapeDtypeStruct((st.tiles(tm, with_ctx) * tm, d), F32),
        compiler_params=_cparams("parallel"),
        name="pool_out",
    )(z, z, z, xs, mods, w_grp.astype(BF16), scale.reshape(1, width), w_out.astype(BF16))


def _rope_tables(seq, tm):
    pos = jnp.arange(seq)
    axis_dim = HEAD_DIM // 2
    freqs = ROPE_THETA ** (-jnp.arange(0, axis_dim, 2, dtype=F32) / axis_dim)
    ang = jnp.stack([pos // GRID_W, pos % GRID_W], axis=-1).astype(F32)[:, :, None] * freqs
    cos, sin = jnp.cos(ang), jnp.sin(ang)
    zero = jnp.zeros_like(sin)
    c = jnp.concatenate([cos, cos], axis=-1).reshape(seq, HEAD_DIM)
    s_lo = jnp.concatenate([-sin, zero], axis=-1).reshape(seq, HEAD_DIM)
    s_hi = jnp.concatenate([zero, sin], axis=-1).reshape(seq, HEAD_DIM)
    pad = jnp.zeros((tm, HEAD_DIM), F32)
    return (jnp.concatenate([c, pad + 1.0]), jnp.concatenate([s_lo, pad]), jnp.concatenate([s_hi, pad]))


def _qkv_kernel(x_ref, mod_ref, g_ref, w_ref, qg_ref, kg_ref, c_ref, slo_ref, shi_ref, q_ref, k_ref, v_ref):
    mod = mod_ref[0]
    h = _modulate(x_ref[...], g_ref[...], mod[0:1], mod[1:2]).astype(BF16)
    qkv = _dot(h, w_ref[...])
    nq, nk = q_ref.shape[1], k_ref.shape[1]
    cos, s_lo, s_hi = c_ref[...], slo_ref[...], shi_ref[...]
    quarter = HEAD_DIM // 4

    def norm_rope(t, gain):
        t = _rms(t) * gain
        return (t * cos + pltpu.roll(t, HEAD_DIM - quarter, 1) * s_lo + pltpu.roll(t, quarter, 1) * s_hi)

    for hh in range(nq // HEAD_DIM):
        sl = slice(hh * HEAD_DIM, (hh + 1) * HEAD_DIM)
        q_ref[:, sl] = norm_rope(qkv[:, sl], qg_ref[...]).astype(BF16)
    for hh in range(nk // HEAD_DIM):
        sl = slice(hh * HEAD_DIM, (hh + 1) * HEAD_DIM)
        k_ref[:, sl] = norm_rope(qkv[:, nq + hh * HEAD_DIM:nq + (hh + 1) * HEAD_DIM], kg_ref[...]).astype(BF16)
    v_ref[...] = qkv[:, nq + nk:].astype(BF16)


def _attn_kernel(q_ref, kl_ref, vl_ref, kc_ref, vc_ref, x_ref, mod_ref, wo_ref, o_ref, *, n_lat_tiles):
    n_heads = q_ref.shape[1] // HEAD_DIM
    q_per_kv = n_heads // N_KV_HEADS
    scale = HEAD_DIM ** -0.5

    def body(with_lat):
        outs = []
        for hh in range(n_heads):
            kv = slice((hh // q_per_kv) * HEAD_DIM, (hh // q_per_kv + 1) * HEAD_DIM)
            q = q_ref[:, hh * HEAD_DIM:(hh + 1) * HEAD_DIM]
            s_c = _dot_nt(q, kc_ref[:, kv]) * scale
            m = jnp.max(s_c, axis=-1, keepdims=True)
            if with_lat:
                s_l = _dot_nt(q, kl_ref[:, kv]) * scale
                m = jnp.maximum(m, jnp.max(s_l, axis=-1, keepdims=True))
            p_c = jnp.exp(s_c - m)
            den = jnp.sum(p_c, axis=-1, keepdims=True)
            o = _dot(p_c.astype(BF16), vc_ref[:, kv])
            if with_lat:
                p_l = jnp.exp(s_l - m)
                den = den + jnp.sum(p_l, axis=-1, keepdims=True)
                o = o + _dot(p_l.astype(BF16), vl_ref[:, kv])
            outs.append(o / den)
        y = _dot(jnp.concatenate(outs, axis=1).astype(BF16), wo_ref[...])
        o_ref[...] = x_ref[...] + mod_ref[0][2:3] * y

    t = pl.program_id(0)
    pl.when(t < n_lat_tiles)(lambda: body(True))
    pl.when(t >= n_lat_tiles)(lambda: body(False))


def _attn_layer(st, xs, mods, gain, w_qkv, q_gain, k_gain, w_o, with_ctx):
    d = st.d
    tm = st.ctx_len
    assert st.seq % tm == 0
    n_all = st.n_tok // tm
    n_lat_tiles, per_seq = st.n_lat // tm, st.seq // tm
    nq = w_o.shape[0]
    nk = (w_qkv.shape[1] - nq) // 2
    cos, s_lo, s_hi = _rope_tables(st.seq, tm)
    rope_spec = pl.BlockSpec((tm, HEAD_DIM), lambda t: (jnp.where(t < n_lat_tiles, t % per_seq, per_seq), 0))
    q, k, v = pl.pallas_call(
        _qkv_kernel,
        grid=(n_all,),
        in_specs=[
            _tok_spec(tm, d), st.mod_spec(tm), _full_spec((1, d)), _full_spec(w_qkv.shape),
            _full_spec((1, HEAD_DIM)), _full_spec((1, HEAD_DIM)), rope_spec, rope_spec, rope_spec,
        ],
        out_specs=[_tok_spec(tm, nq), _tok_spec(tm, nk), _tok_spec(tm, nk)],
        out_shape=[jax.ShapeDtypeStruct((st.n_tok, nq), BF16), jax.ShapeDtypeStruct((st.n_tok, nk), BF16),
                   jax.ShapeDtypeStruct((st.n_tok, nk), BF16)],
        compiler_params=_cparams("parallel"),
        name="attn_qkv",
    )(xs, mods, gain.reshape(1, d), w_qkv.astype(BF16), q_gain.reshape(1, HEAD_DIM),
      k_gain.reshape(1, HEAD_DIM), cos, s_lo, s_hi)

    bsz = st.bsz
    batch_of = lambda t: jnp.where(t < n_lat_tiles, t // per_seq, t - n_lat_tiles)
    lat_spec = pl.BlockSpec((st.seq, nk), lambda t: (jnp.minimum(t // per_seq, bsz - 1), 0))
    ctx_spec = pl.BlockSpec((tm, nk), lambda t: (n_lat_tiles + batch_of(t), 0))
    return pl.pallas_call(
        functools.partial(_attn_kernel, n_lat_tiles=n_lat_tiles),
        grid=(st.tiles(tm, with_ctx),),
        in_specs=[_tok_spec(tm, nq), lat_spec, lat_spec, ctx_spec, ctx_spec,
                  _tok_spec(tm, d), st.mod_spec(tm), _full_spec((nq, d))],
        out_specs=_tok_spec(tm, d),
        out_shape=jax.ShapeDtypeStruct((st.tiles(tm, with_ctx) * tm, d), F32),
        compiler_params=_cparams("parallel"),
        name="attn_core",
    )(q, k, v, k, v, xs, mods, w_o.astype(BF16))


def _top16(s, iota):
    n = float(s.shape[0])
    rank = jnp.full(s.shape, float(PEER_TOPK), F32)
    vals = jnp.zeros((PEER_TOPK, s.shape[1]), F32)
    tops = []
    for a in range(PEER_TOPK):
        m = jnp.max(s, axis=0, keepdims=True)
        first = jnp.min(jnp.where(s == m, iota, n), axis=0, keepdims=True)
        hit = iota == first
        rank = jnp.where(hit, float(a), rank)
        s = jnp.where(hit, -jnp.inf, s)
        vals = jnp.where(iota[0:PEER_TOPK] == float(a), m, vals)
        tops.append(m)
    return vals, tops, rank


_CAND_HALF = PEER_TOPK // 2


def _peer_select(s1, s2):
    n, width = s1.shape
    iota = lax.broadcasted_iota(jnp.int32, (n, width), 0).astype(F32)
    v1, top1, r1 = _top16(s1, iota)
    v2, top2, r2 = _top16(s2, iota)
    pieces = [v2 + top1[0]]
    pieces += [v2[0:_CAND_HALF] + top1[a] for a in range(1, _CAND_HALF)]
    pieces.append(v1[_CAND_HALF:] + top2[0])
    cand = jnp.concatenate(pieces, axis=0)
    nc = cand.shape[0]
    ciota = iota[0:nc]
    sel = jnp.zeros(cand.shape, F32)
    c = cand
    for _ in range(PEER_TOPK):
        m = jnp.max(c, axis=0, keepdims=True)
        first = jnp.min(jnp.where(c == m, ciota, float(nc)), axis=0, keepdims=True)
        hit = ciota == first
        sel = jnp.where(hit, 1.0, sel)
        c = jnp.where(hit, -jnp.inf, c)
    p = sel * jnp.exp(cand - (top1[0] + top2[0]))
    z = jnp.sum(p, axis=0, keepdims=True)
    cnt = [jnp.sum(sel[0:PEER_TOPK], axis=0, keepdims=True)]
    for a in range(1, _CAND_HALF):
        row = PEER_TOPK + (a - 1) * _CAND_HALF
        cnt.append(jnp.sum(sel[row:row + _CAND_HALF], axis=0, keepdims=True))
    tail = sel[nc - _CAND_HALF:]
    for a in range(_CAND_HALF, PEER_TOPK):
        pick = iota[0:_CAND_HALF] == float(a - _CAND_HALF)
        cnt.append(jnp.sum(jnp.where(pick, tail, 0.0), axis=0, keepdims=True))
    cut = jnp.zeros(s1.shape, F32)
    for a in range(PEER_TOPK):
        cut = jnp.where(r1 == float(a), cnt[a], cut)
    e1 = jnp.exp(s1 - top1[0])
    e2 = jnp.exp(s2 - top2[0]) / z
    return cut, e1, r2, e2


def _peer_select_kernel(x_ref, mod_ref, g_ref, wq_ref, k1_ref, k2_ref,
                        h_ref, cut_ref, e1_ref, r2_ref, e2_ref, s1_scr, s2_scr):
    mod = mod_ref[0]
    f = _modulate(x_ref[...], g_ref[...], mod[3:4], mod[4:5]).astype(BF16)
    h_ref[...] = f
    tm = f.shape[0]
    n_heads, n_keys, half = k1_ref.shape
    q_t = _dot_nt(wq_ref[...], f)
    for hh in range(n_heads):
        base = hh * 2 * half
        s1_scr[hh] = _dot(k1_ref[hh], q_t[base:base + half].astype(BF16))
        s2_scr[hh] = _dot(k2_ref[hh], q_t[base + half:base + 2 * half].astype(BF16))
    n_sub = tm // LANES

    def step(idx, carry):
        hh = idx // n_sub
        lanes = pl.ds(pl.multiple_of((idx % n_sub) * LANES, LANES), LANES)
        cut, e1, r2, e2 = _peer_select(s1_scr[hh, :, lanes], s2_scr[hh, :, lanes])
        cut_ref[hh, :, lanes] = cut
        e1_ref[hh, :, lanes] = e1
        r2_ref[hh, :, lanes] = r2
        e2_ref[hh, :, lanes] = e2
        return carry

    lax.fori_loop(0, n_heads * n_sub, step, 0)


def _peer_main_kernel(h_ref, u_ref, vt_ref, cut_ref, e1_ref, r2_ref, e2_ref, x_ref, mod_ref, fg_ref,
                      o_ref, acc_ref, w_scr, act_scr, *, final_norm):
    ci = pl.program_id(1)

    @pl.when(ci == 0)
    def _():
        acc_ref[...] = jnp.zeros_like(acc_ref)

    ec, tm = act_scr.shape
    n_heads, n_keys = r2_ref.shape[0], r2_ref.shape[1]
    act_scr[...] = _dot_nt(u_ref[...], h_ref[...])
    for il in range(ec // n_keys):
        rows = slice(il * n_keys, (il + 1) * n_keys)
        for tl in range(tm // LANES):
            lanes = slice(tl * LANES, (tl + 1) * LANES)
            gate = jnp.zeros((n_keys, LANES), F32)
            for hh in range(n_heads):
                hit = r2_ref[hh, :, lanes] < cut_ref[hh, il:il + 1, lanes]
                gate = gate + jnp.where(hit, e2_ref[hh, :, lanes], 0.0) * e1_ref[hh, il:il + 1, lanes]
            w_scr[rows, lanes] = (_gelu(act_scr[rows, lanes]) * gate).astype(BF16)
    acc_ref[...] += _dot(vt_ref[...], w_scr[...])

    @pl.when(ci == pl.num_programs(1) - 1)
    def _():
        y = x_ref[...] + mod_ref[0][5:6] * acc_ref[...].T
        if final_norm:
            y = _rms(y) * fg_ref[...]
        o_ref[...] = y


def _peer_layer(st, xs, mods, gain, w_q, k1, k2, u_tab, v_tab, final_gain, with_ctx, final_norm):
    tm, d = PEER_TILE, st.d
    n_heads, n_keys, half = k1.shape
    n_exp = u_tab.shape[0]
    ec = PEER_EXPERT_CHUNK
    n_tiles = st.tiles(tm, with_ctx)
    n_rows = n_tiles * tm
    key_spec = pl.BlockSpec((n_heads, n_keys, tm), lambda t, *_: (0, 0, t))
    key_shape = jax.ShapeDtypeStruct((n_heads, n_keys, n_rows), F32)
    h, cut, e1, r2, e2 = pl.pallas_call(
        _peer_select_kernel,
        grid=(n_tiles,),
        in_specs=[_tok_spec(tm, d), st.mod_spec(tm), _full_spec((1, d)), _full_spec((w_q.shape[1], d)),
                  _full_spec(k1.shape), _full_spec(k2.shape)],
        out_specs=[_tok_spec(tm, d), key_spec, key_spec, key_spec, key_spec],
        out_shape=[jax.ShapeDtypeStruct((n_rows, d), BF16), key_shape, key_shape, key_shape, key_shape],
        scratch_shapes=[pltpu.VMEM((n_heads, n_keys, tm), F32), pltpu.VMEM((n_heads, n_keys, tm), F32)],
        compiler_params=_cparams("parallel"),
        name="peer_select",
    )(xs, mods, gain.reshape(1, d), w_q.T.astype(BF16), k1.astype(BF16), k2.astype(BF16))

    rows_per_chunk = ec // n_keys
    sub1_spec = pl.BlockSpec((n_heads, rows_per_chunk, tm), lambda t, c: (0, c, t))
    return pl.pallas_call(
        functools.partial(_peer_main_kernel, final_norm=final_norm),
        grid=(n_tiles, n_exp // ec),
        in_specs=[
            _tok_spec(tm, d),
            pl.BlockSpec((ec, d), lambda t, c: (c, 0)),
            pl.BlockSpec((d, ec), lambda t, c: (0, c)),
            sub1_spec, sub1_spec, key_spec, key_spec,
            _tok_spec(tm, d), st.mod_spec(tm), _full_spec((1, d)),
        ],
        out_specs=_tok_spec(tm, d),
        out_shape=jax.ShapeDtypeStruct((n_rows, d), F32),
        scratch_shapes=[pltpu.VMEM((d, tm), F32), pltpu.VMEM((ec, tm), BF16), pltpu.VMEM((ec, tm), F32)],
        compiler_params=_cparams("parallel", "arbitrary"),
        name="peer_main",
    )(h, u_tab.astype(BF16), v_tab.T.astype(BF16), cut, e1, r2, e2, xs, mods, final_gain.reshape(1, d))


def kernel(x, c, ctx, c_ctx, ada_w, ada_b, norm1_g, norm2_g, sgu_w_in, sgu_v_gain, sgu_w_s, sgu_b_s, sgu_w_out, pool_w_in, pool_w_grp, pool_scale, pool_w_out, attn_w_qkv, attn_q_gain, attn_k_gain, attn_w_o, peer_w_q, peer_k1, peer_k2, peer_u, peer_v, final_gain):
    bsz, seq, d = x.shape
    ctx_len = ctx.shape[1]
    depth = ada_w.shape[0]
    st = _Stream(bsz, seq, ctx_len, d)
    assert bsz + 1 <= MOD_ROWS and seq % PEER_TILE == 0 and (bsz * ctx_len) % PEER_TILE == 0
    assert seq % TOKEN_TILE == 0 and ctx_len % TOKEN_TILE == 0

    cond = jnp.concatenate([c, c_ctx[None, :], jnp.zeros((MOD_ROWS - bsz - 1, d), F32)], axis=0)
    mods = _ada_mods(cond, ada_w, ada_b)
    xs = jnp.concatenate([x.reshape(bsz * seq, d), ctx.reshape(bsz * ctx_len, d)], axis=0)

    for i in range(depth):
        kind, j = i % N_MIXERS, i // N_MIXERS
        with_ctx = i < depth - 1
        if kind == 0:
            xs = _sgu_layer(st, xs, mods[i], norm1_g[i], sgu_w_in[j], sgu_v_gain[j], sgu_w_s[j], sgu_b_s[j],
                            sgu_w_out[j], with_ctx)
        elif kind == 1:
            xs = _pool_layer(st, xs, mods[i], norm1_g[i], pool_w_in[j], pool_w_grp[j], pool_scale[j],
                             pool_w_out[j], with_ctx)
        else:
            xs = _attn_layer(st, xs, mods[i], norm1_g[i], attn_w_qkv[j], attn_q_gain[j], attn_k_gain[j],
                             attn_w_o[j], with_ctx)
        xs = _peer_layer(st, xs, mods[i], norm2_g[i], peer_w_q[i], peer_k1[i], peer_k2[i], peer_u[i],
                         peer_v[i], final_gain, with_ctx, final_norm=(i == depth - 1))
    return xs[:bsz * seq].reshape(bsz, seq, d)
```

```python
import functools
import math

import jax
import jax.numpy as jnp
from jax import lax
from jax.experimental import pallas as pl
from jax.experimental.pallas import tpu as pltpu

F32 = jnp.float32
BF16 = jnp.bfloat16

NORM_EPS = 1e-6
MOD_CHUNKS = 6
GRID_W = 64
SGU_CHUNK = 128
SGU_GROUPS = 8
POOL_WINDOWS = (2, 4, 8, 16)
HEAD_DIM = 128
N_KV_HEADS = 2
ROPE_THETA = 10000.0
PEER_HEADS = 8
PEER_KEYS = 128
PEER_TOPK = 16
N_MIXERS = 3

LANES = 128
MOD_ROWS = 16
VMEM_LIMIT = 56 * 1024 * 1024

TOKEN_TILE = 256
PEER_TILE = 512
PEER_EXPERT_CHUNK = 1024


def _cparams(*sem):
    return pltpu.CompilerParams(dimension_semantics=sem, vmem_limit_bytes=VMEM_LIMIT)


def _rms(x):
    return x * lax.rsqrt(jnp.mean(x * x, axis=-1, keepdims=True) + NORM_EPS)


def _modulate(x, gain, shift, scale):
    return _rms(x) * gain * (1.0 + scale) + shift


def _gelu(x):
    return 0.5 * x * (1.0 + lax.erf(x * (1.0 / math.sqrt(2.0))))


def _dot(a, b):
    return jnp.dot(a, b, preferred_element_type=F32)


def _dot_nt(a, b):
    return lax.dot_general(a, b, (((1,), (1,)), ((), ())), preferred_element_type=F32)


def _ada_kernel(c_ref, w_ref, b_ref, o_ref):
    c = c_ref[...]
    a = c * (1.0 / (1.0 + jnp.exp(-c)))
    o_ref[0] = _dot(a.astype(BF16), w_ref[0].astype(BF16)) + b_ref[0]


def _ada_mods(cond, ada_w, ada_b):
    depth, d, n = ada_w.shape
    tn = n // 4
    out = pl.pallas_call(
        _ada_kernel,
        grid=(depth, n // tn),
        in_specs=[
            pl.BlockSpec((MOD_ROWS, d), lambda l, j: (0, 0)),
            pl.BlockSpec((1, d, tn), lambda l, j: (l, 0, j)),
            pl.BlockSpec((1, 1, tn), lambda l, j: (l, 0, j)),
        ],
        out_specs=pl.BlockSpec((1, MOD_ROWS, tn), lambda l, j: (l, 0, j)),
        out_shape=jax.ShapeDtypeStruct((depth, MOD_ROWS, n), F32),
        compiler_params=_cparams("parallel", "parallel"),
        name="ada_mods",
    )(cond, ada_w, ada_b.reshape(depth, 1, n))
    return out.reshape(depth, MOD_ROWS, MOD_CHUNKS, d)


class _Stream:
    def __init__(self, bsz, seq, ctx_len, d):
        self.bsz, self.seq, self.ctx_len, self.d = bsz, seq, ctx_len, d
        self.n_lat = bsz * seq
        self.n_tok = bsz * (seq + ctx_len)

    def tiles(self, tm, with_ctx):
        return (self.n_tok if with_ctx else self.n_lat) // tm

    def mod_row(self, tm):
        n_lat_tiles, per_seq, bsz = self.n_lat // tm, self.seq // tm, self.bsz
        return lambda t: jnp.where(t < n_lat_tiles, t // per_seq, bsz)

    def mod_spec(self, tm):
        row = self.mod_row(tm)
        return pl.BlockSpec((1, MOD_CHUNKS, self.d), lambda t, *_: (row(t), 0, 0))


def _tok_spec(tm, width):
    return pl.BlockSpec((tm, width), lambda t, *_: (t, 0))


def _full_spec(shape):
    zeros = (0,) * len(shape)
    return pl.BlockSpec(shape, lambda *_: zeros)


def _sgu_kernel(x_ref, mod_ref, g_ref, win_ref, vg_ref, ws_ref, bs_ref, wout_ref, o_ref):
    x = x_ref[...]
    mod = mod_ref[0]
    tm = x.shape[0]
    width = vg_ref.shape[1]
    gdim = width // SGU_GROUPS
    h = _modulate(x, g_ref[...], mod[0:1], mod[1:2]).astype(BF16)
    u = _gelu(_dot(h, win_ref[:, :width]))
    v = _gelu(_dot(h, win_ref[:, width:]))
    vn = (_rms(v) * vg_ref[...]).astype(BF16)
    rows = []
    for c in range(tm // SGU_CHUNK):
        cols = []
        for g in range(SGU_GROUPS):
            vv = vn[c * SGU_CHUNK:(c + 1) * SGU_CHUNK, g * gdim:(g + 1) * gdim]
            cols.append(_dot(ws_ref[g], vv) + bs_ref[:, g:g + 1])
        rows.append(jnp.concatenate(cols, axis=1))
    sv = jnp.concatenate(rows, axis=0)
    y = _dot((u * sv).astype(BF16), wout_ref[...])
    o_ref[...] = x + mod[2:3] * y


def _sgu_layer(st, xs, mods, gain, w_in, v_gain, w_s, b_s, w_out, with_ctx):
    tm, d = TOKEN_TILE, st.d
    width = v_gain.shape[0]
    return pl.pallas_call(
        _sgu_kernel,
        grid=(st.tiles(tm, with_ctx),),
        in_specs=[
            _tok_spec(tm, d), st.mod_spec(tm), _full_spec((1, d)),
            _full_spec((d, 2 * width)), _full_spec((1, width)),
            _full_spec((SGU_GROUPS, SGU_CHUNK, SGU_CHUNK)), _full_spec((SGU_CHUNK, SGU_GROUPS)),
            _full_spec((width, d)),
        ],
        out_specs=_tok_spec(tm, d),
        out_shape=jax.ShapeDtypeStruct((st.tiles(tm, with_ctx) * tm, d), F32),
        compiler_params=_cparams("parallel"),
        name="sgu_mixer",
    )(xs, mods, gain.reshape(1, d), w_in.astype(BF16), v_gain.reshape(1, width),
      w_s.astype(BF16), b_s.T, w_out.astype(BF16))


POOL_HALO = 8


def _pool_in_kernel(x_ref, mod_ref, g_ref, win_ref, z_ref):
    mod = mod_ref[0]
    h = _modulate(x_ref[...], g_ref[...], mod[0:1], mod[1:2]).astype(BF16)
    z_ref[...] = _dot(h, win_ref[...])


def _pool_out_kernel(z_ref, zp_ref, zn_ref, x_ref, mod_ref, wg_ref, sc_ref, wout_ref, o_ref,
                     *, n_lat_tiles, lat_tiles_per_seq, ctx_tiles_per_seq, seq, ctx_len):
    t = pl.program_id(0)
    tm = z_ref.shape[0]
    is_lat = t < n_lat_tiles
    per_seq = jnp.where(is_lat, lat_tiles_per_seq, ctx_tiles_per_seq)
    pos_tile = jnp.where(is_lat, t, t - n_lat_tiles) % per_seq
    length = jnp.where(is_lat, seq, ctx_len)
    has_prev = (pos_tile > 0).astype(F32)
    has_next = (pos_tile < per_seq - 1).astype(F32)
    z = z_ref[...]
    zext = jnp.concatenate([zp_ref[...] * has_prev, z, zn_ref[...] * has_next], axis=0)
    pos = (pos_tile * tm + lax.broadcasted_iota(jnp.int32, (tm, 1), 0))
    gdim = wg_ref.shape[1]
    n_ext = tm + 2 * POOL_HALO

    def shift_up(a, k):
        return a if k == 0 else pltpu.roll(a, n_ext - k, 0)

    outs = []
    for g, w in enumerate(POOL_WINDOWS):
        half = w // 2
        acc = zext[:, g * gdim:(g + 1) * gdim]
        span = 1
        while span < w:
            acc = acc + shift_up(acc, span)
            span *= 2
        win_sum = shift_up(acc, POOL_HALO - half)[:tm]
        cnt = (jnp.minimum(pos + half, length) - jnp.maximum(pos - half, 0)).astype(F32)
        pooled = win_sum / cnt - z[:, g * gdim:(g + 1) * gdim]
        outs.append(_dot(pooled.astype(BF16), wg_ref[g]))
    y = (jnp.concatenate(outs, axis=1) * sc_ref[...]).astype(BF16)
    o_ref[...] = x_ref[...] + mod_ref[0][2:3] * _dot(y, wout_ref[...])


def _pool_layer(st, xs, mods, gain, w_in, w_grp, scale, w_out, with_ctx):
    tm, d = TOKEN_TILE, st.d
    width = w_in.shape[1]
    n_tiles = st.tiles(tm, with_ctx)
    z = pl.pallas_call(
        _pool_in_kernel,
        grid=(n_tiles,),
        in_specs=[_tok_spec(tm, d), st.mod_spec(tm), _full_spec((1, d)), _full_spec((d, width))],
        out_specs=_tok_spec(tm, width),
        out_shape=jax.ShapeDtypeStruct((n_tiles * tm, width), F32),
        compiler_params=_cparams("parallel"),
        name="pool_in",
    )(xs, mods, gain.reshape(1, d), w_in.astype(BF16))
    halo_per_tile = tm // POOL_HALO
    last_halo = n_tiles * halo_per_tile - 1
    kern = functools.partial(
        _pool_out_kernel, n_lat_tiles=st.n_lat // tm, lat_tiles_per_seq=st.seq // tm,
        ctx_tiles_per_seq=st.ctx_len // tm, seq=st.seq, ctx_len=st.ctx_len)
    return pl.pallas_call(
        kern,
        grid=(n_tiles,),
        in_specs=[
            _tok_spec(tm, width),
            pl.BlockSpec((POOL_HALO, width), lambda t: (jnp.maximum(t * halo_per_tile - 1, 0), 0)),
            pl.BlockSpec((POOL_HALO, width), lambda t: (jnp.minimum((t + 1) * halo_per_tile, last_halo), 0)),
            _tok_spec(tm, d), st.mod_spec(tm),
            _full_spec(w_grp.shape), _full_spec((1, width)), _full_spec((width, d)),
        ],
        out_specs=_tok_spec(tm, d),
        out_shape=jax.ShapeDtypeStruct((st.tiles(tm, with_ctx) * tm, d), F32),
        compiler_params=_cparams("parallel"),
        name="pool_out",
    )(z, z, z, xs, mods, w_grp.astype(BF16), scale.reshape(1, width), w_out.astype(BF16))


def _rope_tables(seq, tm):
    pos = jnp.arange(seq)
    axis_dim = HEAD_DIM // 2
    freqs = ROPE_THETA ** (-jnp.arange(0, axis_dim, 2, dtype=F32) / axis_dim)
    ang = jnp.stack([pos // GRID_W, pos % GRID_W], axis=-1).astype(F32)[:, :, None] * freqs
    cos, sin = jnp.cos(ang), jnp.sin(ang)
    zero = jnp.zeros_like(sin)
    c = jnp.concatenate([cos, cos], axis=-1).reshape(seq, HEAD_DIM)
    s_lo = jnp.concatenate([-sin, zero], axis=-1).reshape(seq, HEAD_DIM)
    s_hi = jnp.concatenate([zero, sin], axis=-1).reshape(seq, HEAD_DIM)
    pad = jnp.zeros((tm, HEAD_DIM), F32)
    return (jnp.concatenate([c, pad + 1.0]), jnp.concatenate([s_lo, pad]), jnp.concatenate([s_hi, pad]))


def _qkv_kernel(x_ref, mod_ref, g_ref, w_ref, qg_ref, kg_ref, c_ref, slo_ref, shi_ref, q_ref, k_ref, v_ref):
    mod = mod_ref[0]
    h = _modulate(x_ref[...], g_ref[...], mod[0:1], mod[1:2]).astype(BF16)
    qkv = _dot(h, w_ref[...])
    nq, nk = q_ref.shape[1], k_ref.shape[1]
    cos, s_lo, s_hi = c_ref[...], slo_ref[...], shi_ref[...]
    quarter = HEAD_DIM // 4

    def norm_rope(t, gain):
        t = _rms(t) * gain
        return (t * cos + pltpu.roll(t, HEAD_DIM - quarter, 1) * s_lo + pltpu.roll(t, quarter, 1) * s_hi)

    for hh in range(nq // HEAD_DIM):
        sl = slice(hh * HEAD_DIM, (hh + 1) * HEAD_DIM)
        q_ref[:, sl] = norm_rope(qkv[:, sl], qg_ref[...]).astype(BF16)
    for hh in range(nk // HEAD_DIM):
        sl = slice(hh * HEAD_DIM, (hh + 1) * HEAD_DIM)
        k_ref[:, sl] = norm_rope(qkv[:, nq + hh * HEAD_DIM:nq + (hh + 1) * HEAD_DIM], kg_ref[...]).astype(BF16)
    v_ref[...] = qkv[:, nq + nk:].astype(BF16)


def _attn_kernel(q_ref, kl_ref, vl_ref, kc_ref, vc_ref, x_ref, mod_ref, wo_ref, o_ref, *, n_lat_tiles):
    n_heads = q_ref.shape[1] // HEAD_DIM
    q_per_kv = n_heads // N_KV_HEADS
    scale = HEAD_DIM ** -0.5

    def body(with_lat):
        outs = []
        for hh in range(n_heads):
            kv = slice((hh // q_per_kv) * HEAD_DIM, (hh // q_per_kv + 1) * HEAD_DIM)
            q = q_ref[:, hh * HEAD_DIM:(hh + 1) * HEAD_DIM]
            s_c = _dot_nt(q, kc_ref[:, kv]) * scale
            m = jnp.max(s_c, axis=-1, keepdims=True)
            if with_lat:
                s_l = _dot_nt(q, kl_ref[:, kv]) * scale
                m = jnp.maximum(m, jnp.max(s_l, axis=-1, keepdims=True))
            p_c = jnp.exp(s_c - m)
            den = jnp.sum(p_c, axis=-1, keepdims=True)
            o = _dot(p_c.astype(BF16), vc_ref[:, kv])
            if with_lat:
                p_l = jnp.exp(s_l - m)
                den = den + jnp.sum(p_l, axis=-1, keepdims=True)
                o = o + _dot(p_l.astype(BF16), vl_ref[:, kv])
            outs.append(o / den)
        y = _dot(jnp.concatenate(outs, axis=1).astype(BF16), wo_ref[...])
        o_ref[...] = x_ref[...] + mod_ref[0][2:3] * y

    t = pl.program_id(0)
    pl.when(t < n_lat_tiles)(lambda: body(True))
    pl.when(t >= n_lat_tiles)(lambda: body(False))


def _attn_layer(st, xs, mods, gain, w_qkv, q_gain, k_gain, w_o, with_ctx):
    d = st.d
    tm = st.ctx_len
    assert st.seq % tm == 0
    n_all = st.n_tok // tm
    n_lat_tiles, per_seq = st.n_lat // tm, st.seq // tm
    nq = w_o.shape[0]
    nk = (w_qkv.shape[1] - nq) // 2
    cos, s_lo, s_hi = _rope_tables(st.seq, tm)
    rope_spec = pl.BlockSpec((tm, HEAD_DIM), lambda t: (jnp.where(t < n_lat_tiles, t % per_seq, per_seq), 0))
    q, k, v = pl.pallas_call(
        _qkv_kernel,
        grid=(n_all,),
        in_specs=[
            _tok_spec(tm, d), st.mod_spec(tm), _full_spec((1, d)), _full_spec(w_qkv.shape),
            _full_spec((1, HEAD_DIM)), _full_spec((1, HEAD_DIM)), rope_spec, rope_spec, rope_spec,
        ],
        out_specs=[_tok_spec(tm, nq), _tok_spec(tm, nk), _tok_spec(tm, nk)],
        out_shape=[jax.ShapeDtypeStruct((st.n_tok, nq), BF16), jax.ShapeDtypeStruct((st.n_tok, nk), BF16),
                   jax.ShapeDtypeStruct((st.n_tok, nk), BF16)],
        compiler_params=_cparams("parallel"),
        name="attn_qkv",
    )(xs, mods, gain.reshape(1, d), w_qkv.astype(BF16), q_gain.reshape(1, HEAD_DIM),
      k_gain.reshape(1, HEAD_DIM), cos, s_lo, s_hi)

    bsz = st.bsz
    batch_of = lambda t: jnp.where(t < n_lat_tiles, t // per_seq, t - n_lat_tiles)
    lat_spec = pl.BlockSpec((st.seq, nk), lambda t: (jnp.minimum(t // per_seq, bsz - 1), 0))
    ctx_spec = pl.BlockSpec((tm, nk), lambda t: (n_lat_tiles + batch_of(t), 0))
    return pl.pallas_call(
        functools.partial(_attn_kernel, n_lat_tiles=n_lat_tiles),
        grid=(st.tiles(tm, with_ctx),),
        in_specs=[_tok_spec(tm, nq), lat_spec, lat_spec, ctx_spec, ctx_spec,
                  _tok_spec(tm, d), st.mod_spec(tm), _full_spec((nq, d))],
        out_specs=_tok_spec(tm, d),
        out_shape=jax.ShapeDtypeStruct((st.tiles(tm, with_ctx) * tm, d), F32),
        compiler_params=_cparams("parallel"),
        name="attn_core",
    )(q, k, v, k, v, xs, mods, w_o.astype(BF16))


_INT_MIN = -2 ** 31


def _sortable(x):
    b = lax.bitcast_convert_type(x, jnp.int32)
    return b ^ ((b >> 31) & 0x7FFFFFFF)


def _unsortable(k):
    return lax.bitcast_convert_type(k ^ ((k >> 31) & 0x7FFFFFFF), F32)


def _extract16(k, iota, exact):
    n = k.shape[0]
    tops = []
    for a in range(PEER_TOPK):
        m = jnp.max(k, axis=0, keepdims=True)
        hit = k == m
        if exact:
            first = jnp.min(jnp.where(hit, iota, n), axis=0, keepdims=True)
            hit = iota == first
        k = jnp.where(hit, _INT_MIN + a, k)
        tops.append(m)
    return k, tops


def _rows16(tops, iota16):
    out = jnp.zeros(iota16.shape, tops[0].dtype)
    for a, t in enumerate(tops):
        out = jnp.where(iota16 == a, t, out)
    return out


_CAND_HALF = PEER_TOPK // 2


def _peer_select(s1, s2, exact):
    n, width = s1.shape
    topk = float(PEER_TOPK)
    iota = lax.broadcasted_iota(jnp.int32, (n, width), 0)
    iota16 = iota[0:PEER_TOPK]
    mark_end = _INT_MIN + PEER_TOPK
    k1, top1 = _extract16(_sortable(s1), iota, exact)
    k2, top2 = _extract16(_sortable(s2), iota, exact)
    v1 = _unsortable(_rows16(top1, iota16))
    v2 = _unsortable(_rows16(top2, iota16))
    f1 = [_unsortable(t) for t in top1[:_CAND_HALF]]
    f2_0 = _unsortable(top2[0])
    pieces = [v2 + f1[0]]
    pieces += [v2[0:_CAND_HALF] + f1[a] for a in range(1, _CAND_HALF)]
    pieces.append(v1[_CAND_HALF:] + f2_0)
    cand = jnp.concatenate(pieces, axis=0)
    nc = cand.shape[0]
    kc, _ = _extract16(_sortable(cand), iota[0:nc], exact)
    picked = kc < mark_end
    sel = jnp.where(picked, 1.0, 0.0)
    z = jnp.sum(jnp.where(picked, jnp.exp(cand - (f1[0] + f2_0)), 0.0), axis=0, keepdims=True)
    cnt = [jnp.sum(sel[0:PEER_TOPK], axis=0, keepdims=True)]
    for a in range(1, _CAND_HALF):
        row = PEER_TOPK + (a - 1) * _CAND_HALF
        cnt.append(jnp.sum(sel[row:row + _CAND_HALF], axis=0, keepdims=True))
    tail = sel[nc - _CAND_HALF:]
    for a in range(_CAND_HALF, PEER_TOPK):
        pick = iota[0:_CAND_HALF] == a - _CAND_HALF
        cnt.append(jnp.sum(jnp.where(pick, tail, 0.0), axis=0, keepdims=True))
    cut = jnp.zeros(s1.shape, F32)
    for a in range(PEER_TOPK):
        cut = jnp.where(k1 == _INT_MIN + a, cnt[a], cut)
    in1, in2 = k1 < mark_end, k2 < mark_end
    r2 = jnp.where(in2, (k2 - _INT_MIN).astype(F32), topk)
    e1 = jnp.exp(s1 - f1[0])
    e2 = jnp.exp(s2 - f2_0) / z
    n1 = jnp.sum(jnp.where(in1, 1.0, 0.0), axis=0, keepdims=True)
    n2 = jnp.sum(jnp.where(in2, 1.0, 0.0), axis=0, keepdims=True)
    nsel = jnp.sum(sel, axis=0, keepdims=True)
    bad = jnp.where((n1 != topk) | (n2 != topk) | (nsel != topk), 1.0, 0.0)
    return cut, e1, r2, e2, bad


def _pack_rows(x):
    return pltpu.bitcast(x.astype(BF16), jnp.uint32)


def _unpack_rows(x):
    return pltpu.bitcast(x, BF16)


def _pack_twice(x):
    u = lax.bitcast_convert_type(x.astype(BF16).astype(F32), jnp.uint32)
    return u | (u >> 16)


def _pack_pairs(w):
    n2, m = w.shape
    pairs = jnp.swapaxes(w.astype(BF16).reshape(n2 // 2, 2, m), 1, 2)
    return lax.bitcast_convert_type(pairs, jnp.uint32)


def _peer_select_kernel(x_ref, mod_ref, g_ref, wq_ref, k1_ref, k2_ref,
                        h_ref, cut_ref, e1_ref, r2_ref, e2_ref, s1_scr, s2_scr):
    mod = mod_ref[0]
    f = _modulate(x_ref[...], g_ref[...], mod[3:4], mod[4:5]).astype(BF16)
    h_ref[...] = _pack_rows(f)
    tm = f.shape[0]
    n_heads, n_keys, half = k1_ref.shape
    q_t = _dot_nt(_unpack_rows(wq_ref[...]), f)
    for hh in range(n_heads):
        base = hh * 2 * half
        s1_scr[hh] = _dot(k1_ref[hh], q_t[base:base + half].astype(BF16))
        s2_scr[hh] = _dot(k2_ref[hh], q_t[base + half:base + 2 * half].astype(BF16))
    n_sub = tm // LANES

    def step(idx, carry):
        hh = idx // n_sub
        lanes = pl.ds(pl.multiple_of((idx % n_sub) * LANES, LANES), LANES)
        s1, s2 = s1_scr[hh, :, lanes], s2_scr[hh, :, lanes]

        def store(cut, e1, r2, e2):
            cut_ref[hh, :, lanes] = _pack_twice(cut)
            e1_ref[hh, :, lanes] = _pack_twice(e1)
            r2_ref[hh, :, lanes] = _pack_rows(r2)
            e2_ref[hh, :, lanes] = _pack_rows(e2)

        *fast, bad = _peer_select(s1, s2, exact=False)
        store(*fast)

        @pl.when(jnp.max(bad) > 0.0)
        def _():
            store(*_peer_select(s1, s2, exact=True)[:4])

        return carry

    lax.fori_loop(0, n_heads * n_sub, step, 0)


def _peer_main_kernel(h_ref, u_ref, vt_ref, cut_ref, e1_ref, r2_ref, e2_ref, x_ref, mod_ref, fg_ref,
                      o_ref, acc, act0, act1, w0, w1, *, final_norm):
    c = pl.program_id(1)
    ec, tm = act0.shape
    n_heads, n_keys = r2_ref.shape[0], 2 * r2_ref.shape[1]
    rows_per_chunk = ec // n_keys
    half_keys = n_keys // 2
    packed_half = half_keys // 2

    @pl.when(c == 0)
    def _():
        acc[...] = jnp.zeros_like(acc)
        act1[...] = jnp.zeros_like(act1)
        w0[...] = jnp.zeros_like(w0)

    def gate_chunk(cut_ref, e1_ref, act_ref, w_ref):
        zero = jnp.zeros((half_keys, LANES), BF16)
        for part in range(n_keys // half_keys):
            prow = slice(part * packed_half, (part + 1) * packed_half)
            for tl in range(tm // LANES):
                lanes = slice(tl * LANES, (tl + 1) * LANES)
                gates = [zero] * rows_per_chunk
                for hh in range(n_heads):
                    r2 = _unpack_rows(r2_ref[hh, prow, lanes])
                    e2 = _unpack_rows(e2_ref[hh, prow, lanes])
                    for il in range(rows_per_chunk):
                        cut = _unpack_rows(jnp.broadcast_to(cut_ref[hh, il:il + 1, lanes], (packed_half, LANES)))
                        e1 = _unpack_rows(jnp.broadcast_to(e1_ref[hh, il:il + 1, lanes], (packed_half, LANES)))
                        gates[il] = gates[il] + jnp.where(r2 < cut, e2, zero) * e1
                for il in range(rows_per_chunk):
                    rows = slice(il * n_keys + part * half_keys, il * n_keys + (part + 1) * half_keys)
                    w_ref[rows, lanes] = _gelu(act_ref[rows, lanes]).astype(BF16) * gates[il]

    def stage(act_new, act_old, w_new, w_old):
        act_new[...] = _dot_nt(_unpack_rows(u_ref[...]), _unpack_rows(h_ref[...]))
        gate_chunk(cut_ref, e1_ref, act_old, w_new)
        acc[...] += _dot(_unpack_rows(vt_ref[...]), w_old[...])

    pl.when(c % 2 == 0)(lambda: stage(act0, act1, w1, w0))
    pl.when(c % 2 == 1)(lambda: stage(act1, act0, w0, w1))

    @pl.when(c == pl.num_programs(1) - 1)
    def _():
        y = x_ref[...] + mod_ref[0][5:6] * acc[...].T
        if final_norm:
            y = _rms(y) * fg_ref[...]
        o_ref[...] = y


def _peer_layer(st, xs, mods, gain, w_q, k1, k2, u_tab, v_tab, final_gain, with_ctx, final_norm):
    tm, d = PEER_TILE, st.d
    n_heads, n_keys, half = k1.shape
    n_exp = u_tab.shape[0]
    ec = PEER_EXPERT_CHUNK
    n_tiles = st.tiles(tm, with_ctx)
    n_rows = n_tiles * tm
    u32 = jnp.uint32
    key_spec = pl.BlockSpec((n_heads, n_keys, tm), lambda t, *_: (0, 0, t))
    key_shape = jax.ShapeDtypeStruct((n_heads, n_keys, n_rows), u32)
    pair_spec = pl.BlockSpec((n_heads, n_keys // 2, tm), lambda t, *_: (0, 0, t))
    pair_shape = jax.ShapeDtypeStruct((n_heads, n_keys // 2, n_rows), u32)
    h, cut, e1, r2, e2 = pl.pallas_call(
        _peer_select_kernel,
        grid=(n_tiles,),
        in_specs=[_tok_spec(tm, d), st.mod_spec(tm), _full_spec((1, d)), _full_spec((w_q.shape[1] // 2, d)),
                  _full_spec(k1.shape), _full_spec(k2.shape)],
        out_specs=[_tok_spec(tm // 2, d), key_spec, key_spec, pair_spec, pair_spec],
        out_shape=[jax.ShapeDtypeStruct((n_rows // 2, d), u32), key_shape, key_shape, pair_shape, pair_shape],
        scratch_shapes=[pltpu.VMEM((n_heads, n_keys, tm), F32), pltpu.VMEM((n_heads, n_keys, tm), F32)],
        compiler_params=_cparams("parallel"),
        name="peer_select",
    )(xs, mods, gain.reshape(1, d), _pack_pairs(w_q.T), k1.astype(BF16), k2.astype(BF16))

    n_chunks = n_exp // ec
    last = n_chunks - 1
    rows_per_chunk = ec // n_keys
    sub1_spec = pl.BlockSpec((n_heads, rows_per_chunk, tm), lambda t, c: (0, jnp.clip(c - 1, 0, last), t))
    return pl.pallas_call(
        functools.partial(_peer_main_kernel, final_norm=final_norm),
        grid=(n_tiles, n_chunks + 2),
        in_specs=[
            _tok_spec(tm // 2, d),
            pl.BlockSpec((ec // 2, d), lambda t, c: (jnp.minimum(c, last), 0)),
            pl.BlockSpec((d // 2, ec), lambda t, c: (0, jnp.clip(c - 2, 0, last))),
            sub1_spec, sub1_spec, pair_spec, pair_spec,
            _tok_spec(tm, d), st.mod_spec(tm), _full_spec((1, d)),
        ],
        out_specs=_tok_spec(tm, d),
        out_shape=jax.ShapeDtypeStruct((n_rows, d), F32),
        scratch_shapes=[pltpu.VMEM((d, tm), F32),
                        pltpu.VMEM((ec, tm), F32), pltpu.VMEM((ec, tm), F32),
                        pltpu.VMEM((ec, tm), BF16), pltpu.VMEM((ec, tm), BF16)],
        compiler_params=_cparams("parallel", "arbitrary"),
        name="peer_main",
    )(h, _pack_pairs(u_tab), _pack_pairs(v_tab.T), cut, e1, r2, e2, xs, mods, final_gain.reshape(1, d))


def kernel(x, c, ctx, c_ctx, ada_w, ada_b, norm1_g, norm2_g, sgu_w_in, sgu_v_gain, sgu_w_s, sgu_b_s, sgu_w_out, pool_w_in, pool_w_grp, pool_scale, pool_w_out, attn_w_qkv, attn_q_gain, attn_k_gain, attn_w_o, peer_w_q, peer_k1, peer_k2, peer_u, peer_v, final_gain):
    bsz, seq, d = x.shape
    ctx_len = ctx.shape[1]
    depth = ada_w.shape[0]
    st = _Stream(bsz, seq, ctx_len, d)
    assert bsz + 1 <= MOD_ROWS and seq % PEER_TILE == 0 and (bsz * ctx_len) % PEER_TILE == 0
    assert seq % TOKEN_TILE == 0 and ctx_len % TOKEN_TILE == 0

    cond = jnp.concatenate([c, c_ctx[None, :], jnp.zeros((MOD_ROWS - bsz - 1, d), F32)], axis=0)
    mods = _ada_mods(cond, ada_w, ada_b)
    xs = jnp.concatenate([x.reshape(bsz * seq, d), ctx.reshape(bsz * ctx_len, d)], axis=0)

    for i in range(depth):
        kind, j = i % N_MIXERS, i // N_MIXERS
        with_ctx = i < depth - 1
        if kind == 0:
            xs = _sgu_layer(st, xs, mods[i], norm1_g[i], sgu_w_in[j], sgu_v_gain[j], sgu_w_s[j], sgu_b_s[j],
                            sgu_w_out[j], with_ctx)
        elif kind == 1:
            xs = _pool_layer(st, xs, mods[i], norm1_g[i], pool_w_in[j], pool_w_grp[j], pool_scale[j],
                             pool_w_out[j], with_ctx)
        else:
            xs = _attn_layer(st, xs, mods[i], norm1_g[i], attn_w_qkv[j], attn_q_gain[j], attn_k_gain[j],
                             attn_w_o[j], with_ctx)
        xs = _peer_layer(st, xs, mods[i], norm2_g[i], peer_w_q[i], peer_k1[i], peer_k2[i], peer_u[i],
                         peer_v[i], final_gain, with_ctx, final_norm=(i == depth - 1))
    return xs[:bsz * seq].reshape(bsz, seq, d)
```

```python
import functools
import math

import jax
import jax.numpy as jnp
from jax import lax
from jax.experimental import pallas as pl
from jax.experimental.pallas import tpu as pltpu

F32 = jnp.float32
BF16 = jnp.bfloat16

NORM_EPS = 1e-6
MOD_CHUNKS = 6
GRID_W = 64
SGU_CHUNK = 128
SGU_GROUPS = 8
POOL_WINDOWS = (2, 4, 8, 16)
HEAD_DIM = 128
N_KV_HEADS = 2
ROPE_THETA = 10000.0
PEER_HEADS = 8
PEER_KEYS = 128
PEER_TOPK = 16
N_MIXERS = 3

LANES = 128
MOD_ROWS = 16
VMEM_LIMIT = 56 * 1024 * 1024

TOKEN_TILE = 256
PEER_TILE = 512
PEER_EXPERT_CHUNK = 1024
SELECT_UNROLL = 2


def _cparams(*sem):
    return pltpu.CompilerParams(dimension_semantics=sem, vmem_limit_bytes=VMEM_LIMIT)


def _rms(x):
    return x * lax.rsqrt(jnp.mean(x * x, axis=-1, keepdims=True) + NORM_EPS)


def _modulate(x, gain, shift, scale):
    return _rms(x) * gain * (1.0 + scale) + shift


def _gelu(x):
    return 0.5 * x * (1.0 + lax.erf(x * (1.0 / math.sqrt(2.0))))


def _dot(a, b):
    return jnp.dot(a, b, preferred_element_type=F32)


def _dot_nt(a, b):
    return lax.dot_general(a, b, (((1,), (1,)), ((), ())), preferred_element_type=F32)


def _ada_kernel(c_ref, w_ref, b_ref, o_ref):
    c = c_ref[...]
    a = c * (1.0 / (1.0 + jnp.exp(-c)))
    o_ref[0] = _dot(a.astype(BF16), w_ref[0].astype(BF16)) + b_ref[0]


def _ada_mods(cond, ada_w, ada_b):
    depth, d, n = ada_w.shape
    tn = n // 4
    out = pl.pallas_call(
        _ada_kernel,
        grid=(depth, n // tn),
        in_specs=[
            pl.BlockSpec((MOD_ROWS, d), lambda l, j: (0, 0)),
            pl.BlockSpec((1, d, tn), lambda l, j: (l, 0, j)),
            pl.BlockSpec((1, 1, tn), lambda l, j: (l, 0, j)),
        ],
        out_specs=pl.BlockSpec((1, MOD_ROWS, tn), lambda l, j: (l, 0, j)),
        out_shape=jax.ShapeDtypeStruct((depth, MOD_ROWS, n), F32),
        compiler_params=_cparams("parallel", "parallel"),
        name="ada_mods",
    )(cond, ada_w, ada_b.reshape(depth, 1, n))
    return out.reshape(depth, MOD_ROWS, MOD_CHUNKS, d)


class _Stream:
    def __init__(self, bsz, seq, ctx_len, d):
        self.bsz, self.seq, self.ctx_len, self.d = bsz, seq, ctx_len, d
        self.n_lat = bsz * seq
        self.n_tok = bsz * (seq + ctx_len)

    def tiles(self, tm, with_ctx):
        return (self.n_tok if with_ctx else self.n_lat) // tm

    def mod_row(self, tm):
        n_lat_tiles, per_seq, bsz = self.n_lat // tm, self.seq // tm, self.bsz
        return lambda t: jnp.where(t < n_lat_tiles, t // per_seq, bsz)

    def mod_spec(self, tm):
        row = self.mod_row(tm)
        return pl.BlockSpec((1, MOD_CHUNKS, self.d), lambda t, *_: (row(t), 0, 0))


def _tok_spec(tm, width):
    return pl.BlockSpec((tm, width), lambda t, *_: (t, 0))


def _full_spec(shape):
    zeros = (0,) * len(shape)
    return pl.BlockSpec(shape, lambda *_: zeros)


def _sgu_kernel(x_ref, mod_ref, g_ref, win_ref, vg_ref, ws_ref, bs_ref, wout_ref, o_ref):
    x = x_ref[...]
    mod = mod_ref[0]
    tm = x.shape[0]
    width = vg_ref.shape[1]
    gdim = width // SGU_GROUPS
    h = _modulate(x, g_ref[...], mod[0:1], mod[1:2]).astype(BF16)
    u = _gelu(_dot(h, win_ref[:, :width]))
    v = _gelu(_dot(h, win_ref[:, width:]))
    vn = (_rms(v) * vg_ref[...]).astype(BF16)
    rows = []
    for c in range(tm // SGU_CHUNK):
        cols = []
        for g in range(SGU_GROUPS):
            vv = vn[c * SGU_CHUNK:(c + 1) * SGU_CHUNK, g * gdim:(g + 1) * gdim]
            cols.append(_dot(ws_ref[g], vv) + bs_ref[:, g:g + 1])
        rows.append(jnp.concatenate(cols, axis=1))
    sv = jnp.concatenate(rows, axis=0)
    y = _dot((u * sv).astype(BF16), wout_ref[...])
    o_ref[...] = x + mod[2:3] * y


def _sgu_layer(st, xs, mods, gain, w_in, v_gain, w_s, b_s, w_out, with_ctx):
    tm, d = TOKEN_TILE, st.d
    width = v_gain.shape[0]
    return pl.pallas_call(
        _sgu_kernel,
        grid=(st.tiles(tm, with_ctx),),
        in_specs=[
            _tok_spec(tm, d), st.mod_spec(tm), _full_spec((1, d)),
            _full_spec((d, 2 * width)), _full_spec((1, width)),
            _full_spec((SGU_GROUPS, SGU_CHUNK, SGU_CHUNK)), _full_spec((SGU_CHUNK, SGU_GROUPS)),
            _full_spec((width, d)),
        ],
        out_specs=_tok_spec(tm, d),
        out_shape=jax.ShapeDtypeStruct((st.tiles(tm, with_ctx) * tm, d), F32),
        compiler_params=_cparams("parallel"),
        name="sgu_mixer",
    )(xs, mods, gain.reshape(1, d), w_in.astype(BF16), v_gain.reshape(1, width),
      w_s.astype(BF16), b_s.T, w_out.astype(BF16))


POOL_HALO = 8


def _pool_in_kernel(x_ref, mod_ref, g_ref, win_ref, z_ref):
    mod = mod_ref[0]
    h = _modulate(x_ref[...], g_ref[...], mod[0:1], mod[1:2]).astype(BF16)
    z_ref[...] = _dot(h, win_ref[...])


def _pool_out_kernel(z_ref, zp_ref, zn_ref, x_ref, mod_ref, wg_ref, sc_ref, wout_ref, o_ref,
                     *, n_lat_tiles, lat_tiles_per_seq, ctx_tiles_per_seq, seq, ctx_len):
    t = pl.program_id(0)
    tm = z_ref.shape[0]
    is_lat = t < n_lat_tiles
    per_seq = jnp.where(is_lat, lat_tiles_per_seq, ctx_tiles_per_seq)
    pos_tile = jnp.where(is_lat, t, t - n_lat_tiles) % per_seq
    length = jnp.where(is_lat, seq, ctx_len)
    has_prev = (pos_tile > 0).astype(F32)
    has_next = (pos_tile < per_seq - 1).astype(F32)
    z = z_ref[...]
    zext = jnp.concatenate([zp_ref[...] * has_prev, z, zn_ref[...] * has_next], axis=0)
    pos = (pos_tile * tm + lax.broadcasted_iota(jnp.int32, (tm, 1), 0))
    gdim = wg_ref.shape[1]
    n_ext = tm + 2 * POOL_HALO

    def shift_up(a, k):
        return a if k == 0 else pltpu.roll(a, n_ext - k, 0)

    outs = []
    for g, w in enumerate(POOL_WINDOWS):
        half = w // 2
        acc = zext[:, g * gdim:(g + 1) * gdim]
        span = 1
        while span < w:
            acc = acc + shift_up(acc, span)
            span *= 2
        win_sum = shift_up(acc, POOL_HALO - half)[:tm]
        cnt = (jnp.minimum(pos + half, length) - jnp.maximum(pos - half, 0)).astype(F32)
        pooled = win_sum / cnt - z[:, g * gdim:(g + 1) * gdim]
        outs.append(_dot(pooled.astype(BF16), wg_ref[g]))
    y = (jnp.concatenate(outs, axis=1) * sc_ref[...]).astype(BF16)
    o_ref[...] = x_ref[...] + mod_ref[0][2:3] * _dot(y, wout_ref[...])


def _pool_layer(st, xs, mods, gain, w_in, w_grp, scale, w_out, with_ctx):
    tm, d = TOKEN_TILE, st.d
    width = w_in.shape[1]
    n_tiles = st.tiles(tm, with_ctx)
    z = pl.pallas_call(
        _pool_in_kernel,
        grid=(n_tiles,),
        in_specs=[_tok_spec(tm, d), st.mod_spec(tm), _full_spec((1, d)), _full_spec((d, width))],
        out_specs=_tok_spec(tm, width),
        out_shape=jax.ShapeDtypeStruct((n_tiles * tm, width), F32),
        compiler_params=_cparams("parallel"),
        name="pool_in",
    )(xs, mods, gain.reshape(1, d), w_in.astype(BF16))
    halo_per_tile = tm // POOL_HALO
    last_halo = n_tiles * halo_per_tile - 1
    kern = functools.partial(
        _pool_out_kernel, n_lat_tiles=st.n_lat // tm, lat_tiles_per_seq=st.seq // tm,
        ctx_tiles_per_seq=st.ctx_len // tm, seq=st.seq, ctx_len=st.ctx_len)
    return pl.pallas_call(
        kern,
        grid=(n_tiles,),
        in_specs=[
            _tok_spec(tm, width),
            pl.BlockSpec((POOL_HALO, width), lambda t: (jnp.maximum(t * halo_per_tile - 1, 0), 0)),
            pl.BlockSpec((POOL_HALO, width), lambda t: (jnp.minimum((t + 1) * halo_per_tile, last_halo), 0)),
            _tok_spec(tm, d), st.mod_spec(tm),
            _full_spec(w_grp.shape), _full_spec((1, width)), _full_spec((width, d)),
        ],
        out_specs=_tok_spec(tm, d),
        out_shape=jax.ShapeDtypeStruct((st.tiles(tm, with_ctx) * tm, d), F32),
        compiler_params=_cparams("parallel"),
        name="pool_out",
    )(z, z, z, xs, mods, w_grp.astype(BF16), scale.reshape(1, width), w_out.astype(BF16))


def _rope_tables(seq, tm):
    pos = jnp.arange(seq)
    axis_dim = HEAD_DIM // 2
    freqs = ROPE_THETA ** (-jnp.arange(0, axis_dim, 2, dtype=F32) / axis_dim)
    ang = jnp.stack([pos // GRID_W, pos % GRID_W], axis=-1).astype(F32)[:, :, None] * freqs
    cos, sin = jnp.cos(ang), jnp.sin(ang)
    zero = jnp.zeros_like(sin)
    c = jnp.concatenate([cos, cos], axis=-1).reshape(seq, HEAD_DIM)
    s_lo = jnp.concatenate([-sin, zero], axis=-1).reshape(seq, HEAD_DIM)
    s_hi = jnp.concatenate([zero, sin], axis=-1).reshape(seq, HEAD_DIM)
    pad = jnp.zeros((tm, HEAD_DIM), F32)
    return (jnp.concatenate([c, pad + 1.0]), jnp.concatenate([s_lo, pad]), jnp.concatenate([s_hi, pad]))


def _qkv_kernel(x_ref, mod_ref, g_ref, w_ref, qg_ref, kg_ref, c_ref, slo_ref, shi_ref, q_ref, k_ref, v_ref):
    mod = mod_ref[0]
    h = _modulate(x_ref[...], g_ref[...], mod[0:1], mod[1:2]).astype(BF16)
    qkv = _dot(h, w_ref[...])
    nq, nk = q_ref.shape[1], k_ref.shape[1]
    cos, s_lo, s_hi = c_ref[...], slo_ref[...], shi_ref[...]
    quarter = HEAD_DIM // 4

    def norm_rope(t, gain):
        t = _rms(t) * gain
        return (t * cos + pltpu.roll(t, HEAD_DIM - quarter, 1) * s_lo + pltpu.roll(t, quarter, 1) * s_hi)

    for hh in range(nq // HEAD_DIM):
        sl = slice(hh * HEAD_DIM, (hh + 1) * HEAD_DIM)
        q_ref[:, sl] = norm_rope(qkv[:, sl], qg_ref[...]).astype(BF16)
    for hh in range(nk // HEAD_DIM):
        sl = slice(hh * HEAD_DIM, (hh + 1) * HEAD_DIM)
        k_ref[:, sl] = norm_rope(qkv[:, nq + hh * HEAD_DIM:nq + (hh + 1) * HEAD_DIM], kg_ref[...]).astype(BF16)
    v_ref[...] = qkv[:, nq + nk:].astype(BF16)


def _attn_kernel(q_ref, kl_ref, vl_ref, kc_ref, vc_ref, x_ref, mod_ref, wo_ref, o_ref, *, n_lat_tiles):
    n_heads = q_ref.shape[1] // HEAD_DIM
    q_per_kv = n_heads // N_KV_HEADS
    scale = HEAD_DIM ** -0.5

    def body(with_lat):
        outs = []
        for hh in range(n_heads):
            kv = slice((hh // q_per_kv) * HEAD_DIM, (hh // q_per_kv + 1) * HEAD_DIM)
            q = q_ref[:, hh * HEAD_DIM:(hh + 1) * HEAD_DIM]
            s_c = _dot_nt(q, kc_ref[:, kv]) * scale
            m = jnp.max(s_c, axis=-1, keepdims=True)
            if with_lat:
                s_l = _dot_nt(q, kl_ref[:, kv]) * scale
                m = jnp.maximum(m, jnp.max(s_l, axis=-1, keepdims=True))
            p_c = jnp.exp(s_c - m)
            den = jnp.sum(p_c, axis=-1, keepdims=True)
            o = _dot(p_c.astype(BF16), vc_ref[:, kv])
            if with_lat:
                p_l = jnp.exp(s_l - m)
                den = den + jnp.sum(p_l, axis=-1, keepdims=True)
                o = o + _dot(p_l.astype(BF16), vl_ref[:, kv])
            outs.append(o / den)
        y = _dot(jnp.concatenate(outs, axis=1).astype(BF16), wo_ref[...])
        o_ref[...] = x_ref[...] + mod_ref[0][2:3] * y

    t = pl.program_id(0)
    pl.when(t < n_lat_tiles)(lambda: body(True))
    pl.when(t >= n_lat_tiles)(lambda: body(False))


def _attn_layer(st, xs, mods, gain, w_qkv, q_gain, k_gain, w_o, with_ctx):
    d = st.d
    tm = st.ctx_len
    assert st.seq % tm == 0
    n_all = st.n_tok // tm
    n_lat_tiles, per_seq = st.n_lat // tm, st.seq // tm
    nq = w_o.shape[0]
    nk = (w_qkv.shape[1] - nq) // 2
    cos, s_lo, s_hi = _rope_tables(st.seq, tm)
    rope_spec = pl.BlockSpec((tm, HEAD_DIM), lambda t: (jnp.where(t < n_lat_tiles, t % per_seq, per_seq), 0))
    q, k, v = pl.pallas_call(
        _qkv_kernel,
        grid=(n_all,),
        in_specs=[
            _tok_spec(tm, d), st.mod_spec(tm), _full_spec((1, d)), _full_spec(w_qkv.shape),
            _full_spec((1, HEAD_DIM)), _full_spec((1, HEAD_DIM)), rope_spec, rope_spec, rope_spec,
        ],
        out_specs=[_tok_spec(tm, nq), _tok_spec(tm, nk), _tok_spec(tm, nk)],
        out_shape=[jax.ShapeDtypeStruct((st.n_tok, nq), BF16), jax.ShapeDtypeStruct((st.n_tok, nk), BF16),
                   jax.ShapeDtypeStruct((st.n_tok, nk), BF16)],
        compiler_params=_cparams("parallel"),
        name="attn_qkv",
    )(xs, mods, gain.reshape(1, d), w_qkv.astype(BF16), q_gain.reshape(1, HEAD_DIM),
      k_gain.reshape(1, HEAD_DIM), cos, s_lo, s_hi)

    bsz = st.bsz
    batch_of = lambda t: jnp.where(t < n_lat_tiles, t // per_seq, t - n_lat_tiles)
    lat_spec = pl.BlockSpec((st.seq, nk), lambda t: (jnp.minimum(t // per_seq, bsz - 1), 0))
    ctx_spec = pl.BlockSpec((tm, nk), lambda t: (n_lat_tiles + batch_of(t), 0))
    return pl.pallas_call(
        functools.partial(_attn_kernel, n_lat_tiles=n_lat_tiles),
        grid=(st.tiles(tm, with_ctx),),
        in_specs=[_tok_spec(tm, nq), lat_spec, lat_spec, ctx_spec, ctx_spec,
                  _tok_spec(tm, d), st.mod_spec(tm), _full_spec((nq, d))],
        out_specs=_tok_spec(tm, d),
        out_shape=jax.ShapeDtypeStruct((st.tiles(tm, with_ctx) * tm, d), F32),
        compiler_params=_cparams("parallel"),
        name="attn_core",
    )(q, k, v, k, v, xs, mods, w_o.astype(BF16))


_INT_MIN = -2 ** 31


def _sortable(x):
    b = lax.bitcast_convert_type(x, jnp.int32)
    return b ^ ((b >> 31) & 0x7FFFFFFF)


def _unsortable(k):
    return lax.bitcast_convert_type(k ^ ((k >> 31) & 0x7FFFFFFF), F32)


def _extract16(k, iota, exact):
    n = k.shape[0]
    tops = []
    for a in range(PEER_TOPK):
        m = jnp.max(k, axis=0, keepdims=True)
        hit = k == m
        if exact:
            first = jnp.min(jnp.where(hit, iota, n), axis=0, keepdims=True)
            hit = iota == first
        k = jnp.where(hit, _INT_MIN + a, k)
        tops.append(m)
    return k, tops


def _rows16(tops, iota16):
    out = jnp.zeros(iota16.shape, tops[0].dtype)
    for a, t in enumerate(tops):
        out = jnp.where(iota16 == a, t, out)
    return out


_CAND_HALF = PEER_TOPK // 2


def _peer_select(s1, s2, exact):
    n, width = s1.shape
    topk = float(PEER_TOPK)
    iota = lax.broadcasted_iota(jnp.int32, (n, width), 0)
    iota16 = iota[0:PEER_TOPK]
    mark_end = _INT_MIN + PEER_TOPK
    k1, top1 = _extract16(_sortable(s1), iota, exact)
    k2, top2 = _extract16(_sortable(s2), iota, exact)
    v1 = _unsortable(_rows16(top1, iota16))
    v2 = _unsortable(_rows16(top2, iota16))
    f1 = [_unsortable(t) for t in top1[:_CAND_HALF]]
    f2_0 = _unsortable(top2[0])
    pieces = [v2 + f1[0]]
    pieces += [v2[0:_CAND_HALF] + f1[a] for a in range(1, _CAND_HALF)]
    pieces.append(v1[_CAND_HALF:] + f2_0)
    cand = jnp.concatenate(pieces, axis=0)
    nc = cand.shape[0]
    kc, _ = _extract16(_sortable(cand), iota[0:nc], exact)
    picked = kc < mark_end
    sel = jnp.where(picked, 1.0, 0.0)
    z = jnp.sum(jnp.where(picked, jnp.exp(cand - (f1[0] + f2_0)), 0.0), axis=0, keepdims=True)
    cnt = [jnp.sum(sel[0:PEER_TOPK], axis=0, keepdims=True)]
    for a in range(1, _CAND_HALF):
        row = PEER_TOPK + (a - 1) * _CAND_HALF
        cnt.append(jnp.sum(sel[row:row + _CAND_HALF], axis=0, keepdims=True))
    tail = sel[nc - _CAND_HALF:]
    for a in range(_CAND_HALF, PEER_TOPK):
        pick = iota[0:_CAND_HALF] == a - _CAND_HALF
        cnt.append(jnp.sum(jnp.where(pick, tail, 0.0), axis=0, keepdims=True))
    cut = jnp.zeros(s1.shape, F32)
    for a in range(PEER_TOPK):
        cut = jnp.where(k1 == _INT_MIN + a, cnt[a], cut)
    in1, in2 = k1 < mark_end, k2 < mark_end
    r2 = jnp.where(in2, (k2 - _INT_MIN).astype(F32), topk)
    e1 = jnp.exp(s1 - f1[0])
    e2 = jnp.exp(s2 - f2_0) / z
    n1 = jnp.sum(jnp.where(in1, 1.0, 0.0), axis=0, keepdims=True)
    n2 = jnp.sum(jnp.where(in2, 1.0, 0.0), axis=0, keepdims=True)
    nsel = jnp.sum(sel, axis=0, keepdims=True)
    bad = jnp.where((n1 != topk) | (n2 != topk) | (nsel != topk), 1.0, 0.0)
    return cut, e1, r2, e2, bad


def _pack_rows(x):
    return pltpu.bitcast(x.astype(BF16), jnp.uint32)


def _unpack_rows(x):
    return pltpu.bitcast(x, BF16)


def _pack_twice(x):
    u = lax.bitcast_convert_type(x.astype(BF16).astype(F32), jnp.uint32)
    return u | (u >> 16)


PACK_BLOCK = 1024


def _pack_kernel(w_ref, o_ref, *, transpose):
    w = w_ref[0]
    o_ref[0] = _pack_rows(w.T if transpose else w)


def _pack_weights(w, transpose):
    layers, r, c = w.shape
    pb = PACK_BLOCK
    out_rows, out_cols = (c, r) if transpose else (r, c)
    out_map = (lambda l, i, j: (l, j, i)) if transpose else (lambda l, i, j: (l, i, j))
    return pl.pallas_call(
        functools.partial(_pack_kernel, transpose=transpose),
        grid=(layers, r // pb, c // pb),
        in_specs=[pl.BlockSpec((1, pb, pb), lambda l, i, j: (l, i, j))],
        out_specs=pl.BlockSpec((1, pb // 2, pb), out_map),
        out_shape=jax.ShapeDtypeStruct((layers, out_rows // 2, out_cols), jnp.uint32),
        compiler_params=_cparams("parallel", "parallel", "parallel"),
        name="pack_t" if transpose else "pack",
    )(w)


def _peer_select_kernel(x_ref, mod_ref, g_ref, wq_ref, k1_ref, k2_ref,
                        h_ref, cut_ref, e1_ref, r2_ref, e2_ref, s1_scr, s2_scr):
    mod = mod_ref[0]
    f_t = _modulate(x_ref[...], g_ref[...], mod[3:4], mod[4:5]).T.astype(BF16)
    h_ref[...] = _pack_rows(f_t)
    tm = f_t.shape[1]
    n_heads, n_keys, half = k1_ref.shape
    q_t = _dot(_unpack_rows(wq_ref[...]), f_t)
    for hh in range(n_heads):
        base = hh * 2 * half
        s1_scr[hh] = _dot(k1_ref[hh], q_t[base:base + half].astype(BF16))
        s2_scr[hh] = _dot(k2_ref[hh], q_t[base + half:base + 2 * half].astype(BF16))
    n_sub = tm // LANES

    def step(idx, carry):
        hh = idx // (n_sub // SELECT_UNROLL)
        base = (idx % (n_sub // SELECT_UNROLL)) * SELECT_UNROLL
        bad_any = jnp.zeros((1, LANES), F32)
        work = []
        for sub in range(SELECT_UNROLL):
            lanes = pl.ds(pl.multiple_of((base + sub) * LANES, LANES), LANES)
            s1, s2 = s1_scr[hh, :, lanes], s2_scr[hh, :, lanes]
            *fast, bad = _peer_select(s1, s2, exact=False)
            work.append((lanes, s1, s2, fast))
            bad_any = jnp.maximum(bad_any, bad)

        def store(lanes, cut, e1, r2, e2):
            cut_ref[hh, :, lanes] = _pack_twice(cut)
            e1_ref[hh, :, lanes] = _pack_twice(e1)
            r2_ref[hh, :, lanes] = _pack_rows(r2)
            e2_ref[hh, :, lanes] = _pack_rows(e2)

        for lanes, _, _, fast in work:
            store(lanes, *fast)

        @pl.when(jnp.max(bad_any) > 0.0)
        def _():
            for lanes, s1, s2, _ in work:
                store(lanes, *_peer_select(s1, s2, exact=True)[:4])

        return carry

    lax.fori_loop(0, n_heads * n_sub // SELECT_UNROLL, step, 0)


def _peer_main_kernel(h_ref, u_ref, vt_ref, cut_ref, e1_ref, r2_ref, e2_ref, x_ref, mod_ref, fg_ref,
                      o_ref, acc, act0, act1, w0, w1, *, final_norm):
    c = pl.program_id(1)
    ec, tm = act0.shape
    n_heads, n_keys = r2_ref.shape[0], 2 * r2_ref.shape[1]
    rows_per_chunk = ec // n_keys
    half_keys = n_keys // 2
    packed_half = half_keys // 2

    @pl.when(c == 0)
    def _():
        acc[...] = jnp.zeros_like(acc)
        act1[...] = jnp.zeros_like(act1)
        w0[...] = jnp.zeros_like(w0)

    def gate_chunk(cut_ref, e1_ref, act_ref, w_ref):
        zero = jnp.zeros((half_keys, LANES), BF16)
        for part in range(n_keys // half_keys):
            prow = slice(part * packed_half, (part + 1) * packed_half)
            for tl in range(tm // LANES):
                lanes = slice(tl * LANES, (tl + 1) * LANES)
                gates = [zero] * rows_per_chunk
                for hh in range(n_heads):
                    r2 = _unpack_rows(r2_ref[hh, prow, lanes])
                    e2 = _unpack_rows(e2_ref[hh, prow, lanes])
                    for il in range(rows_per_chunk):
                        cut = _unpack_rows(jnp.broadcast_to(cut_ref[hh, il:il + 1, lanes], (packed_half, LANES)))
                        e1 = _unpack_rows(jnp.broadcast_to(e1_ref[hh, il:il + 1, lanes], (packed_half, LANES)))
                        gates[il] = gates[il] + jnp.where(r2 < cut, e2, zero) * e1
                for il in range(rows_per_chunk):
                    rows = slice(il * n_keys + part * half_keys, il * n_keys + (part + 1) * half_keys)
                    w_ref[rows, lanes] = _gelu(act_ref[rows, lanes]).astype(BF16) * gates[il]

    def stage(act_new, act_old, w_new, w_old):
        act_new[...] = _dot(_unpack_rows(u_ref[...]), _unpack_rows(h_ref[...]))
        gate_chunk(cut_ref, e1_ref, act_old, w_new)
        acc[...] += _dot(_unpack_rows(vt_ref[...]), w_old[...])

    pl.when(c % 2 == 0)(lambda: stage(act0, act1, w1, w0))
    pl.when(c % 2 == 1)(lambda: stage(act1, act0, w0, w1))

    @pl.when(c == pl.num_programs(1) - 1)
    def _():
        y = x_ref[...] + mod_ref[0][5:6] * acc[...].T
        if final_norm:
            y = _rms(y) * fg_ref[...]
        o_ref[...] = y


def _peer_layer(st, xs, mods, gain, wq_packed, k1, k2, u_packed, vt_packed, final_gain, with_ctx, final_norm):
    tm, d = PEER_TILE, st.d
    n_heads, n_keys, half = k1.shape
    n_exp = 2 * u_packed.shape[0]
    ec = PEER_EXPERT_CHUNK
    n_tiles = st.tiles(tm, with_ctx)
    n_rows = n_tiles * tm
    u32 = jnp.uint32
    h_spec = pl.BlockSpec((d // 2, tm), lambda t, *_: (0, t))
    key_spec = pl.BlockSpec((n_heads, n_keys, tm), lambda t, *_: (0, 0, t))
    key_shape = jax.ShapeDtypeStruct((n_heads, n_keys, n_rows), u32)
    pair_spec = pl.BlockSpec((n_heads, n_keys // 2, tm), lambda t, *_: (0, 0, t))
    pair_shape = jax.ShapeDtypeStruct((n_heads, n_keys // 2, n_rows), u32)
    h, cut, e1, r2, e2 = pl.pallas_call(
        _peer_select_kernel,
        grid=(n_tiles,),
        in_specs=[_tok_spec(tm, d), st.mod_spec(tm), _full_spec((1, d)), _full_spec(wq_packed.shape),
                  _full_spec(k1.shape), _full_spec(k2.shape)],
        out_specs=[h_spec, key_spec, key_spec, pair_spec, pair_spec],
        out_shape=[jax.ShapeDtypeStruct((d // 2, n_rows), u32), key_shape, key_shape, pair_shape, pair_shape],
        scratch_shapes=[pltpu.VMEM((n_heads, n_keys, tm), F32), pltpu.VMEM((n_heads, n_keys, tm), F32)],
        compiler_params=_cparams("parallel"),
        name="peer_select",
    )(xs, mods, gain.reshape(1, d), wq_packed, k1.astype(BF16), k2.astype(BF16))

    n_chunks = n_exp // ec
    last = n_chunks - 1
    rows_per_chunk = ec // n_keys
    sub1_spec = pl.BlockSpec((n_heads, rows_per_chunk, tm), lambda t, c: (0, jnp.clip(c - 1, 0, last), t))
    return pl.pallas_call(
        functools.partial(_peer_main_kernel, final_norm=final_norm),
        grid=(n_tiles, n_chunks + 2),
        in_specs=[
            h_spec,
            pl.BlockSpec((ec // 2, d), lambda t, c: (jnp.minimum(c, last), 0)),
            pl.BlockSpec((d // 2, ec), lambda t, c: (0, jnp.clip(c - 2, 0, last))),
            sub1_spec, sub1_spec, pair_spec, pair_spec,
            _tok_spec(tm, d), st.mod_spec(tm), _full_spec((1, d)),
        ],
        out_specs=_tok_spec(tm, d),
        out_shape=jax.ShapeDtypeStruct((n_rows, d), F32),
        scratch_shapes=[pltpu.VMEM((d, tm), F32),
                        pltpu.VMEM((ec, tm), F32), pltpu.VMEM((ec, tm), F32),
                        pltpu.VMEM((ec, tm), BF16), pltpu.VMEM((ec, tm), BF16)],
        compiler_params=_cparams("parallel", "arbitrary"),
        name="peer_main",
    )(h, u_packed, vt_packed, cut, e1, r2, e2, xs, mods, final_gain.reshape(1, d))


def kernel(x, c, ctx, c_ctx, ada_w, ada_b, norm1_g, norm2_g, sgu_w_in, sgu_v_gain, sgu_w_s, sgu_b_s, sgu_w_out, pool_w_in, pool_w_grp, pool_scale, pool_w_out, attn_w_qkv, attn_q_gain, attn_k_gain, attn_w_o, peer_w_q, peer_k1, peer_k2, peer_u, peer_v, final_gain):
    bsz, seq, d = x.shape
    ctx_len = ctx.shape[1]
    depth = ada_w.shape[0]
    st = _Stream(bsz, seq, ctx_len, d)
    assert bsz + 1 <= MOD_ROWS and seq % PEER_TILE == 0 and (bsz * ctx_len) % PEER_TILE == 0
    assert seq % TOKEN_TILE == 0 and ctx_len % TOKEN_TILE == 0

    cond = jnp.concatenate([c, c_ctx[None, :], jnp.zeros((MOD_ROWS - bsz - 1, d), F32)], axis=0)
    mods = _ada_mods(cond, ada_w, ada_b)
    wq_packed = _pack_weights(peer_w_q, transpose=True)
    u_packed = _pack_weights(peer_u, transpose=False)
    vt_packed = _pack_weights(peer_v, transpose=True)
    xs = jnp.concatenate([x.reshape(bsz * seq, d), ctx.reshape(bsz * ctx_len, d)], axis=0)

    for i in range(depth):
        kind, j = i % N_MIXERS, i // N_MIXERS
        with_ctx = i < depth - 1
        if kind == 0:
            xs = _sgu_layer(st, xs, mods[i], norm1_g[i], sgu_w_in[j], sgu_v_gain[j], sgu_w_s[j], sgu_b_s[j],
                            sgu_w_out[j], with_ctx)
        elif kind == 1:
            xs = _pool_layer(st, xs, mods[i], norm1_g[i], pool_w_in[j], pool_w_grp[j], pool_scale[j],
                             pool_w_out[j], with_ctx)
        else:
            xs = _attn_layer(st, xs, mods[i], norm1_g[i], attn_w_qkv[j], attn_q_gain[j], attn_k_gain[j],
                             attn_w_o[j], with_ctx)
        xs = _peer_layer(st, xs, mods[i], norm2_g[i], wq_packed[i], peer_k1[i], peer_k2[i], u_packed[i],
                         vt_packed[i], final_gain, with_ctx, final_norm=(i == depth - 1))
    return xs[:bsz * seq].reshape(bsz, seq, d)
```

```python
import functools
import math

import jax
import jax.numpy as jnp
from jax import lax
from jax.experimental import pallas as pl
from jax.experimental.pallas import tpu as pltpu

F32 = jnp.float32
BF16 = jnp.bfloat16

NORM_EPS = 1e-6
MOD_CHUNKS = 6
GRID_W = 64
SGU_CHUNK = 128
SGU_GROUPS = 8
POOL_WINDOWS = (2, 4, 8, 16)
HEAD_DIM = 128
N_KV_HEADS = 2
ROPE_THETA = 10000.0
PEER_HEADS = 8
PEER_KEYS = 128
PEER_TOPK = 16
N_MIXERS = 3

LANES = 128
MOD_ROWS = 16
VMEM_LIMIT = 56 * 1024 * 1024

TOKEN_TILE = 256
PEER_TILE = 512
PEER_EXPERT_CHUNK = 1024
SELECT_UNROLL = 2


def _cparams(*sem):
    return pltpu.CompilerParams(dimension_semantics=sem, vmem_limit_bytes=VMEM_LIMIT)


def _rms(x):
    return x * lax.rsqrt(jnp.mean(x * x, axis=-1, keepdims=True) + NORM_EPS)


def _modulate(x, gain, shift, scale):
    return _rms(x) * gain * (1.0 + scale) + shift


def _gelu(x):
    return 0.5 * x * (1.0 + lax.erf(x * (1.0 / math.sqrt(2.0))))


def _dot(a, b):
    return jnp.dot(a, b, preferred_element_type=F32)


def _dot_nt(a, b):
    return lax.dot_general(a, b, (((1,), (1,)), ((), ())), preferred_element_type=F32)


def _ada_kernel(c_ref, w_ref, b_ref, o_ref):
    c = c_ref[...]
    a = c * (1.0 / (1.0 + jnp.exp(-c)))
    o_ref[0] = _dot(a.astype(BF16), w_ref[0].astype(BF16)) + b_ref[0]


def _ada_mods(cond, ada_w, ada_b):
    depth, d, n = ada_w.shape
    tn = n // 4
    out = pl.pallas_call(
        _ada_kernel,
        grid=(depth, n // tn),
        in_specs=[
            pl.BlockSpec((MOD_ROWS, d), lambda l, j: (0, 0)),
            pl.BlockSpec((1, d, tn), lambda l, j: (l, 0, j)),
            pl.BlockSpec((1, 1, tn), lambda l, j: (l, 0, j)),
        ],
        out_specs=pl.BlockSpec((1, MOD_ROWS, tn), lambda l, j: (l, 0, j)),
        out_shape=jax.ShapeDtypeStruct((depth, MOD_ROWS, n), F32),
        compiler_params=_cparams("parallel", "parallel"),
        name="ada_mods",
    )(cond, ada_w, ada_b.reshape(depth, 1, n))
    return out.reshape(depth, MOD_ROWS, MOD_CHUNKS, d)


class _Stream:
    def __init__(self, bsz, seq, ctx_len, d):
        self.bsz, self.seq, self.ctx_len, self.d = bsz, seq, ctx_len, d
        self.n_lat = bsz * seq
        self.n_tok = bsz * (seq + ctx_len)

    def tiles(self, tm, with_ctx):
        return (self.n_tok if with_ctx else self.n_lat) // tm

    def mod_row(self, tm):
        n_lat_tiles, per_seq, bsz = self.n_lat // tm, self.seq // tm, self.bsz
        return lambda t: jnp.where(t < n_lat_tiles, t // per_seq, bsz)

    def mod_spec(self, tm):
        row = self.mod_row(tm)
        return pl.BlockSpec((1, MOD_CHUNKS, self.d), lambda t, *_: (row(t), 0, 0))


def _tok_spec(tm, width):
    return pl.BlockSpec((tm, width), lambda t, *_: (t, 0))


def _full_spec(shape):
    zeros = (0,) * len(shape)
    return pl.BlockSpec(shape, lambda *_: zeros)


def _sgu_kernel(x_ref, mod_ref, g_ref, win_ref, vg_ref, ws_ref, bs_ref, wout_ref, o_ref):
    x = x_ref[...]
    mod = mod_ref[0]
    tm = x.shape[0]
    width = vg_ref.shape[1]
    gdim = width // SGU_GROUPS
    h = _modulate(x, g_ref[...], mod[0:1], mod[1:2]).astype(BF16)
    u = _gelu(_dot(h, win_ref[:, :width]))
    v = _gelu(_dot(h, win_ref[:, width:]))
    vn = (_rms(v) * vg_ref[...]).astype(BF16)
    rows = []
    for c in range(tm // SGU_CHUNK):
        cols = []
        for g in range(SGU_GROUPS):
            vv = vn[c * SGU_CHUNK:(c + 1) * SGU_CHUNK, g * gdim:(g + 1) * gdim]
            cols.append(_dot(ws_ref[g], vv) + bs_ref[:, g:g + 1])
        rows.append(jnp.concatenate(cols, axis=1))
    sv = jnp.concatenate(rows, axis=0)
    y = _dot((u * sv).astype(BF16), wout_ref[...])
    o_ref[...] = x + mod[2:3] * y


def _sgu_layer(st, xs, mods, gain, w_in, v_gain, w_s, b_s, w_out, with_ctx):
    tm, d = TOKEN_TILE, st.d
    width = v_gain.shape[0]
    return pl.pallas_call(
        _sgu_kernel,
        grid=(st.tiles(tm, with_ctx),),
        in_specs=[
            _tok_spec(tm, d), st.mod_spec(tm), _full_spec((1, d)),
            _full_spec((d, 2 * width)), _full_spec((1, width)),
            _full_spec((SGU_GROUPS, SGU_CHUNK, SGU_CHUNK)), _full_spec((SGU_CHUNK, SGU_GROUPS)),
            _full_spec((width, d)),
        ],
        out_specs=_tok_spec(tm, d),
        out_shape=jax.ShapeDtypeStruct((st.tiles(tm, with_ctx) * tm, d), F32),
        compiler_params=_cparams("parallel"),
        name="sgu_mixer",
    )(xs, mods, gain.reshape(1, d), w_in.astype(BF16), v_gain.reshape(1, width),
      w_s.astype(BF16), b_s.T, w_out.astype(BF16))


POOL_HALO = 8


def _pool_in_kernel(x_ref, mod_ref, g_ref, win_ref, z_ref):
    mod = mod_ref[0]
    h = _modulate(x_ref[...], g_ref[...], mod[0:1], mod[1:2]).astype(BF16)
    z_ref[...] = _dot(h, win_ref[...])


def _pool_out_kernel(z_ref, zp_ref, zn_ref, x_ref, mod_ref, wg_ref, sc_ref, wout_ref, o_ref,
                     *, n_lat_tiles, lat_tiles_per_seq, ctx_tiles_per_seq, seq, ctx_len):
    t = pl.program_id(0)
    tm = z_ref.shape[0]
    is_lat = t < n_lat_tiles
    per_seq = jnp.where(is_lat, lat_tiles_per_seq, ctx_tiles_per_seq)
    pos_tile = jnp.where(is_lat, t, t - n_lat_tiles) % per_seq
    length = jnp.where(is_lat, seq, ctx_len)
    has_prev = (pos_tile > 0).astype(F32)
    has_next = (pos_tile < per_seq - 1).astype(F32)
    z = z_ref[...]
    zext = jnp.concatenate([zp_ref[...] * has_prev, z, zn_ref[...] * has_next], axis=0)
    pos = (pos_tile * tm + lax.broadcasted_iota(jnp.int32, (tm, 1), 0))
    gdim = wg_ref.shape[1]
    n_ext = tm + 2 * POOL_HALO

    def shift_up(a, k):
        return a if k == 0 else pltpu.roll(a, n_ext - k, 0)

    outs = []
    for g, w in enumerate(POOL_WINDOWS):
        half = w // 2
        acc = zext[:, g * gdim:(g + 1) * gdim]
        span = 1
        while span < w:
            acc = acc + shift_up(acc, span)
            span *= 2
        win_sum = shift_up(acc, POOL_HALO - half)[:tm]
        cnt = (jnp.minimum(pos + half, length) - jnp.maximum(pos - half, 0)).astype(F32)
        pooled = win_sum / cnt - z[:, g * gdim:(g + 1) * gdim]
        outs.append(_dot(pooled.astype(BF16), wg_ref[g]))
    y = (jnp.concatenate(outs, axis=1) * sc_ref[...]).astype(BF16)
    o_ref[...] = x_ref[...] + mod_ref[0][2:3] * _dot(y, wout_ref[...])


def _pool_layer(st, xs, mods, gain, w_in, w_grp, scale, w_out, with_ctx):
    tm, d = TOKEN_TILE, st.d
    width = w_in.shape[1]
    n_tiles = st.tiles(tm, with_ctx)
    z = pl.pallas_call(
        _pool_in_kernel,
        grid=(n_tiles,),
        in_specs=[_tok_spec(tm, d), st.mod_spec(tm), _full_spec((1, d)), _full_spec((d, width))],
        out_specs=_tok_spec(tm, width),
        out_shape=jax.ShapeDtypeStruct((n_tiles * tm, width), F32),
        compiler_params=_cparams("parallel"),
        name="pool_in",
    )(xs, mods, gain.reshape(1, d), w_in.astype(BF16))
    halo_per_tile = tm // POOL_HALO
    last_halo = n_tiles * halo_per_tile - 1
    kern = functools.partial(
        _pool_out_kernel, n_lat_tiles=st.n_lat // tm, lat_tiles_per_seq=st.seq // tm,
        ctx_tiles_per_seq=st.ctx_len // tm, seq=st.seq, ctx_len=st.ctx_len)
    return pl.pallas_call(
        kern,
        grid=(n_tiles,),
        in_specs=[
            _tok_spec(tm, width),
            pl.BlockSpec((POOL_HALO, width), lambda t: (jnp.maximum(t * halo_per_tile - 1, 0), 0)),
            pl.BlockSpec((POOL_HALO, width), lambda t: (jnp.minimum((t + 1) * halo_per_tile, last_halo), 0)),
            _tok_spec(tm, d), st.mod_spec(tm),
            _full_spec(w_grp.shape), _full_spec((1, width)), _full_spec((width, d)),
        ],
        out_specs=_tok_spec(tm, d),
        out_shape=jax.ShapeDtypeStruct((st.tiles(tm, with_ctx) * tm, d), F32),
        compiler_params=_cparams("parallel"),
        name="pool_out",
    )(z, z, z, xs, mods, w_grp.astype(BF16), scale.reshape(1, width), w_out.astype(BF16))


def _rope_tables(seq, tm):
    pos = jnp.arange(seq)
    axis_dim = HEAD_DIM // 2
    freqs = ROPE_THETA ** (-jnp.arange(0, axis_dim, 2, dtype=F32) / axis_dim)
    ang = jnp.stack([pos // GRID_W, pos % GRID_W], axis=-1).astype(F32)[:, :, None] * freqs
    cos, sin = jnp.cos(ang), jnp.sin(ang)
    zero = jnp.zeros_like(sin)
    c = jnp.concatenate([cos, cos], axis=-1).reshape(seq, HEAD_DIM)
    s_lo = jnp.concatenate([-sin, zero], axis=-1).reshape(seq, HEAD_DIM)
    s_hi = jnp.concatenate([zero, sin], axis=-1).reshape(seq, HEAD_DIM)
    pad = jnp.zeros((tm, HEAD_DIM), F32)
    return (jnp.concatenate([c, pad + 1.0]), jnp.concatenate([s_lo, pad]), jnp.concatenate([s_hi, pad]))


def _qkv_kernel(x_ref, mod_ref, g_ref, w_ref, qg_ref, kg_ref, c_ref, slo_ref, shi_ref, q_ref, k_ref, v_ref):
    mod = mod_ref[0]
    h = _modulate(x_ref[...], g_ref[...], mod[0:1], mod[1:2]).astype(BF16)
    qkv = _dot(h, w_ref[...])
    nq, nk = q_ref.shape[1], k_ref.shape[1]
    cos, s_lo, s_hi = c_ref[...], slo_ref[...], shi_ref[...]
    quarter = HEAD_DIM // 4

    def norm_rope(t, gain):
        t = _rms(t) * gain
        return (t * cos + pltpu.roll(t, HEAD_DIM - quarter, 1) * s_lo + pltpu.roll(t, quarter, 1) * s_hi)

    for hh in range(nq // HEAD_DIM):
        sl = slice(hh * HEAD_DIM, (hh + 1) * HEAD_DIM)
        q_ref[:, sl] = norm_rope(qkv[:, sl], qg_ref[...]).astype(BF16)
    for hh in range(nk // HEAD_DIM):
        sl = slice(hh * HEAD_DIM, (hh + 1) * HEAD_DIM)
        k_ref[:, sl] = norm_rope(qkv[:, nq + hh * HEAD_DIM:nq + (hh + 1) * HEAD_DIM], kg_ref[...]).astype(BF16)
    v_ref[...] = qkv[:, nq + nk:].astype(BF16)


def _attn_kernel(q_ref, kl_ref, vl_ref, kc_ref, vc_ref, x_ref, mod_ref, wo_ref, o_ref, *, n_lat_tiles):
    n_heads = q_ref.shape[1] // HEAD_DIM
    q_per_kv = n_heads // N_KV_HEADS
    scale = HEAD_DIM ** -0.5

    def body(with_lat):
        outs = []
        for hh in range(n_heads):
            kv = slice((hh // q_per_kv) * HEAD_DIM, (hh // q_per_kv + 1) * HEAD_DIM)
            q = q_ref[:, hh * HEAD_DIM:(hh + 1) * HEAD_DIM]
            s_c = _dot_nt(q, kc_ref[:, kv]) * scale
            m = jnp.max(s_c, axis=-1, keepdims=True)
            if with_lat:
                s_l = _dot_nt(q, kl_ref[:, kv]) * scale
                m = jnp.maximum(m, jnp.max(s_l, axis=-1, keepdims=True))
            p_c = jnp.exp(s_c - m)
            den = jnp.sum(p_c, axis=-1, keepdims=True)
            o = _dot(p_c.astype(BF16), vc_ref[:, kv])
            if with_lat:
                p_l = jnp.exp(s_l - m)
                den = den + jnp.sum(p_l, axis=-1, keepdims=True)
                o = o + _dot(p_l.astype(BF16), vl_ref[:, kv])
            outs.append(o / den)
        y = _dot(jnp.concatenate(outs, axis=1).astype(BF16), wo_ref[...])
        o_ref[...] = x_ref[...] + mod_ref[0][2:3] * y

    t = pl.program_id(0)
    pl.when(t < n_lat_tiles)(lambda: body(True))
    pl.when(t >= n_lat_tiles)(lambda: body(False))


def _attn_layer(st, xs, mods, gain, w_qkv, q_gain, k_gain, w_o, with_ctx):
    d = st.d
    tm = st.ctx_len
    assert st.seq % tm == 0
    n_all = st.n_tok // tm
    n_lat_tiles, per_seq = st.n_lat // tm, st.seq // tm
    nq = w_o.shape[0]
    nk = (w_qkv.shape[1] - nq) // 2
    cos, s_lo, s_hi = _rope_tables(st.seq, tm)
    rope_spec = pl.BlockSpec((tm, HEAD_DIM), lambda t: (jnp.where(t < n_lat_tiles, t % per_seq, per_seq), 0))
    q, k, v = pl.pallas_call(
        _qkv_kernel,
        grid=(n_all,),
        in_specs=[
            _tok_spec(tm, d), st.mod_spec(tm), _full_spec((1, d)), _full_spec(w_qkv.shape),
            _full_spec((1, HEAD_DIM)), _full_spec((1, HEAD_DIM)), rope_spec, rope_spec, rope_spec,
        ],
        out_specs=[_tok_spec(tm, nq), _tok_spec(tm, nk), _tok_spec(tm, nk)],
        out_shape=[jax.ShapeDtypeStruct((st.n_tok, nq), BF16), jax.ShapeDtypeStruct((st.n_tok, nk), BF16),
                   jax.ShapeDtypeStruct((st.n_tok, nk), BF16)],
        compiler_params=_cparams("parallel"),
        name="attn_qkv",
    )(xs, mods, gain.reshape(1, d), w_qkv.astype(BF16), q_gain.reshape(1, HEAD_DIM),
      k_gain.reshape(1, HEAD_DIM), cos, s_lo, s_hi)

    bsz = st.bsz
    batch_of = lambda t: jnp.where(t < n_lat_tiles, t // per_seq, t - n_lat_tiles)
    lat_spec = pl.BlockSpec((st.seq, nk), lambda t: (jnp.minimum(t // per_seq, bsz - 1), 0))
    ctx_spec = pl.BlockSpec((tm, nk), lambda t: (n_lat_tiles + batch_of(t), 0))
    return pl.pallas_call(
        functools.partial(_attn_kernel, n_lat_tiles=n_lat_tiles),
        grid=(st.tiles(tm, with_ctx),),
        in_specs=[_tok_spec(tm, nq), lat_spec, lat_spec, ctx_spec, ctx_spec,
                  _tok_spec(tm, d), st.mod_spec(tm), _full_spec((nq, d))],
        out_specs=_tok_spec(tm, d),
        out_shape=jax.ShapeDtypeStruct((st.tiles(tm, with_ctx) * tm, d), F32),
        compiler_params=_cparams("parallel"),
        name="attn_core",
    )(q, k, v, k, v, xs, mods, w_o.astype(BF16))


_INT_MIN = -2 ** 31


def _sortable(x):
    b = lax.bitcast_convert_type(x, jnp.int32)
    return b ^ ((b >> 31) & 0x7FFFFFFF)


def _unsortable(k):
    return lax.bitcast_convert_type(k ^ ((k >> 31) & 0x7FFFFFFF), F32)


def _extract16(k, iota, exact):
    n = k.shape[0]
    tops = []
    for a in range(PEER_TOPK):
        m = jnp.max(k, axis=0, keepdims=True)
        hit = k == m
        if exact:
            first = jnp.min(jnp.where(hit, iota, n), axis=0, keepdims=True)
            hit = iota == first
        k = jnp.where(hit, _INT_MIN + a, k)
        tops.append(m)
    return k, tops


def _rows16(tops, iota16):
    out = jnp.zeros(iota16.shape, tops[0].dtype)
    for a, t in enumerate(tops):
        out = jnp.where(iota16 == a, t, out)
    return out


_CAND_HALF = PEER_TOPK // 2


def _peer_select(s1, s2, exact):
    n, width = s1.shape
    topk = float(PEER_TOPK)
    iota = lax.broadcasted_iota(jnp.int32, (n, width), 0)
    iota16 = iota[0:PEER_TOPK]
    mark_end = _INT_MIN + PEER_TOPK
    k1, top1 = _extract16(_sortable(s1), iota, exact)
    k2, top2 = _extract16(_sortable(s2), iota, exact)
    v1 = _unsortable(_rows16(top1, iota16))
    v2 = _unsortable(_rows16(top2, iota16))
    f1 = [_unsortable(t) for t in top1[:_CAND_HALF]]
    f2_0 = _unsortable(top2[0])
    pieces = [v2 + f1[0]]
    pieces += [v2[0:_CAND_HALF] + f1[a] for a in range(1, _CAND_HALF)]
    pieces.append(v1[_CAND_HALF:] + f2_0)
    cand = jnp.concatenate(pieces, axis=0)
    nc = cand.shape[0]
    kc, _ = _extract16(_sortable(cand), iota[0:nc], exact)
    picked = kc < mark_end
    sel = jnp.where(picked, 1.0, 0.0)
    z = jnp.sum(jnp.where(picked, jnp.exp(cand - (f1[0] + f2_0)), 0.0), axis=0, keepdims=True)
    cnt = [jnp.sum(sel[0:PEER_TOPK], axis=0, keepdims=True)]
    for a in range(1, _CAND_HALF):
        row = PEER_TOPK + (a - 1) * _CAND_HALF
        cnt.append(jnp.sum(sel[row:row + _CAND_HALF], axis=0, keepdims=True))
    tail = sel[nc - _CAND_HALF:]
    for a in range(_CAND_HALF, PEER_TOPK):
        pick = iota[0:_CAND_HALF] == a - _CAND_HALF
        cnt.append(jnp.sum(jnp.where(pick, tail, 0.0), axis=0, keepdims=True))
    cut = jnp.zeros(s1.shape, F32)
    for a in range(PEER_TOPK):
        cut = jnp.where(k1 == _INT_MIN + a, cnt[a], cut)
    in1, in2 = k1 < mark_end, k2 < mark_end
    r2 = jnp.where(in2, (k2 - _INT_MIN).astype(F32), topk)
    e1 = jnp.exp(s1 - f1[0])
    e2 = jnp.exp(s2 - f2_0) / z
    n1 = jnp.sum(jnp.where(in1, 1.0, 0.0), axis=0, keepdims=True)
    n2 = jnp.sum(jnp.where(in2, 1.0, 0.0), axis=0, keepdims=True)
    nsel = jnp.sum(sel, axis=0, keepdims=True)
    bad = jnp.where((n1 != topk) | (n2 != topk) | (nsel != topk), 1.0, 0.0)
    return cut, e1, r2, e2, bad


def _pack_rows(x):
    return pltpu.bitcast(x.astype(BF16), jnp.uint32)


def _unpack_rows(x):
    return pltpu.bitcast(x, BF16)


def _pack_twice(x):
    u = lax.bitcast_convert_type(x.astype(BF16).astype(F32), jnp.uint32)
    return u | (u >> 16)


PACK_BLOCK = 1024


def _pack_kernel(w_ref, o_ref, *, transpose):
    w = w_ref[0]
    o_ref[0] = _pack_rows(w.T if transpose else w)


def _pack_weights(w, transpose):
    layers, r, c = w.shape
    pb = PACK_BLOCK
    out_rows, out_cols = (c, r) if transpose else (r, c)
    out_map = (lambda l, i, j: (l, j, i)) if transpose else (lambda l, i, j: (l, i, j))
    return pl.pallas_call(
        functools.partial(_pack_kernel, transpose=transpose),
        grid=(layers, r // pb, c // pb),
        in_specs=[pl.BlockSpec((1, pb, pb), lambda l, i, j: (l, i, j))],
        out_specs=pl.BlockSpec((1, pb // 2, pb), out_map),
        out_shape=jax.ShapeDtypeStruct((layers, out_rows // 2, out_cols), jnp.uint32),
        compiler_params=_cparams("parallel", "parallel", "parallel"),
        name="pack_t" if transpose else "pack",
    )(w)


def _peer_select_kernel(x_ref, mod_ref, g_ref, wq_ref, k1_ref, k2_ref,
                        h_ref, cut_ref, e1_ref, r2_ref, e2_ref, s1_scr, s2_scr):
    mod = mod_ref[0]
    f_t = _modulate(x_ref[...], g_ref[...], mod[3:4], mod[4:5]).T.astype(BF16)
    h_ref[...] = _pack_rows(f_t)
    tm = f_t.shape[1]
    n_heads, n_keys, half = k1_ref.shape
    q_t = _dot(_unpack_rows(wq_ref[...]), f_t)
    for hh in range(n_heads):
        base = hh * 2 * half
        s1_scr[hh] = _dot(k1_ref[hh], q_t[base:base + half].astype(BF16))
        s2_scr[hh] = _dot(k2_ref[hh], q_t[base + half:base + 2 * half].astype(BF16))
    n_sub = tm // LANES

    def step(idx, carry):
        hh = idx // (n_sub // SELECT_UNROLL)
        base = (idx % (n_sub // SELECT_UNROLL)) * SELECT_UNROLL
        bad_any = jnp.zeros((1, LANES), F32)
        work = []
        for sub in range(SELECT_UNROLL):
            lanes = pl.ds(pl.multiple_of((base + sub) * LANES, LANES), LANES)
            s1, s2 = s1_scr[hh, :, lanes], s2_scr[hh, :, lanes]
            *fast, bad = _peer_select(s1, s2, exact=False)
            work.append((lanes, s1, s2, fast))
            bad_any = jnp.maximum(bad_any, bad)

        def store(lanes, cut, e1, r2, e2):
            cut_ref[hh, :, lanes] = _pack_twice(cut)
            e1_ref[hh, :, lanes] = _pack_twice(e1)
            r2_ref[hh, :, lanes] = _pack_rows(r2)
            e2_ref[hh, :, lanes] = _pack_rows(e2)

        for lanes, _, _, fast in work:
            store(lanes, *fast)

        @pl.when(jnp.max(bad_any) > 0.0)
        def _():
            for lanes, s1, s2, _ in work:
                store(lanes, *_peer_select(s1, s2, exact=True)[:4])

        return carry

    lax.fori_loop(0, n_heads * n_sub // SELECT_UNROLL, step, 0)


def _peer_main_kernel(h_ref, u_ref, vt_ref, cut_ref, e1_ref, r2_ref, e2_ref, x_ref, mod_ref, fg_ref,
                      o_ref, acc, act0, act1, w0, w1, *, final_norm, n_chunks):
    g = pl.program_id(0)
    ec, tm = act0.shape
    n_heads, n_keys = r2_ref.shape[0], 2 * r2_ref.shape[1]
    rows_per_chunk = ec // n_keys
    half_keys = n_keys // 2
    packed_half = half_keys // 2

    @pl.when(g == 0)
    def _():
        acc[...] = jnp.zeros_like(acc)
        act1[...] = jnp.zeros_like(act1)
        w0[...] = jnp.zeros_like(w0)

    def gate_chunk(cut_ref, e1_ref, act_ref, w_ref):
        zero = jnp.zeros((half_keys, LANES), BF16)
        for part in range(n_keys // half_keys):
            prow = slice(part * packed_half, (part + 1) * packed_half)
            for tl in range(tm // LANES):
                lanes = slice(tl * LANES, (tl + 1) * LANES)
                gates = [zero] * rows_per_chunk
                for hh in range(n_heads):
                    r2 = _unpack_rows(r2_ref[hh, prow, lanes])
                    e2 = _unpack_rows(e2_ref[hh, prow, lanes])
                    for il in range(rows_per_chunk):
                        cut = _unpack_rows(jnp.broadcast_to(cut_ref[hh, il:il + 1, lanes], (packed_half, LANES)))
                        e1 = _unpack_rows(jnp.broadcast_to(e1_ref[hh, il:il + 1, lanes], (packed_half, LANES)))
                        gates[il] = gates[il] + jnp.where(r2 < cut, e2, zero) * e1
                for il in range(rows_per_chunk):
                    rows = slice(il * n_keys + part * half_keys, il * n_keys + (part + 1) * half_keys)
                    w_ref[rows, lanes] = _gelu(act_ref[rows, lanes]).astype(BF16) * gates[il]

    def stage(act_new, act_old, w_new, w_old):
        act_new[...] = _dot(_unpack_rows(u_ref[...]), _unpack_rows(h_ref[...]))
        gate_chunk(cut_ref, e1_ref, act_old, w_new)
        acc[...] += _dot(_unpack_rows(vt_ref[...]), w_old[...])

    pl.when(g % 2 == 0)(lambda: stage(act0, act1, w1, w0))
    pl.when(g % 2 == 1)(lambda: stage(act1, act0, w0, w1))

    @pl.when((g >= 2) & ((g - 2) % n_chunks == n_chunks - 1))
    def _():
        y = x_ref[...] + mod_ref[0][5:6] * acc[...].T
        if final_norm:
            y = _rms(y) * fg_ref[...]
        o_ref[...] = y
        acc[...] = jnp.zeros_like(acc)


def _peer_layer(st, xs, mods, gain, wq_packed, k1, k2, u_packed, vt_packed, final_gain, with_ctx, final_norm):
    tm, d = PEER_TILE, st.d
    n_heads, n_keys, half = k1.shape
    n_exp = 2 * u_packed.shape[0]
    ec = PEER_EXPERT_CHUNK
    n_tiles = st.tiles(tm, with_ctx)
    n_rows = n_tiles * tm
    u32 = jnp.uint32
    h_spec = pl.BlockSpec((d // 2, tm), lambda t, *_: (0, t))
    key_spec = pl.BlockSpec((n_heads, n_keys, tm), lambda t, *_: (0, 0, t))
    key_shape = jax.ShapeDtypeStruct((n_heads, n_keys, n_rows), u32)
    pair_spec = pl.BlockSpec((n_heads, n_keys // 2, tm), lambda t, *_: (0, 0, t))
    pair_shape = jax.ShapeDtypeStruct((n_heads, n_keys // 2, n_rows), u32)
    h, cut, e1, r2, e2 = pl.pallas_call(
        _peer_select_kernel,
        grid=(n_tiles,),
        in_specs=[_tok_spec(tm, d), st.mod_spec(tm), _full_spec((1, d)), _full_spec(wq_packed.shape),
                  _full_spec(k1.shape), _full_spec(k2.shape)],
        out_specs=[h_spec, key_spec, key_spec, pair_spec, pair_spec],
        out_shape=[jax.ShapeDtypeStruct((d // 2, n_rows), u32), key_shape, key_shape, pair_shape, pair_shape],
        scratch_shapes=[pltpu.VMEM((n_heads, n_keys, tm), F32), pltpu.VMEM((n_heads, n_keys, tm), F32)],
        compiler_params=_cparams("parallel"),
        name="peer_select",
    )(xs, mods, gain.reshape(1, d), wq_packed, k1.astype(BF16), k2.astype(BF16))

    n_chunks = n_exp // ec
    n_items = n_tiles * n_chunks
    rows_per_chunk = ec // n_keys
    mod_row = st.mod_row(tm)

    def item(lag):
        def tile(g):
            return jnp.clip(g - lag, 0, n_items - 1) // n_chunks

        def chunk(g):
            return jnp.clip(g - lag, 0, n_items - 1) % n_chunks

        return tile, chunk

    (tile0, chunk0), (tile1, chunk1), (tile2, chunk2) = item(0), item(1), item(2)
    sub1_spec = pl.BlockSpec((n_heads, rows_per_chunk, tm), lambda g: (0, chunk1(g), tile1(g)))
    sub2_spec = pl.BlockSpec((n_heads, n_keys // 2, tm), lambda g: (0, 0, tile1(g)))
    return pl.pallas_call(
        functools.partial(_peer_main_kernel, final_norm=final_norm, n_chunks=n_chunks),
        grid=(n_items + 2,),
        in_specs=[
            pl.BlockSpec((d // 2, tm), lambda g: (0, tile0(g))),
            pl.BlockSpec((ec // 2, d), lambda g: (chunk0(g), 0)),
            pl.BlockSpec((d // 2, ec), lambda g: (0, chunk2(g))),
            sub1_spec, sub1_spec, sub2_spec, sub2_spec,
            pl.BlockSpec((tm, d), lambda g: (tile2(g), 0)),
            pl.BlockSpec((1, MOD_CHUNKS, d), lambda g: (mod_row(tile2(g)), 0, 0)),
            _full_spec((1, d)),
        ],
        out_specs=pl.BlockSpec((tm, d), lambda g: (tile2(g), 0)),
        out_shape=jax.ShapeDtypeStruct((n_rows, d), F32),
        scratch_shapes=[pltpu.VMEM((d, tm), F32),
                        pltpu.VMEM((ec, tm), F32), pltpu.VMEM((ec, tm), F32),
                        pltpu.VMEM((ec, tm), BF16), pltpu.VMEM((ec, tm), BF16)],
        compiler_params=_cparams("arbitrary"),
        name="peer_main",
    )(h, u_packed, vt_packed, cut, e1, r2, e2, xs, mods, final_gain.reshape(1, d))


def kernel(x, c, ctx, c_ctx, ada_w, ada_b, norm1_g, norm2_g, sgu_w_in, sgu_v_gain, sgu_w_s, sgu_b_s, sgu_w_out, pool_w_in, pool_w_grp, pool_scale, pool_w_out, attn_w_qkv, attn_q_gain, attn_k_gain, attn_w_o, peer_w_q, peer_k1, peer_k2, peer_u, peer_v, final_gain):
    bsz, seq, d = x.shape
    ctx_len = ctx.shape[1]
    depth = ada_w.shape[0]
    st = _Stream(bsz, seq, ctx_len, d)
    assert bsz + 1 <= MOD_ROWS and seq % PEER_TILE == 0 and (bsz * ctx_len) % PEER_TILE == 0
    assert seq % TOKEN_TILE == 0 and ctx_len % TOKEN_TILE == 0

    cond = jnp.concatenate([c, c_ctx[None, :], jnp.zeros((MOD_ROWS - bsz - 1, d), F32)], axis=0)
    mods = _ada_mods(cond, ada_w, ada_b)
    wq_packed = _pack_weights(peer_w_q, transpose=True)
    u_packed = _pack_weights(peer_u, transpose=False)
    vt_packed = _pack_weights(peer_v, transpose=True)
    xs = jnp.concatenate([x.reshape(bsz * seq, d), ctx.reshape(bsz * ctx_len, d)], axis=0)

    for i in range(depth):
        kind, j = i % N_MIXERS, i // N_MIXERS
        with_ctx = i < depth - 1
        if kind == 0:
            xs = _sgu_layer(st, xs, mods[i], norm1_g[i], sgu_w_in[j], sgu_v_gain[j], sgu_w_s[j], sgu_b_s[j],
                            sgu_w_out[j], with_ctx)
        elif kind == 1:
            xs = _pool_layer(st, xs, mods[i], norm1_g[i], pool_w_in[j], pool_w_grp[j], pool_scale[j],
                             pool_w_out[j], with_ctx)
        else:
            xs = _attn_layer(st, xs, mods[i], norm1_g[i], attn_w_qkv[j], attn_q_gain[j], attn_k_gain[j],
                             attn_w_o[j], with_ctx)
        xs = _peer_layer(st, xs, mods[i], norm2_g[i], wq_packed[i], peer_k1[i], peer_k2[i], u_packed[i],
                         vt_packed[i], final_gain, with_ctx, final_norm=(i == depth - 1))
    return xs[:bsz * seq].reshape(bsz, seq, d)
```

```python
import functools
import math

import jax
import jax.numpy as jnp
from jax import lax
from jax.experimental import pallas as pl
from jax.experimental.pallas import tpu as pltpu

F32 = jnp.float32
BF16 = jnp.bfloat16

NORM_EPS = 1e-6
MOD_CHUNKS = 6
GRID_W = 64
SGU_CHUNK = 128
SGU_GROUPS = 8
POOL_WINDOWS = (2, 4, 8, 16)
HEAD_DIM = 128
N_KV_HEADS = 2
ROPE_THETA = 10000.0
PEER_HEADS = 8
PEER_KEYS = 128
PEER_TOPK = 16
N_MIXERS = 3

LANES = 128
MOD_ROWS = 16
VMEM_LIMIT = 56 * 1024 * 1024
FUSED_VMEM_LIMIT = 60 * 1024 * 1024

TOKEN_TILE = 256
PEER_TILE = 512
PEER_EXPERT_CHUNK = 1024
SELECT_UNROLL = 2

def _cparams(*sem):
    return pltpu.CompilerParams(dimension_semantics=sem, vmem_limit_bytes=VMEM_LIMIT)


def _rms(x):
    return x * lax.rsqrt(jnp.mean(x * x, axis=-1, keepdims=True) + NORM_EPS)


def _modulate(x, gain, shift, scale):
    return _rms(x) * gain * (1.0 + scale) + shift


def _gelu(x):
    return 0.5 * x * (1.0 + lax.erf(x * (1.0 / math.sqrt(2.0))))


def _dot(a, b):
    return jnp.dot(a, b, preferred_element_type=F32)


def _dot_nt(a, b):
    return lax.dot_general(a, b, (((1,), (1,)), ((), ())), preferred_element_type=F32)


def _ada_kernel(c_ref, w_ref, b_ref, o_ref):
    c = c_ref[...]
    a = c * (1.0 / (1.0 + jnp.exp(-c)))
    o_ref[0] = _dot(a.astype(BF16), w_ref[0].astype(BF16)) + b_ref[0]


def _ada_mods(cond, ada_w, ada_b):
    depth, d, n = ada_w.shape
    tn = n // 4
    out = pl.pallas_call(
        _ada_kernel,
        grid=(depth, n // tn),
        in_specs=[
            pl.BlockSpec((MOD_ROWS, d), lambda l, j: (0, 0)),
            pl.BlockSpec((1, d, tn), lambda l, j: (l, 0, j)),
            pl.BlockSpec((1, 1, tn), lambda l, j: (l, 0, j)),
        ],
        out_specs=pl.BlockSpec((1, MOD_ROWS, tn), lambda l, j: (l, 0, j)),
        out_shape=jax.ShapeDtypeStruct((depth, MOD_ROWS, n), F32),
        compiler_params=_cparams("parallel", "parallel"),
        name="ada_mods",
    )(cond, ada_w, ada_b.reshape(depth, 1, n))
    return out.reshape(depth, MOD_ROWS, MOD_CHUNKS, d)


class _Stream:
    def __init__(self, bsz, seq, ctx_len, d):
        self.bsz, self.seq, self.ctx_len, self.d = bsz, seq, ctx_len, d
        self.n_lat = bsz * seq
        self.n_tok = bsz * (seq + ctx_len)

    def tiles(self, tm, with_ctx):
        return (self.n_tok if with_ctx else self.n_lat) // tm

    def mod_row(self, tm):
        n_lat_tiles, per_seq, bsz = self.n_lat // tm, self.seq // tm, self.bsz
        return lambda t: jnp.where(t < n_lat_tiles, t // per_seq, bsz)

    def mod_spec(self, tm):
        row = self.mod_row(tm)
        return pl.BlockSpec((1, MOD_CHUNKS, self.d), lambda t, *_: (row(t), 0, 0))


def _tok_spec(tm, width):
    return pl.BlockSpec((tm, width), lambda t, *_: (t, 0))


def _full_spec(shape):
    zeros = (0,) * len(shape)
    return pl.BlockSpec(shape, lambda *_: zeros)


def _sgu_kernel(x_ref, mod_ref, g_ref, win_ref, vg_ref, ws_ref, bs_ref, wout_ref, o_ref):
    x = x_ref[...]
    mod = mod_ref[0]
    tm = x.shape[0]
    width = vg_ref.shape[1]
    gdim = width // SGU_GROUPS
    h = _modulate(x, g_ref[...], mod[0:1], mod[1:2]).astype(BF16)
    u = _gelu(_dot(h, win_ref[:, :width]))
    v = _gelu(_dot(h, win_ref[:, width:]))
    vn = (_rms(v) * vg_ref[...]).astype(BF16)
    rows = []
    for c in range(tm // SGU_CHUNK):
        cols = []
        for g in range(SGU_GROUPS):
            vv = vn[c * SGU_CHUNK:(c + 1) * SGU_CHUNK, g * gdim:(g + 1) * gdim]
            cols.append(_dot(ws_ref[g], vv) + bs_ref[:, g:g + 1])
        rows.append(jnp.concatenate(cols, axis=1))
    sv = jnp.concatenate(rows, axis=0)
    y = _dot((u * sv).astype(BF16), wout_ref[...])
    o_ref[...] = x + mod[2:3] * y


def _sgu_layer(st, xs, mods, gain, w_in, v_gain, w_s, b_s, w_out, with_ctx):
    tm, d = TOKEN_TILE, st.d
    width = v_gain.shape[0]
    return pl.pallas_call(
        _sgu_kernel,
        grid=(st.tiles(tm, with_ctx),),
        in_specs=[
            _tok_spec(tm, d), st.mod_spec(tm), _full_spec((1, d)),
            _full_spec((d, 2 * width)), _full_spec((1, width)),
            _full_spec((SGU_GROUPS, SGU_CHUNK, SGU_CHUNK)), _full_spec((SGU_CHUNK, SGU_GROUPS)),
            _full_spec((width, d)),
        ],
        out_specs=_tok_spec(tm, d),
        out_shape=jax.ShapeDtypeStruct((st.tiles(tm, with_ctx) * tm, d), F32),
        compiler_params=_cparams("parallel"),
        name="sgu_mixer",
    )(xs, mods, gain.reshape(1, d), w_in.astype(BF16), v_gain.reshape(1, width),
      w_s.astype(BF16), b_s.T, w_out.astype(BF16))


POOL_HALO = 8


def _pool_in_kernel(x_ref, mod_ref, g_ref, win_ref, z_ref):
    mod = mod_ref[0]
    h = _modulate(x_ref[...], g_ref[...], mod[0:1], mod[1:2]).astype(BF16)
    z_ref[...] = _dot(h, win_ref[...])


def _pool_out_kernel(z_ref, zp_ref, zn_ref, x_ref, mod_ref, wg_ref, sc_ref, wout_ref, o_ref,
                     *, n_lat_tiles, lat_tiles_per_seq, ctx_tiles_per_seq, seq, ctx_len):
    t = pl.program_id(0)
    tm = z_ref.shape[0]
    is_lat = t < n_lat_tiles
    per_seq = jnp.where(is_lat, lat_tiles_per_seq, ctx_tiles_per_seq)
    pos_tile = jnp.where(is_lat, t, t - n_lat_tiles) % per_seq
    length = jnp.where(is_lat, seq, ctx_len)
    has_prev = (pos_tile > 0).astype(F32)
    has_next = (pos_tile < per_seq - 1).astype(F32)
    z = z_ref[...]
    zext = jnp.concatenate([zp_ref[...] * has_prev, z, zn_ref[...] * has_next], axis=0)
    pos = (pos_tile * tm + lax.broadcasted_iota(jnp.int32, (tm, 1), 0))
    gdim = wg_ref.shape[1]
    n_ext = tm + 2 * POOL_HALO

    def shift_up(a, k):
        return a if k == 0 else pltpu.roll(a, n_ext - k, 0)

    outs = []
    for g, w in enumerate(POOL_WINDOWS):
        half = w // 2
        acc = zext[:, g * gdim:(g + 1) * gdim]
        span = 1
        while span < w:
            acc = acc + shift_up(acc, span)
            span *= 2
        win_sum = shift_up(acc, POOL_HALO - half)[:tm]
        cnt = (jnp.minimum(pos + half, length) - jnp.maximum(pos - half, 0)).astype(F32)
        pooled = win_sum / cnt - z[:, g * gdim:(g + 1) * gdim]
        outs.append(_dot(pooled.astype(BF16), wg_ref[g]))
    y = (jnp.concatenate(outs, axis=1) * sc_ref[...]).astype(BF16)
    o_ref[...] = x_ref[...] + mod_ref[0][2:3] * _dot(y, wout_ref[...])


def _pool_layer(st, xs, mods, gain, w_in, w_grp, scale, w_out, with_ctx):
    tm, d = TOKEN_TILE, st.d
    width = w_in.shape[1]
    n_tiles = st.tiles(tm, with_ctx)
    z = pl.pallas_call(
        _pool_in_kernel,
        grid=(n_tiles,),
        in_specs=[_tok_spec(tm, d), st.mod_spec(tm), _full_spec((1, d)), _full_spec((d, width))],
        out_specs=_tok_spec(tm, width),
        out_shape=jax.ShapeDtypeStruct((n_tiles * tm, width), F32),
        compiler_params=_cparams("parallel"),
        name="pool_in",
    )(xs, mods, gain.reshape(1, d), w_in.astype(BF16))
    halo_per_tile = tm // POOL_HALO
    last_halo = n_tiles * halo_per_tile - 1
    kern = functools.partial(
        _pool_out_kernel, n_lat_tiles=st.n_lat // tm, lat_tiles_per_seq=st.seq // tm,
        ctx_tiles_per_seq=st.ctx_len // tm, seq=st.seq, ctx_len=st.ctx_len)
    return pl.pallas_call(
        kern,
        grid=(n_tiles,),
        in_specs=[
            _tok_spec(tm, width),
            pl.BlockSpec((POOL_HALO, width), lambda t: (jnp.maximum(t * halo_per_tile - 1, 0), 0)),
            pl.BlockSpec((POOL_HALO, width), lambda t: (jnp.minimum((t + 1) * halo_per_tile, last_halo), 0)),
            _tok_spec(tm, d), st.mod_spec(tm),
            _full_spec(w_grp.shape), _full_spec((1, width)), _full_spec((width, d)),
        ],
        out_specs=_tok_spec(tm, d),
        out_shape=jax.ShapeDtypeStruct((st.tiles(tm, with_ctx) * tm, d), F32),
        compiler_params=_cparams("parallel"),
        name="pool_out",
    )(z, z, z, xs, mods, w_grp.astype(BF16), scale.reshape(1, width), w_out.astype(BF16))


def _rope_tables(seq, tm):
    pos = jnp.arange(seq)
    axis_dim = HEAD_DIM // 2
    freqs = ROPE_THETA ** (-jnp.arange(0, axis_dim, 2, dtype=F32) / axis_dim)
    ang = jnp.stack([pos // GRID_W, pos % GRID_W], axis=-1).astype(F32)[:, :, None] * freqs
    cos, sin = jnp.cos(ang), jnp.sin(ang)
    zero = jnp.zeros_like(sin)
    c = jnp.concatenate([cos, cos], axis=-1).reshape(seq, HEAD_DIM)
    s_lo = jnp.concatenate([-sin, zero], axis=-1).reshape(seq, HEAD_DIM)
    s_hi = jnp.concatenate([zero, sin], axis=-1).reshape(seq, HEAD_DIM)
    pad = jnp.zeros((tm, HEAD_DIM), F32)
    return (jnp.concatenate([c, pad + 1.0]), jnp.concatenate([s_lo, pad]), jnp.concatenate([s_hi, pad]))


def _qkv_kernel(x_ref, mod_ref, g_ref, w_ref, qg_ref, kg_ref, c_ref, slo_ref, shi_ref, q_ref, k_ref, v_ref):
    mod = mod_ref[0]
    h = _modulate(x_ref[...], g_ref[...], mod[0:1], mod[1:2]).astype(BF16)
    qkv = _dot(h, w_ref[...])
    nq, nk = q_ref.shape[1], k_ref.shape[1]
    cos, s_lo, s_hi = c_ref[...], slo_ref[...], shi_ref[...]
    quarter = HEAD_DIM // 4

    def norm_rope(t, gain):
        t = _rms(t) * gain
        return (t * cos + pltpu.roll(t, HEAD_DIM - quarter, 1) * s_lo + pltpu.roll(t, quarter, 1) * s_hi)

    for hh in range(nq // HEAD_DIM):
        sl = slice(hh * HEAD_DIM, (hh + 1) * HEAD_DIM)
        q_ref[:, sl] = norm_rope(qkv[:, sl], qg_ref[...]).astype(BF16)
    for hh in range(nk // HEAD_DIM):
        sl = slice(hh * HEAD_DIM, (hh + 1) * HEAD_DIM)
        k_ref[:, sl] = norm_rope(qkv[:, nq + hh * HEAD_DIM:nq + (hh + 1) * HEAD_DIM], kg_ref[...]).astype(BF16)
    v_ref[...] = qkv[:, nq + nk:].astype(BF16)


def _attn_kernel(q_ref, kl_ref, vl_ref, kc_ref, vc_ref, x_ref, mod_ref, wo_ref, o_ref, *, n_lat_tiles):
    n_heads = q_ref.shape[1] // HEAD_DIM
    q_per_kv = n_heads // N_KV_HEADS
    scale = HEAD_DIM ** -0.5

    def body(with_lat):
        outs = []
        for hh in range(n_heads):
            kv = slice((hh // q_per_kv) * HEAD_DIM, (hh // q_per_kv + 1) * HEAD_DIM)
            q = q_ref[:, hh * HEAD_DIM:(hh + 1) * HEAD_DIM]
            s_c = _dot_nt(q, kc_ref[:, kv]) * scale
            m = jnp.max(s_c, axis=-1, keepdims=True)
            if with_lat:
                s_l = _dot_nt(q, kl_ref[:, kv]) * scale
                m = jnp.maximum(m, jnp.max(s_l, axis=-1, keepdims=True))
            p_c = jnp.exp(s_c - m)
            den = jnp.sum(p_c, axis=-1, keepdims=True)
            o = _dot(p_c.astype(BF16), vc_ref[:, kv])
            if with_lat:
                p_l = jnp.exp(s_l - m)
                den = den + jnp.sum(p_l, axis=-1, keepdims=True)
                o = o + _dot(p_l.astype(BF16), vl_ref[:, kv])
            outs.append(o / den)
        y = _dot(jnp.concatenate(outs, axis=1).astype(BF16), wo_ref[...])
        o_ref[...] = x_ref[...] + mod_ref[0][2:3] * y

    t = pl.program_id(0)
    pl.when(t < n_lat_tiles)(lambda: body(True))
    pl.when(t >= n_lat_tiles)(lambda: body(False))


def _attn_layer(st, xs, mods, gain, w_qkv, q_gain, k_gain, w_o, with_ctx):
    d = st.d
    tm = st.ctx_len
    assert st.seq % tm == 0
    n_all = st.n_tok // tm
    n_lat_tiles, per_seq = st.n_lat // tm, st.seq // tm
    nq = w_o.shape[0]
    nk = (w_qkv.shape[1] - nq) // 2
    cos, s_lo, s_hi = _rope_tables(st.seq, tm)
    rope_spec = pl.BlockSpec((tm, HEAD_DIM), lambda t: (jnp.where(t < n_lat_tiles, t % per_seq, per_seq), 0))
    q, k, v = pl.pallas_call(
        _qkv_kernel,
        grid=(n_all,),
        in_specs=[
            _tok_spec(tm, d), st.mod_spec(tm), _full_spec((1, d)), _full_spec(w_qkv.shape),
            _full_spec((1, HEAD_DIM)), _full_spec((1, HEAD_DIM)), rope_spec, rope_spec, rope_spec,
        ],
        out_specs=[_tok_spec(tm, nq), _tok_spec(tm, nk), _tok_spec(tm, nk)],
        out_shape=[jax.ShapeDtypeStruct((st.n_tok, nq), BF16), jax.ShapeDtypeStruct((st.n_tok, nk), BF16),
                   jax.ShapeDtypeStruct((st.n_tok, nk), BF16)],
        compiler_params=_cparams("parallel"),
        name="attn_qkv",
    )(xs, mods, gain.reshape(1, d), w_qkv.astype(BF16), q_gain.reshape(1, HEAD_DIM),
      k_gain.reshape(1, HEAD_DIM), cos, s_lo, s_hi)

    bsz = st.bsz
    batch_of = lambda t: jnp.where(t < n_lat_tiles, t // per_seq, t - n_lat_tiles)
    lat_spec = pl.BlockSpec((st.seq, nk), lambda t: (jnp.minimum(t // per_seq, bsz - 1), 0))
    ctx_spec = pl.BlockSpec((tm, nk), lambda t: (n_lat_tiles + batch_of(t), 0))
    return pl.pallas_call(
        functools.partial(_attn_kernel, n_lat_tiles=n_lat_tiles),
        grid=(st.tiles(tm, with_ctx),),
        in_specs=[_tok_spec(tm, nq), lat_spec, lat_spec, ctx_spec, ctx_spec,
                  _tok_spec(tm, d), st.mod_spec(tm), _full_spec((nq, d))],
        out_specs=_tok_spec(tm, d),
        out_shape=jax.ShapeDtypeStruct((st.tiles(tm, with_ctx) * tm, d), F32),
        compiler_params=_cparams("parallel"),
        name="attn_core",
    )(q, k, v, k, v, xs, mods, w_o.astype(BF16))


_INT_MIN = -2 ** 31


def _sortable(x):
    b = lax.bitcast_convert_type(x, jnp.int32)
    return b ^ ((b >> 31) & 0x7FFFFFFF)


def _unsortable(k):
    return lax.bitcast_convert_type(k ^ ((k >> 31) & 0x7FFFFFFF), F32)


def _extract16(k, iota, exact):
    n = k.shape[0]
    tops = []
    for a in range(PEER_TOPK):
        m = jnp.max(k, axis=0, keepdims=True)
        hit = k == m
        if exact:
            first = jnp.min(jnp.where(hit, iota, n), axis=0, keepdims=True)
            hit = iota == first
        k = jnp.where(hit, _INT_MIN + a, k)
        tops.append(m)
    return k, tops


def _rows16(tops, iota16):
    out = jnp.zeros(iota16.shape, tops[0].dtype)
    for a, t in enumerate(tops):
        out = jnp.where(iota16 == a, t, out)
    return out


_CAND_HALF = PEER_TOPK // 2


def _peer_select(s1, s2, exact):
    n, width = s1.shape
    topk = float(PEER_TOPK)
    iota = lax.broadcasted_iota(jnp.int32, (n, width), 0)
    iota16 = iota[0:PEER_TOPK]
    mark_end = _INT_MIN + PEER_TOPK
    k1, top1 = _extract16(_sortable(s1), iota, exact)
    k2, top2 = _extract16(_sortable(s2), iota, exact)
    v1 = _unsortable(_rows16(top1, iota16))
    v2 = _unsortable(_rows16(top2, iota16))
    f1 = [_unsortable(t) for t in top1[:_CAND_HALF]]
    f2_0 = _unsortable(top2[0])
    pieces = [v2 + f1[0]]
    pieces += [v2[0:_CAND_HALF] + f1[a] for a in range(1, _CAND_HALF)]
    pieces.append(v1[_CAND_HALF:] + f2_0)
    cand = jnp.concatenate(pieces, axis=0)
    nc = cand.shape[0]
    kc, _ = _extract16(_sortable(cand), iota[0:nc], exact)
    picked = kc < mark_end
    if not exact:
        surplus = jnp.sum(jnp.where(picked, 1.0, 0.0), axis=0, keepdims=True) - topk
        picked = kc < mark_end - surplus.astype(jnp.int32)
    sel = jnp.where(picked, 1.0, 0.0)
    z = jnp.sum(jnp.where(picked, jnp.exp(cand - (f1[0] + f2_0)), 0.0), axis=0, keepdims=True)
    cnt = [jnp.sum(sel[0:PEER_TOPK], axis=0, keepdims=True)]
    for a in range(1, _CAND_HALF):
        row = PEER_TOPK + (a - 1) * _CAND_HALF
        cnt.append(jnp.sum(sel[row:row + _CAND_HALF], axis=0, keepdims=True))
    tail = sel[nc - _CAND_HALF:]
    for a in range(_CAND_HALF, PEER_TOPK):
        pick = iota[0:_CAND_HALF] == a - _CAND_HALF
        cnt.append(jnp.sum(jnp.where(pick, tail, 0.0), axis=0, keepdims=True))
    cut = jnp.zeros(s1.shape, F32)
    for a in range(PEER_TOPK):
        cut = jnp.where(k1 == _INT_MIN + a, cnt[a], cut)
    in1, in2 = k1 < mark_end, k2 < mark_end
    r2 = jnp.where(in2, (k2 - _INT_MIN).astype(F32), topk)
    e1 = jnp.exp(s1 - f1[0])
    e2 = jnp.exp(s2 - f2_0) / z
    n1 = jnp.sum(jnp.where(in1, 1.0, 0.0), axis=0, keepdims=True)
    n2 = jnp.sum(jnp.where(in2, 1.0, 0.0), axis=0, keepdims=True)
    nsel = jnp.sum(sel, axis=0, keepdims=True)
    bad = jnp.where((n1 != topk) | (n2 != topk) | (nsel != topk), 1.0, 0.0)
    return cut, e1, r2, e2, bad


def _pack_rows(x):
    return pltpu.bitcast(x.astype(BF16), jnp.uint32)


def _unpack_rows(x):
    return pltpu.bitcast(x, BF16)


def _pack_twice(x):
    u = lax.bitcast_convert_type(x.astype(BF16).astype(F32), jnp.uint32)
    return u | (u >> 16)


PACK_BLOCK = 1024


def _pack_kernel(w_ref, o_ref, *, transpose):
    w = w_ref[0]
    o_ref[0] = _pack_rows(w.T if transpose else w)


def _pack_weights(w, transpose):
    layers, r, c = w.shape
    pb = PACK_BLOCK
    out_rows, out_cols = (c, r) if transpose else (r, c)
    out_map = (lambda l, i, j: (l, j, i)) if transpose else (lambda l, i, j: (l, i, j))
    return pl.pallas_call(
        functools.partial(_pack_kernel, transpose=transpose),
        grid=(layers, r // pb, c // pb),
        in_specs=[pl.BlockSpec((1, pb, pb), lambda l, i, j: (l, i, j))],
        out_specs=pl.BlockSpec((1, pb // 2, pb), out_map),
        out_shape=jax.ShapeDtypeStruct((layers, out_rows // 2, out_cols), jnp.uint32),
        compiler_params=_cparams("parallel", "parallel", "parallel"),
        name="pack_t" if transpose else "pack",
    )(w)


def _peer_select_kernel(x_ref, mod_ref, g_ref, wq_ref, k1_ref, k2_ref,
                        h_ref, cut_ref, e1_ref, r2_ref, e2_ref, s1_scr, s2_scr):
    mod = mod_ref[0]
    f_t = _modulate(x_ref[...], g_ref[...], mod[3:4], mod[4:5]).T.astype(BF16)
    h_ref[...] = _pack_rows(f_t)
    tm = f_t.shape[1]
    n_heads, n_keys, half = k1_ref.shape
    q_t = _dot(_unpack_rows(wq_ref[...]), f_t)
    for hh in range(n_heads):
        base = hh * 2 * half
        s1_scr[hh] = _dot(k1_ref[hh], q_t[base:base + half].astype(BF16))
        s2_scr[hh] = _dot(k2_ref[hh], q_t[base + half:base + 2 * half].astype(BF16))
    n_sub = tm // LANES

    def step(idx, carry):
        hh = idx // (n_sub // SELECT_UNROLL)
        base = (idx % (n_sub // SELECT_UNROLL)) * SELECT_UNROLL
        bad_any = jnp.zeros((1, LANES), F32)
        work = []
        for sub in range(SELECT_UNROLL):
            lanes = pl.ds(pl.multiple_of((base + sub) * LANES, LANES), LANES)
            s1, s2 = s1_scr[hh, :, lanes], s2_scr[hh, :, lanes]
            *fast, bad = _peer_select(s1, s2, exact=False)
            work.append((lanes, s1, s2, fast))
            bad_any = jnp.maximum(bad_any, bad)

        def store(lanes, cut, e1, r2, e2):
            cut_ref[hh, :, lanes] = _pack_twice(cut)
            e1_ref[hh, :, lanes] = _pack_twice(e1)
            r2_ref[hh, :, lanes] = _pack_rows(r2)
            e2_ref[hh, :, lanes] = _pack_rows(e2)

        for lanes, _, _, fast in work:
            store(lanes, *fast)

        @pl.when(jnp.max(bad_any) > 0.0)
        def _():
            for lanes, s1, s2, _ in work:
                store(lanes, *_peer_select(s1, s2, exact=True)[:4])

        return carry

    lax.fori_loop(0, n_heads * n_sub // SELECT_UNROLL, step, 0)


def _peer_main_kernel(h_ref, u_ref, vt_ref, cut_ref, e1_ref, r2_ref, e2_ref, x_ref, mod_ref, fg_ref,
                      o_ref, acc, act0, act1, w0, w1, *, final_norm, n_chunks):
    g = pl.program_id(0)
    ec, tm = act0.shape
    n_heads, n_keys = r2_ref.shape[0], 2 * r2_ref.shape[1]
    rows_per_chunk = ec // n_keys
    half_keys = n_keys // 2
    packed_half = half_keys // 2

    @pl.when(g == 0)
    def _():
        acc[...] = jnp.zeros_like(acc)
        act1[...] = jnp.zeros_like(act1)
        w0[...] = jnp.zeros_like(w0)

    def gate_chunk(cut_ref, e1_ref, act_ref, w_ref):
        zero = jnp.zeros((half_keys, LANES), BF16)
        for part in range(n_keys // half_keys):
            prow = slice(part * packed_half, (part + 1) * packed_half)
            for tl in range(tm // LANES):
                lanes = slice(tl * LANES, (tl + 1) * LANES)
                gates = [zero] * rows_per_chunk
                for hh in range(n_heads):
                    r2 = _unpack_rows(r2_ref[hh, prow, lanes])
                    e2 = _unpack_rows(e2_ref[hh, prow, lanes])
                    for il in range(rows_per_chunk):
                        cut = _unpack_rows(jnp.broadcast_to(cut_ref[hh, il:il + 1, lanes], (packed_half, LANES)))
                        e1 = _unpack_rows(jnp.broadcast_to(e1_ref[hh, il:il + 1, lanes], (packed_half, LANES)))
                        gates[il] = gates[il] + jnp.where(r2 < cut, e2, zero) * e1
                for il in range(rows_per_chunk):
                    rows = slice(il * n_keys + part * half_keys, il * n_keys + (part + 1) * half_keys)
                    w_ref[rows, lanes] = _gelu(act_ref[rows, lanes]).astype(BF16) * gates[il]

    def stage(act_new, act_old, w_new, w_old):
        act_new[...] = _dot(_unpack_rows(u_ref[...]), _unpack_rows(h_ref[...]))
        gate_chunk(cut_ref, e1_ref, act_old, w_new)
        acc[...] += _dot(_unpack_rows(vt_ref[...]), w_old[...])

    pl.when(g % 2 == 0)(lambda: stage(act0, act1, w1, w0))
    pl.when(g % 2 == 1)(lambda: stage(act1, act0, w0, w1))

    @pl.when((g >= 2) & ((g - 2) % n_chunks == n_chunks - 1))
    def _():
        y = x_ref[...] + mod_ref[0][5:6] * acc[...].T
        if final_norm:
            y = _rms(y) * fg_ref[...]
        o_ref[...] = y
        acc[...] = jnp.zeros_like(acc)


def _peer_layer(st, xs, mods, gain, wq_packed, k1, k2, u_packed, vt_packed, final_gain, with_ctx, final_norm):
    tm, d = PEER_TILE, st.d
    n_heads, n_keys, half = k1.shape
    n_exp = 2 * u_packed.shape[0]
    ec = PEER_EXPERT_CHUNK
    n_tiles = st.tiles(tm, with_ctx)
    n_rows = n_tiles * tm
    u32 = jnp.uint32
    h_spec = pl.BlockSpec((d // 2, tm), lambda t, *_: (0, t))
    key_spec = pl.BlockSpec((n_heads, n_keys, tm), lambda t, *_: (0, 0, t))
    key_shape = jax.ShapeDtypeStruct((n_heads, n_keys, n_rows), u32)
    pair_spec = pl.BlockSpec((n_heads, n_keys // 2, tm), lambda t, *_: (0, 0, t))
    pair_shape = jax.ShapeDtypeStruct((n_heads, n_keys // 2, n_rows), u32)
    h, cut, e1, r2, e2 = pl.pallas_call(
        _peer_select_kernel,
        grid=(n_tiles,),
        in_specs=[_tok_spec(tm, d), st.mod_spec(tm), _full_spec((1, d)), _full_spec(wq_packed.shape),
                  _full_spec(k1.shape), _full_spec(k2.shape)],
        out_specs=[h_spec, key_spec, key_spec, pair_spec, pair_spec],
        out_shape=[jax.ShapeDtypeStruct((d // 2, n_rows), u32), key_shape, key_shape, pair_shape, pair_shape],
        scratch_shapes=[pltpu.VMEM((n_heads, n_keys, tm), F32), pltpu.VMEM((n_heads, n_keys, tm), F32)],
        compiler_params=_cparams("parallel"),
        name="peer_select",
    )(xs, mods, gain.reshape(1, d), wq_packed, k1.astype(BF16), k2.astype(BF16))

    n_chunks = n_exp // ec
    n_items = n_tiles * n_chunks
    rows_per_chunk = ec // n_keys
    mod_row = st.mod_row(tm)

    def item(lag):
        def tile(g):
            return jnp.clip(g - lag, 0, n_items - 1) // n_chunks

        def chunk(g):
            return jnp.clip(g - lag, 0, n_items - 1) % n_chunks

        return tile, chunk

    (tile0, chunk0), (tile1, chunk1), (tile2, chunk2) = item(0), item(1), item(2)
    sub1_spec = pl.BlockSpec((n_heads, rows_per_chunk, tm), lambda g: (0, chunk1(g), tile1(g)))
    sub2_spec = pl.BlockSpec((n_heads, n_keys // 2, tm), lambda g: (0, 0, tile1(g)))
    return pl.pallas_call(
        functools.partial(_peer_main_kernel, final_norm=final_norm, n_chunks=n_chunks),
        grid=(n_items + 2,),
        in_specs=[
            pl.BlockSpec((d // 2, tm), lambda g: (0, tile0(g))),
            pl.BlockSpec((ec // 2, d), lambda g: (chunk0(g), 0)),
            pl.BlockSpec((d // 2, ec), lambda g: (0, chunk2(g))),
            sub1_spec, sub1_spec, sub2_spec, sub2_spec,
            pl.BlockSpec((tm, d), lambda g: (tile2(g), 0)),
            pl.BlockSpec((1, MOD_CHUNKS, d), lambda g: (mod_row(tile2(g)), 0, 0)),
            _full_spec((1, d)),
        ],
        out_specs=pl.BlockSpec((tm, d), lambda g: (tile2(g), 0)),
        out_shape=jax.ShapeDtypeStruct((n_rows, d), F32),
        scratch_shapes=[pltpu.VMEM((d, tm), F32),
                        pltpu.VMEM((ec, tm), F32), pltpu.VMEM((ec, tm), F32),
                        pltpu.VMEM((ec, tm), BF16), pltpu.VMEM((ec, tm), BF16)],
        compiler_params=_cparams("arbitrary"),
        name="peer_main",
    )(h, u_packed, vt_packed, cut, e1, r2, e2, xs, mods, final_gain.reshape(1, d))


def _peer_fused_kernel(xn_ref, modn_ref, g_ref, wq_ref, k1_ref, k2_ref, u_ref, vt_ref, x_ref, mod_ref, fg_ref,
                       o_ref, acc, act0, act1, w0, w1, ht_buf, q_buf, cut_buf, e1_buf, r2_buf, e2_buf,
                       *, final_norm, n_chunks, n_items):
    g = pl.program_id(0)
    ec, tm = act0.shape
    n_heads, n_keys, half = k1_ref.shape
    rows_per_chunk = ec // n_keys
    half_keys = n_keys // 2
    packed_half = half_keys // 2
    subs_per_head = tm // LANES // SELECT_UNROLL
    assert n_heads * subs_per_head == n_chunks

    last = n_items - 1
    sel_item = jnp.clip(g - 1, 0, last)
    sel_slot, sel_iter = (sel_item // n_chunks) % 2, sel_item % n_chunks
    slot_a = (jnp.clip(g - n_chunks, 0, last) // n_chunks) % 2
    item_b = jnp.clip(g - n_chunks - 1, 0, last)
    slot_b, chunk_b = (item_b // n_chunks) % 2, item_b % n_chunks

    @pl.when(g == 0)
    def _():
        acc[...] = jnp.zeros_like(acc)
        act1[...] = jnp.zeros_like(act1)
        w0[...] = jnp.zeros_like(w0)

    @pl.when((g % n_chunks == 0) & (g < n_items))
    def _():
        slot = (g // n_chunks) % 2
        mod = modn_ref[0]
        f_t = _modulate(xn_ref[...], g_ref[...], mod[3:4], mod[4:5]).T.astype(BF16)
        ht_buf[slot] = _pack_rows(f_t)
        q_buf[slot] = _dot(_unpack_rows(wq_ref[...]), f_t).astype(BF16)

    def gate_chunk(act_ref, w_ref):
        zero = jnp.zeros((half_keys, LANES), BF16)
        for part in range(n_keys // half_keys):
            prow = slice(part * packed_half, (part + 1) * packed_half)
            for tl in range(tm // LANES):
                lanes = slice(tl * LANES, (tl + 1) * LANES)
                gates = [zero] * rows_per_chunk
                for hh in range(n_heads):
                    r2 = _unpack_rows(r2_buf[slot_b, hh, prow, lanes])
                    e2 = _unpack_rows(e2_buf[slot_b, hh, prow, lanes])
                    for il in range(rows_per_chunk):
                        cut = cut_buf[slot_b, chunk_b, hh, il:il + 1, lanes]
                        e1 = e1_buf[slot_b, chunk_b, hh, il:il + 1, lanes]
                        cut = _unpack_rows(jnp.broadcast_to(cut, (packed_half, LANES)))
                        e1 = _unpack_rows(jnp.broadcast_to(e1, (packed_half, LANES)))
                        gates[il] = gates[il] + jnp.where(r2 < cut, e2, zero) * e1
                for il in range(rows_per_chunk):
                    rows = slice(il * n_keys + part * half_keys, il * n_keys + (part + 1) * half_keys)
                    w_ref[rows, lanes] = _gelu(act_ref[rows, lanes]).astype(BF16) * gates[il]

    def select_iteration(dense=None):
        hh = sel_iter // subs_per_head
        width = SELECT_UNROLL * LANES
        lanes = pl.ds(pl.multiple_of((sel_iter % subs_per_head) * width, width), width)
        qrow = pl.multiple_of(hh * 2 * half, 2 * half)
        s1_all = _dot(k1_ref[hh], q_buf[sel_slot, pl.ds(qrow, half), lanes])
        s2_all = _dot(k2_ref[hh], q_buf[sel_slot, pl.ds(qrow + half, half), lanes])
        groups = n_keys // rows_per_chunk

        def store(results):
            cut, e1, r2, e2 = (jnp.concatenate(parts, axis=1) for parts in zip(*results))
            cut_buf[sel_slot, :, hh, :, lanes] = _pack_twice(cut).reshape(groups, rows_per_chunk, width)
            e1_buf[sel_slot, :, hh, :, lanes] = _pack_twice(e1).reshape(groups, rows_per_chunk, width)
            r2_buf[sel_slot, hh, :, lanes] = _pack_rows(r2)
            e2_buf[sel_slot, hh, :, lanes] = _pack_rows(e2)

        scores = [(s1_all[:, sub * LANES:(sub + 1) * LANES], s2_all[:, sub * LANES:(sub + 1) * LANES])
                  for sub in range(SELECT_UNROLL)]
        fast = [_peer_select(s1, s2, exact=False) for s1, s2 in scores]
        store([f[:4] for f in fast])
        bad = functools.reduce(jnp.maximum, [f[4] for f in fast])
        if dense is not None:
            dense()

        @pl.when(jnp.max(bad) > 0.0)
        def _():
            store([_peer_select(s1, s2, exact=True)[:4] for s1, s2 in scores])

    def dense_stage(act_new, act_old, w_new, w_old):
        act_new[...] = _dot(_unpack_rows(u_ref[...]), _unpack_rows(ht_buf[slot_a]))
        gate_chunk(act_old, w_new)
        acc[...] += _dot(_unpack_rows(vt_ref[...]), w_old[...])

    @pl.when(g < n_chunks)
    def _():
        select_iteration()

    @pl.when((g >= n_chunks) & (g % 2 == 0))
    def _():
        select_iteration(lambda: dense_stage(act0, act1, w1, w0))

    @pl.when((g >= n_chunks) & (g % 2 == 1))
    def _():
        select_iteration(lambda: dense_stage(act1, act0, w0, w1))

    done = g - n_chunks - 2
    @pl.when((done >= 0) & (done % n_chunks == n_chunks - 1))
    def _():
        y = x_ref[...] + mod_ref[0][5:6] * acc[...].T
        if final_norm:
            y = _rms(y) * fg_ref[...]
        o_ref[...] = y
        acc[...] = jnp.zeros_like(acc)


def _peer_fused_layer(st, xs, mods, gain, wq_packed, k1, k2, u_packed, vt_packed, final_gain, with_ctx,
                      final_norm):
    tm, d = PEER_TILE, st.d
    n_heads, n_keys, half = k1.shape
    n_exp = 2 * u_packed.shape[0]
    ec = PEER_EXPERT_CHUNK
    n_tiles = st.tiles(tm, with_ctx)
    n_chunks = n_exp // ec
    n_items = n_tiles * n_chunks
    rows_per_chunk = ec // n_keys
    mod_row = st.mod_row(tm)
    u32 = jnp.uint32

    def lagged(lag):
        def tile(g):
            return jnp.clip(g - lag, 0, n_items - 1) // n_chunks

        def chunk(g):
            return jnp.clip(g - lag, 0, n_items - 1) % n_chunks

        return tile, chunk

    sel_tile, _ = lagged(0)
    _, chunk_a = lagged(n_chunks)
    tile_c, chunk_c = lagged(n_chunks + 2)
    return pl.pallas_call(
        functools.partial(_peer_fused_kernel, final_norm=final_norm, n_chunks=n_chunks, n_items=n_items),
        grid=(n_items + n_chunks + 2,),
        in_specs=[
            pl.BlockSpec((tm, d), lambda g: (sel_tile(g), 0)),
            pl.BlockSpec((1, MOD_CHUNKS, d), lambda g: (mod_row(sel_tile(g)), 0, 0)),
            _full_spec((1, d)), _full_spec(wq_packed.shape), _full_spec(k1.shape), _full_spec(k2.shape),
            pl.BlockSpec((ec // 2, d), lambda g: (chunk_a(g), 0)),
            pl.BlockSpec((d // 2, ec), lambda g: (0, chunk_c(g))),
            pl.BlockSpec((tm, d), lambda g: (tile_c(g), 0)),
            pl.BlockSpec((1, MOD_CHUNKS, d), lambda g: (mod_row(tile_c(g)), 0, 0)),
            _full_spec((1, d)),
        ],
        out_specs=pl.BlockSpec((tm, d), lambda g: (tile_c(g), 0)),
        out_shape=jax.ShapeDtypeStruct((n_tiles * tm, d), F32),
        scratch_shapes=[
            pltpu.VMEM((d, tm), F32),
            pltpu.VMEM((ec, tm), F32), pltpu.VMEM((ec, tm), F32),
            pltpu.VMEM((ec, tm), BF16), pltpu.VMEM((ec, tm), BF16),
            pltpu.VMEM((2, d // 2, tm), u32),
            pltpu.VMEM((2, 2 * half * n_heads, tm), BF16),
            pltpu.VMEM((2, n_keys // rows_per_chunk, n_heads, rows_per_chunk, tm), u32),
            pltpu.VMEM((2, n_keys // rows_per_chunk, n_heads, rows_per_chunk, tm), u32),
            pltpu.VMEM((2, n_heads, n_keys // 2, tm), u32),
            pltpu.VMEM((2, n_heads, n_keys // 2, tm), u32),
        ],
        compiler_params=pltpu.CompilerParams(dimension_semantics=("arbitrary",),
                                             vmem_limit_bytes=FUSED_VMEM_LIMIT),
        name="peer_fused",
    )(xs, mods, gain.reshape(1, d), wq_packed, k1.astype(BF16), k2.astype(BF16), u_packed, vt_packed,
      xs, mods, final_gain.reshape(1, d))


def kernel(x, c, ctx, c_ctx, ada_w, ada_b, norm1_g, norm2_g, sgu_w_in, sgu_v_gain, sgu_w_s, sgu_b_s, sgu_w_out, pool_w_in, pool_w_grp, pool_scale, pool_w_out, attn_w_qkv, attn_q_gain, attn_k_gain, attn_w_o, peer_w_q, peer_k1, peer_k2, peer_u, peer_v, final_gain):
    bsz, seq, d = x.shape
    ctx_len = ctx.shape[1]
    depth = ada_w.shape[0]
    st = _Stream(bsz, seq, ctx_len, d)
    assert bsz + 1 <= MOD_ROWS and seq % PEER_TILE == 0 and (bsz * ctx_len) % PEER_TILE == 0
    assert seq % TOKEN_TILE == 0 and ctx_len % TOKEN_TILE == 0

    cond = jnp.concatenate([c, c_ctx[None, :], jnp.zeros((MOD_ROWS - bsz - 1, d), F32)], axis=0)
    mods = _ada_mods(cond, ada_w, ada_b)
    wq_packed = _pack_weights(peer_w_q, transpose=True)
    u_packed = _pack_weights(peer_u, transpose=False)
    vt_packed = _pack_weights(peer_v, transpose=True)
    xs = jnp.concatenate([x.reshape(bsz * seq, d), ctx.reshape(bsz * ctx_len, d)], axis=0)

    for i in range(depth):
        kind, j = i % N_MIXERS, i // N_MIXERS
        with_ctx = i < depth - 1
        if kind == 0:
            xs = _sgu_layer(st, xs, mods[i], norm1_g[i], sgu_w_in[j], sgu_v_gain[j], sgu_w_s[j], sgu_b_s[j],
                            sgu_w_out[j], with_ctx)
        elif kind == 1:
            xs = _pool_layer(st, xs, mods[i], norm1_g[i], pool_w_in[j], pool_w_grp[j], pool_scale[j],
                             pool_w_out[j], with_ctx)
        else:
            xs = _attn_layer(st, xs, mods[i], norm1_g[i], attn_w_qkv[j], attn_q_gain[j], attn_k_gain[j],
                             attn_w_o[j], with_ctx)
        xs = _peer_layer(st, xs, mods[i], norm2_g[i], wq_packed[i], peer_k1[i], peer_k2[i], u_packed[i],
                         vt_packed[i], final_gain, with_ctx, final_norm=(i == depth - 1))
    return xs[:bsz * seq].reshape(bsz, seq, d)
```

```python
import functools
import math

import jax
import jax.numpy as jnp
from jax import lax
from jax.experimental import pallas as pl
from jax.experimental.pallas import tpu as pltpu

F32 = jnp.float32
BF16 = jnp.bfloat16

NORM_EPS = 1e-6
MOD_CHUNKS = 6
GRID_W = 64
SGU_CHUNK = 128
SGU_GROUPS = 8
POOL_WINDOWS = (2, 4, 8, 16)
HEAD_DIM = 128
N_KV_HEADS = 2
ROPE_THETA = 10000.0
PEER_HEADS = 8
PEER_KEYS = 128
PEER_TOPK = 16
N_MIXERS = 3

LANES = 128
MOD_ROWS = 16
VMEM_LIMIT = 56 * 1024 * 1024
FUSED_VMEM_LIMIT = 60 * 1024 * 1024

TOKEN_TILE = 256
PEER_TILE = 512
PEER_EXPERT_CHUNK = 1024
SELECT_UNROLL = 2

def _cparams(*sem):
    return pltpu.CompilerParams(dimension_semantics=sem, vmem_limit_bytes=VMEM_LIMIT)


def _rms(x):
    return x * lax.rsqrt(jnp.mean(x * x, axis=-1, keepdims=True) + NORM_EPS)


def _modulate(x, gain, shift, scale):
    return _rms(x) * gain * (1.0 + scale) + shift


def _gelu(x):
    return 0.5 * x * (1.0 + lax.erf(x * (1.0 / math.sqrt(2.0))))


def _dot(a, b):
    return jnp.dot(a, b, preferred_element_type=F32)


def _dot_nt(a, b):
    return lax.dot_general(a, b, (((1,), (1,)), ((), ())), preferred_element_type=F32)


def _ada_kernel(c_ref, w_ref, b_ref, o_ref):
    c = c_ref[...]
    a = c * (1.0 / (1.0 + jnp.exp(-c)))
    o_ref[0] = _dot(a.astype(BF16), w_ref[0].astype(BF16)) + b_ref[0]


def _ada_mods(cond, ada_w, ada_b):
    depth, d, n = ada_w.shape
    tn = n // 4
    out = pl.pallas_call(
        _ada_kernel,
        grid=(depth, n // tn),
        in_specs=[
            pl.BlockSpec((MOD_ROWS, d), lambda l, j: (0, 0)),
            pl.BlockSpec((1, d, tn), lambda l, j: (l, 0, j)),
            pl.BlockSpec((1, 1, tn), lambda l, j: (l, 0, j)),
        ],
        out_specs=pl.BlockSpec((1, MOD_ROWS, tn), lambda l, j: (l, 0, j)),
        out_shape=jax.ShapeDtypeStruct((depth, MOD_ROWS, n), F32),
        compiler_params=_cparams("parallel", "parallel"),
        name="ada_mods",
    )(cond, ada_w, ada_b.reshape(depth, 1, n))
    return out.reshape(depth, MOD_ROWS, MOD_CHUNKS, d)


class _Stream:
    def __init__(self, bsz, seq, ctx_len, d):
        self.bsz, self.seq, self.ctx_len, self.d = bsz, seq, ctx_len, d
        self.n_lat = bsz * seq
        self.n_tok = bsz * (seq + ctx_len)

    def tiles(self, tm, with_ctx):
        return (self.n_tok if with_ctx else self.n_lat) // tm

    def mod_row(self, tm):
        n_lat_tiles, per_seq, bsz = self.n_lat // tm, self.seq // tm, self.bsz
        return lambda t: jnp.where(t < n_lat_tiles, t // per_seq, bsz)

    def mod_spec(self, tm):
        row = self.mod_row(tm)
        return pl.BlockSpec((1, MOD_CHUNKS, self.d), lambda t, *_: (row(t), 0, 0))


def _tok_spec(tm, width):
    return pl.BlockSpec((tm, width), lambda t, *_: (t, 0))


def _full_spec(shape):
    zeros = (0,) * len(shape)
    return pl.BlockSpec(shape, lambda *_: zeros)


def _sgu_kernel(x_ref, mod_ref, g_ref, win_ref, vg_ref, ws_ref, bs_ref, wout_ref, o_ref):
    x = x_ref[...]
    mod = mod_ref[0]
    tm = x.shape[0]
    width = vg_ref.shape[1]
    gdim = width // SGU_GROUPS
    h = _modulate(x, g_ref[...], mod[0:1], mod[1:2]).astype(BF16)
    u = _gelu(_dot(h, win_ref[:, :width]))
    v = _gelu(_dot(h, win_ref[:, width:]))
    vn = (_rms(v) * vg_ref[...]).astype(BF16)
    rows = []
    for c in range(tm // SGU_CHUNK):
        cols = []
        for g in range(SGU_GROUPS):
            vv = vn[c * SGU_CHUNK:(c + 1) * SGU_CHUNK, g * gdim:(g + 1) * gdim]
            cols.append(_dot(ws_ref[g], vv) + bs_ref[:, g:g + 1])
        rows.append(jnp.concatenate(cols, axis=1))
    sv = jnp.concatenate(rows, axis=0)
    y = _dot((u * sv).astype(BF16), wout_ref[...])
    o_ref[...] = x + mod[2:3] * y


def _sgu_layer(st, xs, mods, gain, w_in, v_gain, w_s, b_s, w_out, with_ctx):
    tm, d = TOKEN_TILE, st.d
    width = v_gain.shape[0]
    return pl.pallas_call(
        _sgu_kernel,
        grid=(st.tiles(tm, with_ctx),),
        in_specs=[
            _tok_spec(tm, d), st.mod_spec(tm), _full_spec((1, d)),
            _full_spec((d, 2 * width)), _full_spec((1, width)),
            _full_spec((SGU_GROUPS, SGU_CHUNK, SGU_CHUNK)), _full_spec((SGU_CHUNK, SGU_GROUPS)),
            _full_spec((width, d)),
        ],
        out_specs=_tok_spec(tm, d),
        out_shape=jax.ShapeDtypeStruct((st.tiles(tm, with_ctx) * tm, d), F32),
        compiler_params=_cparams("parallel"),
        name="sgu_mixer",
    )(xs, mods, gain.reshape(1, d), w_in.astype(BF16), v_gain.reshape(1, width),
      w_s.astype(BF16), b_s.T, w_out.astype(BF16))


POOL_HALO = 8


def _pool_in_kernel(x_ref, mod_ref, g_ref, win_ref, z_ref):
    mod = mod_ref[0]
    h = _modulate(x_ref[...], g_ref[...], mod[0:1], mod[1:2]).astype(BF16)
    z_ref[...] = _dot(h, win_ref[...])


def _pool_out_kernel(z_ref, zp_ref, zn_ref, x_ref, mod_ref, wg_ref, sc_ref, wout_ref, o_ref,
                     *, n_lat_tiles, lat_tiles_per_seq, ctx_tiles_per_seq, seq, ctx_len):
    t = pl.program_id(0)
    tm = z_ref.shape[0]
    is_lat = t < n_lat_tiles
    per_seq = jnp.where(is_lat, lat_tiles_per_seq, ctx_tiles_per_seq)
    pos_tile = jnp.where(is_lat, t, t - n_lat_tiles) % per_seq
    length = jnp.where(is_lat, seq, ctx_len)
    has_prev = (pos_tile > 0).astype(F32)
    has_next = (pos_tile < per_seq - 1).astype(F32)
    z = z_ref[...]
    zext = jnp.concatenate([zp_ref[...] * has_prev, z, zn_ref[...] * has_next], axis=0)
    pos = (pos_tile * tm + lax.broadcasted_iota(jnp.int32, (tm, 1), 0))
    gdim = wg_ref.shape[1]
    n_ext = tm + 2 * POOL_HALO

    def shift_up(a, k):
        return a if k == 0 else pltpu.roll(a, n_ext - k, 0)

    outs = []
    for g, w in enumerate(POOL_WINDOWS):
        half = w // 2
        acc = zext[:, g * gdim:(g + 1) * gdim]
        span = 1
        while span < w:
            acc = acc + shift_up(acc, span)
            span *= 2
        win_sum = shift_up(acc, POOL_HALO - half)[:tm]
        cnt = (jnp.minimum(pos + half, length) - jnp.maximum(pos - half, 0)).astype(F32)
        pooled = win_sum / cnt - z[:, g * gdim:(g + 1) * gdim]
        outs.append(_dot(pooled.astype(BF16), wg_ref[g]))
    y = (jnp.concatenate(outs, axis=1) * sc_ref[...]).astype(BF16)
    o_ref[...] = x_ref[...] + mod_ref[0][2:3] * _dot(y, wout_ref[...])


def _pool_layer(st, xs, mods, gain, w_in, w_grp, scale, w_out, with_ctx):
    tm, d = TOKEN_TILE, st.d
    width = w_in.shape[1]
    n_tiles = st.tiles(tm, with_ctx)
    z = pl.pallas_call(
        _pool_in_kernel,
        grid=(n_tiles,),
        in_specs=[_tok_spec(tm, d), st.mod_spec(tm), _full_spec((1, d)), _full_spec((d, width))],
        out_specs=_tok_spec(tm, width),
        out_shape=jax.ShapeDtypeStruct((n_tiles * tm, width), F32),
        compiler_params=_cparams("parallel"),
        name="pool_in",
    )(xs, mods, gain.reshape(1, d), w_in.astype(BF16))
    halo_per_tile = tm // POOL_HALO
    last_halo = n_tiles * halo_per_tile - 1
    kern = functools.partial(
        _pool_out_kernel, n_lat_tiles=st.n_lat // tm, lat_tiles_per_seq=st.seq // tm,
        ctx_tiles_per_seq=st.ctx_len // tm, seq=st.seq, ctx_len=st.ctx_len)
    return pl.pallas_call(
        kern,
        grid=(n_tiles,),
        in_specs=[
            _tok_spec(tm, width),
            pl.BlockSpec((POOL_HALO, width), lambda t: (jnp.maximum(t * halo_per_tile - 1, 0), 0)),
            pl.BlockSpec((POOL_HALO, width), lambda t: (jnp.minimum((t + 1) * halo_per_tile, last_halo), 0)),
            _tok_spec(tm, d), st.mod_spec(tm),
            _full_spec(w_grp.shape), _full_spec((1, width)), _full_spec((width, d)),
        ],
        out_specs=_tok_spec(tm, d),
        out_shape=jax.ShapeDtypeStruct((st.tiles(tm, with_ctx) * tm, d), F32),
        compiler_params=_cparams("parallel"),
        name="pool_out",
    )(z, z, z, xs, mods, w_grp.astype(BF16), scale.reshape(1, width), w_out.astype(BF16))


def _rope_tables(seq, tm):
    pos = jnp.arange(seq)
    axis_dim = HEAD_DIM // 2
    freqs = ROPE_THETA ** (-jnp.arange(0, axis_dim, 2, dtype=F32) / axis_dim)
    ang = jnp.stack([pos // GRID_W, pos % GRID_W], axis=-1).astype(F32)[:, :, None] * freqs
    cos, sin = jnp.cos(ang), jnp.sin(ang)
    zero = jnp.zeros_like(sin)
    c = jnp.concatenate([cos, cos], axis=-1).reshape(seq, HEAD_DIM)
    s_lo = jnp.concatenate([-sin, zero], axis=-1).reshape(seq, HEAD_DIM)
    s_hi = jnp.concatenate([zero, sin], axis=-1).reshape(seq, HEAD_DIM)
    pad = jnp.zeros((tm, HEAD_DIM), F32)
    return (jnp.concatenate([c, pad + 1.0]), jnp.concatenate([s_lo, pad]), jnp.concatenate([s_hi, pad]))


def _qkv_kernel(x_ref, mod_ref, g_ref, w_ref, qg_ref, kg_ref, c_ref, slo_ref, shi_ref, q_ref, k_ref, v_ref):
    mod = mod_ref[0]
    h = _modulate(x_ref[...], g_ref[...], mod[0:1], mod[1:2]).astype(BF16)
    qkv = _dot(h, w_ref[...])
    nq, nk = q_ref.shape[1], k_ref.shape[1]
    cos, s_lo, s_hi = c_ref[...], slo_ref[...], shi_ref[...]
    quarter = HEAD_DIM // 4

    def norm_rope(t, gain):
        t = _rms(t) * gain
        return (t * cos + pltpu.roll(t, HEAD_DIM - quarter, 1) * s_lo + pltpu.roll(t, quarter, 1) * s_hi)

    for hh in range(nq // HEAD_DIM):
        sl = slice(hh * HEAD_DIM, (hh + 1) * HEAD_DIM)
        q_ref[:, sl] = norm_rope(qkv[:, sl], qg_ref[...]).astype(BF16)
    for hh in range(nk // HEAD_DIM):
        sl = slice(hh * HEAD_DIM, (hh + 1) * HEAD_DIM)
        k_ref[:, sl] = norm_rope(qkv[:, nq + hh * HEAD_DIM:nq + (hh + 1) * HEAD_DIM], kg_ref[...]).astype(BF16)
    v_ref[...] = qkv[:, nq + nk:].astype(BF16)


def _attn_kernel(q_ref, kl_ref, vl_ref, kc_ref, vc_ref, x_ref, mod_ref, wo_ref, o_ref, *, n_lat_tiles):
    n_heads = q_ref.shape[1] // HEAD_DIM
    q_per_kv = n_heads // N_KV_HEADS
    scale = HEAD_DIM ** -0.5

    def body(with_lat):
        def kv_of(hh):
            return slice((hh // q_per_kv) * HEAD_DIM, (hh // q_per_kv + 1) * HEAD_DIM)

        def scores(hh):
            q = q_ref[:, hh * HEAD_DIM:(hh + 1) * HEAD_DIM]
            s_c = _dot_nt(q, kc_ref[:, kv_of(hh)]) * scale
            s_l = _dot_nt(q, kl_ref[:, kv_of(hh)]) * scale if with_lat else None
            return s_c, s_l

        def attend(hh, s_c, s_l):
            m = jnp.max(s_c, axis=-1, keepdims=True)
            if with_lat:
                m = jnp.maximum(m, jnp.max(s_l, axis=-1, keepdims=True))
            p_c = jnp.exp(s_c - m)
            den = jnp.sum(p_c, axis=-1, keepdims=True)
            o = _dot(p_c.astype(BF16), vc_ref[:, kv_of(hh)])
            if with_lat:
                p_l = jnp.exp(s_l - m)
                den = den + jnp.sum(p_l, axis=-1, keepdims=True)
                o = o + _dot(p_l.astype(BF16), vl_ref[:, kv_of(hh)])
            return o / den

        outs = []
        pending = scores(0)
        for hh in range(n_heads):
            upcoming = scores(hh + 1) if hh + 1 < n_heads else None
            outs.append(attend(hh, *pending))
            pending = upcoming
        y = _dot(jnp.concatenate(outs, axis=1).astype(BF16), wo_ref[...])
        o_ref[...] = x_ref[...] + mod_ref[0][2:3] * y

    t = pl.program_id(0)
    pl.when(t < n_lat_tiles)(lambda: body(True))
    pl.when(t >= n_lat_tiles)(lambda: body(False))


def _attn_layer(st, xs, mods, gain, w_qkv, q_gain, k_gain, w_o, with_ctx):
    d = st.d
    tm = st.ctx_len
    assert st.seq % tm == 0
    n_all = st.n_tok // tm
    n_lat_tiles, per_seq = st.n_lat // tm, st.seq // tm
    nq = w_o.shape[0]
    nk = (w_qkv.shape[1] - nq) // 2
    cos, s_lo, s_hi = _rope_tables(st.seq, tm)
    rope_spec = pl.BlockSpec((tm, HEAD_DIM), lambda t: (jnp.where(t < n_lat_tiles, t % per_seq, per_seq), 0))
    q, k, v = pl.pallas_call(
        _qkv_kernel,
        grid=(n_all,),
        in_specs=[
            _tok_spec(tm, d), st.mod_spec(tm), _full_spec((1, d)), _full_spec(w_qkv.shape),
            _full_spec((1, HEAD_DIM)), _full_spec((1, HEAD_DIM)), rope_spec, rope_spec, rope_spec,
        ],
        out_specs=[_tok_spec(tm, nq), _tok_spec(tm, nk), _tok_spec(tm, nk)],
        out_shape=[jax.ShapeDtypeStruct((st.n_tok, nq), BF16), jax.ShapeDtypeStruct((st.n_tok, nk), BF16),
                   jax.ShapeDtypeStruct((st.n_tok, nk), BF16)],
        compiler_params=_cparams("parallel"),
        name="attn_qkv",
    )(xs, mods, gain.reshape(1, d), w_qkv.astype(BF16), q_gain.reshape(1, HEAD_DIM),
      k_gain.reshape(1, HEAD_DIM), cos, s_lo, s_hi)

    bsz = st.bsz
    batch_of = lambda t: jnp.where(t < n_lat_tiles, t // per_seq, t - n_lat_tiles)
    lat_spec = pl.BlockSpec((st.seq, nk), lambda t: (jnp.minimum(t // per_seq, bsz - 1), 0))
    ctx_spec = pl.BlockSpec((tm, nk), lambda t: (n_lat_tiles + batch_of(t), 0))
    return pl.pallas_call(
        functools.partial(_attn_kernel, n_lat_tiles=n_lat_tiles),
        grid=(st.tiles(tm, with_ctx),),
        in_specs=[_tok_spec(tm, nq), lat_spec, lat_spec, ctx_spec, ctx_spec,
                  _tok_spec(tm, d), st.mod_spec(tm), _full_spec((nq, d))],
        out_specs=_tok_spec(tm, d),
        out_shape=jax.ShapeDtypeStruct((st.tiles(tm, with_ctx) * tm, d), F32),
        compiler_params=_cparams("parallel"),
        name="attn_core",
    )(q, k, v, k, v, xs, mods, w_o.astype(BF16))


_INT_MIN = -2 ** 31


def _sortable(x):
    b = lax.bitcast_convert_type(x, jnp.int32)
    return b ^ ((b >> 31) & 0x7FFFFFFF)


def _unsortable(k):
    return lax.bitcast_convert_type(k ^ ((k >> 31) & 0x7FFFFFFF), F32)


def _extract16(k, iota, exact):
    n = k.shape[0]
    tops = []
    for a in range(PEER_TOPK):
        m = jnp.max(k, axis=0, keepdims=True)
        hit = k == m
        if exact:
            first = jnp.min(jnp.where(hit, iota, n), axis=0, keepdims=True)
            hit = iota == first
        k = jnp.where(hit, _INT_MIN + a, k)
        tops.append(m)
    return k, tops


def _rows16(tops, iota16):
    out = jnp.zeros(iota16.shape, tops[0].dtype)
    for a, t in enumerate(tops):
        out = jnp.where(iota16 == a, t, out)
    return out


_CAND_HALF = PEER_TOPK // 2


def _peer_select(s1, s2, exact):
    n, width = s1.shape
    topk = float(PEER_TOPK)
    iota = lax.broadcasted_iota(jnp.int32, (n, width), 0)
    iota16 = iota[0:PEER_TOPK]
    mark_end = _INT_MIN + PEER_TOPK
    k1, top1 = _extract16(_sortable(s1), iota, exact)
    k2, top2 = _extract16(_sortable(s2), iota, exact)
    v1 = _unsortable(_rows16(top1, iota16))
    v2 = _unsortable(_rows16(top2, iota16))
    f1 = [_unsortable(t) for t in top1[:_CAND_HALF]]
    f2_0 = _unsortable(top2[0])
    pieces = [v2 + f1[0]]
    pieces += [v2[0:_CAND_HALF] + f1[a] for a in range(1, _CAND_HALF)]
    pieces.append(v1[_CAND_HALF:] + f2_0)
    cand = jnp.concatenate(pieces, axis=0)
    nc = cand.shape[0]
    kc, _ = _extract16(_sortable(cand), iota[0:nc], exact)
    picked = kc < mark_end
    if not exact:
        surplus = jnp.sum(jnp.where(picked, 1.0, 0.0), axis=0, keepdims=True) - topk
        picked = kc < mark_end - surplus.astype(jnp.int32)
    sel = jnp.where(picked, 1.0, 0.0)
    z = jnp.sum(jnp.where(picked, jnp.exp(cand - (f1[0] + f2_0)), 0.0), axis=0, keepdims=True)
    cnt = [jnp.sum(sel[0:PEER_TOPK], axis=0, keepdims=True)]
    for a in range(1, _CAND_HALF):
        row = PEER_TOPK + (a - 1) * _CAND_HALF
        cnt.append(jnp.sum(sel[row:row + _CAND_HALF], axis=0, keepdims=True))
    tail = sel[nc - _CAND_HALF:]
    for a in range(_CAND_HALF, PEER_TOPK):
        pick = iota[0:_CAND_HALF] == a - _CAND_HALF
        cnt.append(jnp.sum(jnp.where(pick, tail, 0.0), axis=0, keepdims=True))
    cut = jnp.zeros(s1.shape, F32)
    for a in range(PEER_TOPK):
        cut = jnp.where(k1 == _INT_MIN + a, cnt[a], cut)
    in1, in2 = k1 < mark_end, k2 < mark_end
    r2 = jnp.where(in2, (k2 - _INT_MIN).astype(F32), topk)
    e1 = jnp.exp(s1 - f1[0])
    e2 = jnp.exp(s2 - f2_0) / z
    n1 = jnp.sum(jnp.where(in1, 1.0, 0.0), axis=0, keepdims=True)
    n2 = jnp.sum(jnp.where(in2, 1.0, 0.0), axis=0, keepdims=True)
    nsel = jnp.sum(sel, axis=0, keepdims=True)
    bad = jnp.where((n1 != topk) | (n2 != topk) | (nsel != topk), 1.0, 0.0)
    return cut, e1, r2, e2, bad


def _pack_rows(x):
    return pltpu.bitcast(x.astype(BF16), jnp.uint32)


def _unpack_rows(x):
    return pltpu.bitcast(x, BF16)


def _pack_twice(x):
    u = lax.bitcast_convert_type(x.astype(BF16).astype(F32), jnp.uint32)
    return u | (u >> 16)


PACK_BLOCK = 1024


def _pack_kernel(w_ref, o_ref, *, transpose):
    if transpose:
        o_ref[0, 0] = _pack_rows(w_ref[0].T)
    else:
        o_ref[0] = _pack_rows(w_ref[0])


def _pack_weights(w, transpose):
    layers, r, c = w.shape
    pb = PACK_BLOCK
    if transpose:
        out_spec = pl.BlockSpec((1, 1, pb // 2, pb), lambda l, i, j: (l, i, j, 0))
        out_shape = (layers, r // pb, c // 2, pb)
    else:
        out_spec = pl.BlockSpec((1, pb // 2, pb), lambda l, i, j: (l, i, j))
        out_shape = (layers, r // 2, c)
    return pl.pallas_call(
        functools.partial(_pack_kernel, transpose=transpose),
        grid=(layers, r // pb, c // pb),
        in_specs=[pl.BlockSpec((1, pb, pb), lambda l, i, j: (l, i, j))],
        out_specs=out_spec,
        out_shape=jax.ShapeDtypeStruct(out_shape, jnp.uint32),
        compiler_params=_cparams("parallel", "parallel", "parallel"),
        name="pack_t" if transpose else "pack",
    )(w)


def _peer_select_kernel(x_ref, mod_ref, g_ref, wq_ref, k1_ref, k2_ref,
                        h_ref, cut_ref, e1_ref, r2_ref, e2_ref, s1_scr, s2_scr):
    mod = mod_ref[0]
    f_t = _modulate(x_ref[...], g_ref[...], mod[3:4], mod[4:5]).T.astype(BF16)
    h_ref[...] = _pack_rows(f_t)
    tm = f_t.shape[1]
    n_heads, n_keys, half = k1_ref.shape
    q_t = _dot(_unpack_rows(wq_ref[...]), f_t)
    for hh in range(n_heads):
        base = hh * 2 * half
        s1_scr[hh] = _dot(k1_ref[hh], q_t[base:base + half].astype(BF16))
        s2_scr[hh] = _dot(k2_ref[hh], q_t[base + half:base + 2 * half].astype(BF16))
    n_sub = tm // LANES

    def step(idx, carry):
        hh = idx // (n_sub // SELECT_UNROLL)
        base = (idx % (n_sub // SELECT_UNROLL)) * SELECT_UNROLL
        bad_any = jnp.zeros((1, LANES), F32)
        work = []
        for sub in range(SELECT_UNROLL):
            lanes = pl.ds(pl.multiple_of((base + sub) * LANES, LANES), LANES)
            s1, s2 = s1_scr[hh, :, lanes], s2_scr[hh, :, lanes]
            *fast, bad = _peer_select(s1, s2, exact=False)
            work.append((lanes, s1, s2, fast))
            bad_any = jnp.maximum(bad_any, bad)

        def store(lanes, cut, e1, r2, e2):
            cut_ref[hh, :, lanes] = _pack_twice(cut)
            e1_ref[hh, :, lanes] = _pack_twice(e1)
            r2_ref[hh, :, lanes] = _pack_rows(r2)
            e2_ref[hh, :, lanes] = _pack_rows(e2)

        for lanes, _, _, fast in work:
            store(lanes, *fast)

        @pl.when(jnp.max(bad_any) > 0.0)
        def _():
            for lanes, s1, s2, _ in work:
                store(lanes, *_peer_select(s1, s2, exact=True)[:4])

        return carry

    lax.fori_loop(0, n_heads * n_sub // SELECT_UNROLL, step, 0)


def _peer_main_kernel(h_ref, u_ref, vt_ref, cut_ref, e1_ref, r2_ref, e2_ref, x_ref, mod_ref, fg_ref,
                      o_ref, acc, act0, act1, w0, w1, *, final_norm, n_chunks):
    g = pl.program_id(0)
    ec, tm = act0.shape
    n_heads, n_keys = r2_ref.shape[0], 2 * r2_ref.shape[1]
    rows_per_chunk = ec // n_keys
    half_keys = n_keys // 2
    packed_half = half_keys // 2

    @pl.when(g == 0)
    def _():
        acc[...] = jnp.zeros_like(acc)
        act1[...] = jnp.zeros_like(act1)
        w0[...] = jnp.zeros_like(w0)

    def gate_chunk(cut_ref, e1_ref, act_ref, w_ref, token_blocks):
        zero = jnp.zeros((half_keys, LANES), BF16)
        for part in range(n_keys // half_keys):
            prow = slice(part * packed_half, (part + 1) * packed_half)
            for tl in token_blocks:
                lanes = slice(tl * LANES, (tl + 1) * LANES)
                gates = [zero] * rows_per_chunk
                for hh in range(n_heads):
                    r2 = _unpack_rows(r2_ref[hh, prow, lanes])
                    e2 = _unpack_rows(e2_ref[hh, prow, lanes])
                    for il in range(rows_per_chunk):
                        cut = _unpack_rows(jnp.broadcast_to(cut_ref[hh, il:il + 1, lanes], (packed_half, LANES)))
                        e1 = _unpack_rows(jnp.broadcast_to(e1_ref[hh, il:il + 1, lanes], (packed_half, LANES)))
                        gates[il] = gates[il] + jnp.where(r2 < cut, e2, zero) * e1
                for il in range(rows_per_chunk):
                    rows = slice(il * n_keys + part * half_keys, il * n_keys + (part + 1) * half_keys)
                    w_ref[rows, lanes] = _gelu(act_ref[rows, lanes]).astype(BF16) * gates[il]

    def stage(act_new, act_old, w_new, w_old):
        n_blocks = tm // LANES
        gate_chunk(cut_ref, e1_ref, act_old, w_new, range(0, n_blocks // 2))
        act_new[...] = _dot(_unpack_rows(u_ref[...]), _unpack_rows(h_ref[...]))
        gate_chunk(cut_ref, e1_ref, act_old, w_new, range(n_blocks // 2, n_blocks))
        acc[...] += _dot(_unpack_rows(vt_ref[0]), w_old[...])

    pl.when(g % 2 == 0)(lambda: stage(act0, act1, w1, w0))
    pl.when(g % 2 == 1)(lambda: stage(act1, act0, w0, w1))

    @pl.when((g >= 2) & ((g - 2) % n_chunks == n_chunks - 1))
    def _():
        y = x_ref[...] + mod_ref[0][5:6] * acc[...].T
        if final_norm:
            y = _rms(y) * fg_ref[...]
        o_ref[...] = y
        acc[...] = jnp.zeros_like(acc)


def _peer_layer(st, xs, mods, gain, wq_packed, k1, k2, u_packed, vt_packed, final_gain, with_ctx, final_norm):
    tm, d = PEER_TILE, st.d
    n_heads, n_keys, half = k1.shape
    n_exp = 2 * u_packed.shape[0]
    ec = PEER_EXPERT_CHUNK
    n_tiles = st.tiles(tm, with_ctx)
    n_rows = n_tiles * tm
    u32 = jnp.uint32
    h_spec = pl.BlockSpec((d // 2, tm), lambda t, *_: (0, t))
    key_spec = pl.BlockSpec((n_heads, n_keys, tm), lambda t, *_: (0, 0, t))
    key_shape = jax.ShapeDtypeStruct((n_heads, n_keys, n_rows), u32)
    pair_spec = pl.BlockSpec((n_heads, n_keys // 2, tm), lambda t, *_: (0, 0, t))
    pair_shape = jax.ShapeDtypeStruct((n_heads, n_keys // 2, n_rows), u32)
    h, cut, e1, r2, e2 = pl.pallas_call(
        _peer_select_kernel,
        grid=(n_tiles,),
        in_specs=[_tok_spec(tm, d), st.mod_spec(tm), _full_spec((1, d)), _full_spec(wq_packed.shape),
                  _full_spec(k1.shape), _full_spec(k2.shape)],
        out_specs=[h_spec, key_spec, key_spec, pair_spec, pair_spec],
        out_shape=[jax.ShapeDtypeStruct((d // 2, n_rows), u32), key_shape, key_shape, pair_shape, pair_shape],
        scratch_shapes=[pltpu.VMEM((n_heads, n_keys, tm), F32), pltpu.VMEM((n_heads, n_keys, tm), F32)],
        compiler_params=_cparams("parallel"),
        name="peer_select",
    )(xs, mods, gain.reshape(1, d), wq_packed, k1.astype(BF16), k2.astype(BF16))

    n_chunks = n_exp // ec
    n_items = n_tiles * n_chunks
    rows_per_chunk = ec // n_keys
    mod_row = st.mod_row(tm)

    def item(lag):
        def tile(g):
            return jnp.clip(g - lag, 0, n_items - 1) // n_chunks

        def chunk(g):
            return jnp.clip(g - lag, 0, n_items - 1) % n_chunks

        return tile, chunk

    (tile0, chunk0), (tile1, chunk1), (tile2, chunk2) = item(0), item(1), item(2)
    sub1_spec = pl.BlockSpec((n_heads, rows_per_chunk, tm), lambda g: (0, chunk1(g), tile1(g)))
    sub2_spec = pl.BlockSpec((n_heads, n_keys // 2, tm), lambda g: (0, 0, tile1(g)))
    return pl.pallas_call(
        functools.partial(_peer_main_kernel, final_norm=final_norm, n_chunks=n_chunks),
        grid=(n_items + 2,),
        in_specs=[
            pl.BlockSpec((d // 2, tm), lambda g: (0, tile0(g))),
            pl.BlockSpec((ec // 2, d), lambda g: (chunk0(g), 0)),
            pl.BlockSpec((1, d // 2, ec), lambda g: (chunk2(g), 0, 0)),
            sub1_spec, sub1_spec, sub2_spec, sub2_spec,
            pl.BlockSpec((tm, d), lambda g: (tile2(g), 0)),
            pl.BlockSpec((1, MOD_CHUNKS, d), lambda g: (mod_row(tile2(g)), 0, 0)),
            _full_spec((1, d)),
        ],
        out_specs=pl.BlockSpec((tm, d), lambda g: (tile2(g), 0)),
        out_shape=jax.ShapeDtypeStruct((n_rows, d), F32),
        scratch_shapes=[pltpu.VMEM((d, tm), F32),
                        pltpu.VMEM((ec, tm), F32), pltpu.VMEM((ec, tm), F32),
                        pltpu.VMEM((ec, tm), BF16), pltpu.VMEM((ec, tm), BF16)],
        compiler_params=_cparams("arbitrary"),
        name="peer_main",
    )(h, u_packed, vt_packed, cut, e1, r2, e2, xs, mods, final_gain.reshape(1, d))


def _peer_fused_kernel(xn_ref, modn_ref, g_ref, wq_ref, k1_ref, k2_ref, u_ref, vt_ref, x_ref, mod_ref, fg_ref,
                       o_ref, acc, act0, act1, w0, w1, ht_buf, q_buf, cut_buf, e1_buf, r2_buf, e2_buf,
                       *, final_norm, n_chunks, n_items):
    g = pl.program_id(0)
    ec, tm = act0.shape
    n_heads, n_keys, half = k1_ref.shape
    rows_per_chunk = ec // n_keys
    half_keys = n_keys // 2
    packed_half = half_keys // 2
    subs_per_head = tm // LANES // SELECT_UNROLL
    assert n_heads * subs_per_head == n_chunks

    last = n_items - 1
    sel_item = jnp.clip(g - 1, 0, last)
    sel_slot, sel_iter = (sel_item // n_chunks) % 2, sel_item % n_chunks
    slot_a = (jnp.clip(g - n_chunks, 0, last) // n_chunks) % 2
    item_b = jnp.clip(g - n_chunks - 1, 0, last)
    slot_b, chunk_b = (item_b // n_chunks) % 2, item_b % n_chunks

    @pl.when(g == 0)
    def _():
        acc[...] = jnp.zeros_like(acc)
        act1[...] = jnp.zeros_like(act1)
        w0[...] = jnp.zeros_like(w0)

    @pl.when((g % n_chunks == 0) & (g < n_items))
    def _():
        slot = (g // n_chunks) % 2
        mod = modn_ref[0]
        f_t = _modulate(xn_ref[...], g_ref[...], mod[3:4], mod[4:5]).T.astype(BF16)
        ht_buf[slot] = _pack_rows(f_t)
        q_buf[slot] = _dot(_unpack_rows(wq_ref[...]), f_t).astype(BF16)

    def gate_chunk(act_ref, w_ref):
        zero = jnp.zeros((half_keys, LANES), BF16)
        for part in range(n_keys // half_keys):
            prow = slice(part * packed_half, (part + 1) * packed_half)
            for tl in range(tm // LANES):
                lanes = slice(tl * LANES, (tl + 1) * LANES)
                gates = [zero] * rows_per_chunk
                for hh in range(n_heads):
                    r2 = _unpack_rows(r2_buf[slot_b, hh, prow, lanes])
                    e2 = _unpack_rows(e2_buf[slot_b, hh, prow, lanes])
                    for il in range(rows_per_chunk):
                        cut = cut_buf[slot_b, chunk_b, hh, il:il + 1, lanes]
                        e1 = e1_buf[slot_b, chunk_b, hh, il:il + 1, lanes]
                        cut = _unpack_rows(jnp.broadcast_to(cut, (packed_half, LANES)))
                        e1 = _unpack_rows(jnp.broadcast_to(e1, (packed_half, LANES)))
                        gates[il] = gates[il] + jnp.where(r2 < cut, e2, zero) * e1
                for il in range(rows_per_chunk):
                    rows = slice(il * n_keys + part * half_keys, il * n_keys + (part + 1) * half_keys)
                    w_ref[rows, lanes] = _gelu(act_ref[rows, lanes]).astype(BF16) * gates[il]

    def select_iteration(dense=None):
        hh = sel_iter // subs_per_head
        width = SELECT_UNROLL * LANES
        lanes = pl.ds(pl.multiple_of((sel_iter % subs_per_head) * width, width), width)
        qrow = pl.multiple_of(hh * 2 * half, 2 * half)
        s1_all = _dot(k1_ref[hh], q_buf[sel_slot, pl.ds(qrow, half), lanes])
        s2_all = _dot(k2_ref[hh], q_buf[sel_slot, pl.ds(qrow + half, half), lanes])
        groups = n_keys // rows_per_chunk

        def store(results):
            cut, e1, r2, e2 = (jnp.concatenate(parts, axis=1) for parts in zip(*results))
            cut_buf[sel_slot, :, hh, :, lanes] = _pack_twice(cut).reshape(groups, rows_per_chunk, width)
            e1_buf[sel_slot, :, hh, :, lanes] = _pack_twice(e1).reshape(groups, rows_per_chunk, width)
            r2_buf[sel_slot, hh, :, lanes] = _pack_rows(r2)
            e2_buf[sel_slot, hh, :, lanes] = _pack_rows(e2)

        scores = [(s1_all[:, sub * LANES:(sub + 1) * LANES], s2_all[:, sub * LANES:(sub + 1) * LANES])
                  for sub in range(SELECT_UNROLL)]
        fast = [_peer_select(s1, s2, exact=False) for s1, s2 in scores]
        store([f[:4] for f in fast])
        bad = functools.reduce(jnp.maximum, [f[4] for f in fast])
        if dense is not None:
            dense()

        @pl.when(jnp.max(bad) > 0.0)
        def _():
            store([_peer_select(s1, s2, exact=True)[:4] for s1, s2 in scores])

    def dense_stage(act_new, act_old, w_new, w_old):
        act_new[...] = _dot(_unpack_rows(u_ref[...]), _unpack_rows(ht_buf[slot_a]))
        gate_chunk(act_old, w_new)
        acc[...] += _dot(_unpack_rows(vt_ref[...]), w_old[...])

    @pl.when(g < n_chunks)
    def _():
        select_iteration()

    @pl.when((g >= n_chunks) & (g % 2 == 0))
    def _():
        select_iteration(lambda: dense_stage(act0, act1, w1, w0))

    @pl.when((g >= n_chunks) & (g % 2 == 1))
    def _():
        select_iteration(lambda: dense_stage(act1, act0, w0, w1))

    done = g - n_chunks - 2
    @pl.when((done >= 0) & (done % n_chunks == n_chunks - 1))
    def _():
        y = x_ref[...] + mod_ref[0][5:6] * acc[...].T
        if final_norm:
            y = _rms(y) * fg_ref[...]
        o_ref[...] = y
        acc[...] = jnp.zeros_like(acc)


def _peer_fused_layer(st, xs, mods, gain, wq_packed, k1, k2, u_packed, vt_packed, final_gain, with_ctx,
                      final_norm):
    tm, d = PEER_TILE, st.d
    n_heads, n_keys, half = k1.shape
    n_exp = 2 * u_packed.shape[0]
    ec = PEER_EXPERT_CHUNK
    n_tiles = st.tiles(tm, with_ctx)
    n_chunks = n_exp // ec
    n_items = n_tiles * n_chunks
    rows_per_chunk = ec // n_keys
    mod_row = st.mod_row(tm)
    u32 = jnp.uint32

    def lagged(lag):
        def tile(g):
            return jnp.clip(g - lag, 0, n_items - 1) // n_chunks

        def chunk(g):
            return jnp.clip(g - lag, 0, n_items - 1) % n_chunks

        return tile, chunk

    sel_tile, _ = lagged(0)
    _, chunk_a = lagged(n_chunks)
    tile_c, chunk_c = lagged(n_chunks + 2)
    return pl.pallas_call(
        functools.partial(_peer_fused_kernel, final_norm=final_norm, n_chunks=n_chunks, n_items=n_items),
        grid=(n_items + n_chunks + 2,),
        in_specs=[
            pl.BlockSpec((tm, d), lambda g: (sel_tile(g), 0)),
            pl.BlockSpec((1, MOD_CHUNKS, d), lambda g: (mod_row(sel_tile(g)), 0, 0)),
            _full_spec((1, d)), _full_spec(wq_packed.shape), _full_spec(k1.shape), _full_spec(k2.shape),
            pl.BlockSpec((ec // 2, d), lambda g: (chunk_a(g), 0)),
            pl.BlockSpec((d // 2, ec), lambda g: (0, chunk_c(g))),
            pl.BlockSpec((tm, d), lambda g: (tile_c(g), 0)),
            pl.BlockSpec((1, MOD_CHUNKS, d), lambda g: (mod_row(tile_c(g)), 0, 0)),
            _full_spec((1, d)),
        ],
        out_specs=pl.BlockSpec((tm, d), lambda g: (tile_c(g), 0)),
        out_shape=jax.ShapeDtypeStruct((n_tiles * tm, d), F32),
        scratch_shapes=[
            pltpu.VMEM((d, tm), F32),
            pltpu.VMEM((ec, tm), F32), pltpu.VMEM((ec, tm), F32),
            pltpu.VMEM((ec, tm), BF16), pltpu.VMEM((ec, tm), BF16),
            pltpu.VMEM((2, d // 2, tm), u32),
            pltpu.VMEM((2, 2 * half * n_heads, tm), BF16),
            pltpu.VMEM((2, n_keys // rows_per_chunk, n_heads, rows_per_chunk, tm), u32),
            pltpu.VMEM((2, n_keys // rows_per_chunk, n_heads, rows_per_chunk, tm), u32),
            pltpu.VMEM((2, n_heads, n_keys // 2, tm), u32),
            pltpu.VMEM((2, n_heads, n_keys // 2, tm), u32),
        ],
        compiler_params=pltpu.CompilerParams(dimension_semantics=("arbitrary",),
                                             vmem_limit_bytes=FUSED_VMEM_LIMIT),
        name="peer_fused",
    )(xs, mods, gain.reshape(1, d), wq_packed, k1.astype(BF16), k2.astype(BF16), u_packed, vt_packed,
      xs, mods, final_gain.reshape(1, d))


def kernel(x, c, ctx, c_ctx, ada_w, ada_b, norm1_g, norm2_g, sgu_w_in, sgu_v_gain, sgu_w_s, sgu_b_s, sgu_w_out, pool_w_in, pool_w_grp, pool_scale, pool_w_out, attn_w_qkv, attn_q_gain, attn_k_gain, attn_w_o, peer_w_q, peer_k1, peer_k2, peer_u, peer_v, final_gain):
    bsz, seq, d = x.shape
    ctx_len = ctx.shape[1]
    depth = ada_w.shape[0]
    st = _Stream(bsz, seq, ctx_len, d)
    assert bsz + 1 <= MOD_ROWS and seq % PEER_TILE == 0 and (bsz * ctx_len) % PEER_TILE == 0
    assert seq % TOKEN_TILE == 0 and ctx_len % TOKEN_TILE == 0

    cond = jnp.concatenate([c, c_ctx[None, :], jnp.zeros((MOD_ROWS - bsz - 1, d), F32)], axis=0)
    mods = _ada_mods(cond, ada_w, ada_b)
    assert peer_w_q.shape[1] == PACK_BLOCK and PEER_EXPERT_CHUNK == PACK_BLOCK
    wq_packed = _pack_weights(peer_w_q, transpose=True)[:, 0]
    u_packed = _pack_weights(peer_u, transpose=False)
    vt_packed = _pack_weights(peer_v, transpose=True)
    xs = jnp.concatenate([x.reshape(bsz * seq, d), ctx.reshape(bsz * ctx_len, d)], axis=0)

    for i in range(depth):
        kind, j = i % N_MIXERS, i // N_MIXERS
        with_ctx = i < depth - 1
        if kind == 0:
            xs = _sgu_layer(st, xs, mods[i], norm1_g[i], sgu_w_in[j], sgu_v_gain[j], sgu_w_s[j], sgu_b_s[j],
                            sgu_w_out[j], with_ctx)
        elif kind == 1:
            xs = _pool_layer(st, xs, mods[i], norm1_g[i], pool_w_in[j], pool_w_grp[j], pool_scale[j],
                             pool_w_out[j], with_ctx)
        else:
            xs = _attn_layer(st, xs, mods[i], norm1_g[i], attn_w_qkv[j], attn_q_gain[j], attn_k_gain[j],
                             attn_w_o[j], with_ctx)
        xs = _peer_layer(st, xs, mods[i], norm2_g[i], wq_packed[i], peer_k1[i], peer_k2[i], u_packed[i],
                         vt_packed[i], final_gain, with_ctx, final_norm=(i == depth - 1))
    return xs[:bsz * seq].reshape(bsz, seq, d)
```

```python
import functools
import math

import jax
import jax.numpy as jnp
from jax import lax
from jax.experimental import pallas as pl
from jax.experimental.pallas import tpu as pltpu

F32 = jnp.float32
BF16 = jnp.bfloat16

NORM_EPS = 1e-6
MOD_CHUNKS = 6
GRID_W = 64
SGU_CHUNK = 128
SGU_GROUPS = 8
POOL_WINDOWS = (2, 4, 8, 16)
HEAD_DIM = 128
N_KV_HEADS = 2
ROPE_THETA = 10000.0
PEER_HEADS = 8
PEER_KEYS = 128
PEER_TOPK = 16
N_MIXERS = 3

LANES = 128
MOD_ROWS = 16
VMEM_LIMIT = 56 * 1024 * 1024
FUSED_VMEM_LIMIT = 60 * 1024 * 1024

TOKEN_TILE = 256
PEER_TILE = 512
PEER_EXPERT_CHUNK = 1024
GATE_ROWS = 8
SELECT_UNROLL = 2

def _cparams(*sem):
    return pltpu.CompilerParams(dimension_semantics=sem, vmem_limit_bytes=VMEM_LIMIT)


def _rms(x):
    return x * lax.rsqrt(jnp.mean(x * x, axis=-1, keepdims=True) + NORM_EPS)


def _modulate(x, gain, shift, scale):
    return _rms(x) * gain * (1.0 + scale) + shift


def _gelu(x):
    return 0.5 * x * (1.0 + lax.erf(x * (1.0 / math.sqrt(2.0))))


def _dot(a, b):
    return jnp.dot(a, b, preferred_element_type=F32)


def _dot_nt(a, b):
    return lax.dot_general(a, b, (((1,), (1,)), ((), ())), preferred_element_type=F32)


def _ada_kernel(c_ref, w_ref, b_ref, o_ref):
    c = c_ref[...]
    a = c * (1.0 / (1.0 + jnp.exp(-c)))
    o_ref[0] = _dot(a.astype(BF16), w_ref[0].astype(BF16)) + b_ref[0]


def _ada_mods(cond, ada_w, ada_b):
    depth, d, n = ada_w.shape
    tn = n // 4
    out = pl.pallas_call(
        _ada_kernel,
        grid=(depth, n // tn),
        in_specs=[
            pl.BlockSpec((MOD_ROWS, d), lambda l, j: (0, 0)),
            pl.BlockSpec((1, d, tn), lambda l, j: (l, 0, j)),
            pl.BlockSpec((1, 1, tn), lambda l, j: (l, 0, j)),
        ],
        out_specs=pl.BlockSpec((1, MOD_ROWS, tn), lambda l, j: (l, 0, j)),
        out_shape=jax.ShapeDtypeStruct((depth, MOD_ROWS, n), F32),
        compiler_params=_cparams("parallel", "parallel"),
        name="ada_mods",
    )(cond, ada_w, ada_b.reshape(depth, 1, n))
    return out.reshape(depth, MOD_ROWS, MOD_CHUNKS, d)


class _Stream:
    def __init__(self, bsz, seq, ctx_len, d):
        self.bsz, self.seq, self.ctx_len, self.d = bsz, seq, ctx_len, d
        self.n_lat = bsz * seq
        self.n_tok = bsz * (seq + ctx_len)

    def tiles(self, tm, with_ctx):
        return (self.n_tok if with_ctx else self.n_lat) // tm

    def mod_row(self, tm):
        n_lat_tiles, per_seq, bsz = self.n_lat // tm, self.seq // tm, self.bsz
        return lambda t: jnp.where(t < n_lat_tiles, t // per_seq, bsz)

    def mod_spec(self, tm):
        row = self.mod_row(tm)
        return pl.BlockSpec((1, MOD_CHUNKS, self.d), lambda t, *_: (row(t), 0, 0))


def _tok_spec(tm, width):
    return pl.BlockSpec((tm, width), lambda t, *_: (t, 0))


def _full_spec(shape):
    zeros = (0,) * len(shape)
    return pl.BlockSpec(shape, lambda *_: zeros)


def _sgu_kernel(x_ref, mod_ref, g_ref, win_ref, vg_ref, ws_ref, bs_ref, wout_ref, o_ref):
    x = x_ref[...]
    mod = mod_ref[0]
    tm = x.shape[0]
    width = vg_ref.shape[1]
    gdim = width // SGU_GROUPS
    h = _modulate(x, g_ref[...], mod[0:1], mod[1:2]).astype(BF16)
    u = _gelu(_dot(h, win_ref[:, :width]))
    v = _gelu(_dot(h, win_ref[:, width:]))
    vn = (_rms(v) * vg_ref[...]).astype(BF16)
    rows = []
    for c in range(tm // SGU_CHUNK):
        cols = []
        for g in range(SGU_GROUPS):
            vv = vn[c * SGU_CHUNK:(c + 1) * SGU_CHUNK, g * gdim:(g + 1) * gdim]
            cols.append(_dot(ws_ref[g], vv) + bs_ref[:, g:g + 1])
        rows.append(jnp.concatenate(cols, axis=1))
    sv = jnp.concatenate(rows, axis=0)
    y = _dot((u * sv).astype(BF16), wout_ref[...])
    o_ref[...] = x + mod[2:3] * y


def _sgu_layer(st, xs, mods, gain, w_in, v_gain, w_s, b_s, w_out, with_ctx):
    tm, d = TOKEN_TILE, st.d
    width = v_gain.shape[0]
    return pl.pallas_call(
        _sgu_kernel,
        grid=(st.tiles(tm, with_ctx),),
        in_specs=[
            _tok_spec(tm, d), st.mod_spec(tm), _full_spec((1, d)),
            _full_spec((d, 2 * width)), _full_spec((1, width)),
            _full_spec((SGU_GROUPS, SGU_CHUNK, SGU_CHUNK)), _full_spec((SGU_CHUNK, SGU_GROUPS)),
            _full_spec((width, d)),
        ],
        out_specs=_tok_spec(tm, d),
        out_shape=jax.ShapeDtypeStruct((st.tiles(tm, with_ctx) * tm, d), F32),
        compiler_params=_cparams("parallel"),
        name="sgu_mixer",
    )(xs, mods, gain.reshape(1, d), w_in.astype(BF16), v_gain.reshape(1, width),
      w_s.astype(BF16), b_s.T, w_out.astype(BF16))


POOL_HALO = 8


def _pool_in_kernel(x_ref, mod_ref, g_ref, win_ref, z_ref):
    mod = mod_ref[0]
    h = _modulate(x_ref[...], g_ref[...], mod[0:1], mod[1:2]).astype(BF16)
    z_ref[...] = _dot(h, win_ref[...])


def _pool_out_kernel(z_ref, zp_ref, zn_ref, x_ref, mod_ref, wg_ref, sc_ref, wout_ref, o_ref,
                     *, n_lat_tiles, lat_tiles_per_seq, ctx_tiles_per_seq, seq, ctx_len):
    t = pl.program_id(0)
    tm = z_ref.shape[0]
    is_lat = t < n_lat_tiles
    per_seq = jnp.where(is_lat, lat_tiles_per_seq, ctx_tiles_per_seq)
    pos_tile = jnp.where(is_lat, t, t - n_lat_tiles) % per_seq
    length = jnp.where(is_lat, seq, ctx_len)
    has_prev = (pos_tile > 0).astype(F32)
    has_next = (pos_tile < per_seq - 1).astype(F32)
    z = z_ref[...]
    zext = jnp.concatenate([zp_ref[...] * has_prev, z, zn_ref[...] * has_next], axis=0)
    pos = (pos_tile * tm + lax.broadcasted_iota(jnp.int32, (tm, 1), 0))
    gdim = wg_ref.shape[1]
    n_ext = tm + 2 * POOL_HALO

    def shift_up(a, k):
        return a if k == 0 else pltpu.roll(a, n_ext - k, 0)

    outs = []
    for g, w in enumerate(POOL_WINDOWS):
        half = w // 2
        acc = zext[:, g * gdim:(g + 1) * gdim]
        span = 1
        while span < w:
            acc = acc + shift_up(acc, span)
            span *= 2
        win_sum = shift_up(acc, POOL_HALO - half)[:tm]
        cnt = (jnp.minimum(pos + half, length) - jnp.maximum(pos - half, 0)).astype(F32)
        pooled = win_sum / cnt - z[:, g * gdim:(g + 1) * gdim]
        outs.append(_dot(pooled.astype(BF16), wg_ref[g]))
    y = (jnp.concatenate(outs, axis=1) * sc_ref[...]).astype(BF16)
    o_ref[...] = x_ref[...] + mod_ref[0][2:3] * _dot(y, wout_ref[...])


def _pool_layer(st, xs, mods, gain, w_in, w_grp, scale, w_out, with_ctx):
    tm, d = TOKEN_TILE, st.d
    width = w_in.shape[1]
    n_tiles = st.tiles(tm, with_ctx)
    z = pl.pallas_call(
        _pool_in_kernel,
        grid=(n_tiles,),
        in_specs=[_tok_spec(tm, d), st.mod_spec(tm), _full_spec((1, d)), _full_spec((d, width))],
        out_specs=_tok_spec(tm, width),
        out_shape=jax.ShapeDtypeStruct((n_tiles * tm, width), F32),
        compiler_params=_cparams("parallel"),
        name="pool_in",
    )(xs, mods, gain.reshape(1, d), w_in.astype(BF16))
    halo_per_tile = tm // POOL_HALO
    last_halo = n_tiles * halo_per_tile - 1
    kern = functools.partial(
        _pool_out_kernel, n_lat_tiles=st.n_lat // tm, lat_tiles_per_seq=st.seq // tm,
        ctx_tiles_per_seq=st.ctx_len // tm, seq=st.seq, ctx_len=st.ctx_len)
    return pl.pallas_call(
        kern,
        grid=(n_tiles,),
        in_specs=[
            _tok_spec(tm, width),
            pl.BlockSpec((POOL_HALO, width), lambda t: (jnp.maximum(t * halo_per_tile - 1, 0), 0)),
            pl.BlockSpec((POOL_HALO, width), lambda t: (jnp.minimum((t + 1) * halo_per_tile, last_halo), 0)),
            _tok_spec(tm, d), st.mod_spec(tm),
            _full_spec(w_grp.shape), _full_spec((1, width)), _full_spec((width, d)),
        ],
        out_specs=_tok_spec(tm, d),
        out_shape=jax.ShapeDtypeStruct((st.tiles(tm, with_ctx) * tm, d), F32),
        compiler_params=_cparams("parallel"),
        name="pool_out",
    )(z, z, z, xs, mods, w_grp.astype(BF16), scale.reshape(1, width), w_out.astype(BF16))


def _rope_tables(seq, tm):
    pos = jnp.arange(seq)
    axis_dim = HEAD_DIM // 2
    freqs = ROPE_THETA ** (-jnp.arange(0, axis_dim, 2, dtype=F32) / axis_dim)
    ang = jnp.stack([pos // GRID_W, pos % GRID_W], axis=-1).astype(F32)[:, :, None] * freqs
    cos, sin = jnp.cos(ang), jnp.sin(ang)
    zero = jnp.zeros_like(sin)
    c = jnp.concatenate([cos, cos], axis=-1).reshape(seq, HEAD_DIM)
    s_lo = jnp.concatenate([-sin, zero], axis=-1).reshape(seq, HEAD_DIM)
    s_hi = jnp.concatenate([zero, sin], axis=-1).reshape(seq, HEAD_DIM)
    pad = jnp.zeros((tm, HEAD_DIM), F32)
    return (jnp.concatenate([c, pad + 1.0]), jnp.concatenate([s_lo, pad]), jnp.concatenate([s_hi, pad]))


def _qkv_kernel(x_ref, mod_ref, g_ref, w_ref, qg_ref, kg_ref, c_ref, slo_ref, shi_ref, q_ref, k_ref, v_ref):
    mod = mod_ref[0]
    h = _modulate(x_ref[...], g_ref[...], mod[0:1], mod[1:2]).astype(BF16)
    qkv = _dot(h, w_ref[...])
    nq, nk = q_ref.shape[1], k_ref.shape[1]
    cos, s_lo, s_hi = c_ref[...], slo_ref[...], shi_ref[...]
    quarter = HEAD_DIM // 4

    def norm_rope(t, gain):
        t = _rms(t) * gain
        return (t * cos + pltpu.roll(t, HEAD_DIM - quarter, 1) * s_lo + pltpu.roll(t, quarter, 1) * s_hi)

    for hh in range(nq // HEAD_DIM):
        sl = slice(hh * HEAD_DIM, (hh + 1) * HEAD_DIM)
        q_ref[:, sl] = norm_rope(qkv[:, sl], qg_ref[...]).astype(BF16)
    for hh in range(nk // HEAD_DIM):
        sl = slice(hh * HEAD_DIM, (hh + 1) * HEAD_DIM)
        k_ref[:, sl] = norm_rope(qkv[:, nq + hh * HEAD_DIM:nq + (hh + 1) * HEAD_DIM], kg_ref[...]).astype(BF16)
    v_ref[...] = qkv[:, nq + nk:].astype(BF16)


def _attn_kernel(q_ref, kl_ref, vl_ref, kc_ref, vc_ref, x_ref, mod_ref, wo_ref, o_ref, *, n_lat_tiles):
    n_heads = q_ref.shape[1] // HEAD_DIM
    q_per_kv = n_heads // N_KV_HEADS
    scale = HEAD_DIM ** -0.5

    def body(with_lat):
        def kv_of(hh):
            return slice((hh // q_per_kv) * HEAD_DIM, (hh // q_per_kv + 1) * HEAD_DIM)

        def scores(hh):
            q = q_ref[:, hh * HEAD_DIM:(hh + 1) * HEAD_DIM]
            s_c = _dot_nt(q, kc_ref[:, kv_of(hh)]) * scale
            s_l = _dot_nt(q, kl_ref[:, kv_of(hh)]) * scale if with_lat else None
            return s_c, s_l

        def attend(hh, s_c, s_l):
            m = jnp.max(s_c, axis=-1, keepdims=True)
            if with_lat:
                m = jnp.maximum(m, jnp.max(s_l, axis=-1, keepdims=True))
            p_c = jnp.exp(s_c - m)
            den = jnp.sum(p_c, axis=-1, keepdims=True)
            o = _dot(p_c.astype(BF16), vc_ref[:, kv_of(hh)])
            if with_lat:
                p_l = jnp.exp(s_l - m)
                den = den + jnp.sum(p_l, axis=-1, keepdims=True)
                o = o + _dot(p_l.astype(BF16), vl_ref[:, kv_of(hh)])
            return o / den

        outs = []
        pending = scores(0)
        for hh in range(n_heads):
            upcoming = scores(hh + 1) if hh + 1 < n_heads else None
            outs.append(attend(hh, *pending))
            pending = upcoming
        y = _dot(jnp.concatenate(outs, axis=1).astype(BF16), wo_ref[...])
        o_ref[...] = x_ref[...] + mod_ref[0][2:3] * y

    t = pl.program_id(0)
    pl.when(t < n_lat_tiles)(lambda: body(True))
    pl.when(t >= n_lat_tiles)(lambda: body(False))


def _attn_layer(st, xs, mods, gain, w_qkv, q_gain, k_gain, w_o, with_ctx):
    d = st.d
    tm = st.ctx_len
    assert st.seq % tm == 0
    n_all = st.n_tok // tm
    n_lat_tiles, per_seq = st.n_lat // tm, st.seq // tm
    nq = w_o.shape[0]
    nk = (w_qkv.shape[1] - nq) // 2
    cos, s_lo, s_hi = _rope_tables(st.seq, tm)
    rope_spec = pl.BlockSpec((tm, HEAD_DIM), lambda t: (jnp.where(t < n_lat_tiles, t % per_seq, per_seq), 0))
    q, k, v = pl.pallas_call(
        _qkv_kernel,
        grid=(n_all,),
        in_specs=[
            _tok_spec(tm, d), st.mod_spec(tm), _full_spec((1, d)), _full_spec(w_qkv.shape),
            _full_spec((1, HEAD_DIM)), _full_spec((1, HEAD_DIM)), rope_spec, rope_spec, rope_spec,
        ],
        out_specs=[_tok_spec(tm, nq), _tok_spec(tm, nk), _tok_spec(tm, nk)],
        out_shape=[jax.ShapeDtypeStruct((st.n_tok, nq), BF16), jax.ShapeDtypeStruct((st.n_tok, nk), BF16),
                   jax.ShapeDtypeStruct((st.n_tok, nk), BF16)],
        compiler_params=_cparams("parallel"),
        name="attn_qkv",
    )(xs, mods, gain.reshape(1, d), w_qkv.astype(BF16), q_gain.reshape(1, HEAD_DIM),
      k_gain.reshape(1, HEAD_DIM), cos, s_lo, s_hi)

    bsz = st.bsz
    batch_of = lambda t: jnp.where(t < n_lat_tiles, t // per_seq, t - n_lat_tiles)
    lat_spec = pl.BlockSpec((st.seq, nk), lambda t: (jnp.minimum(t // per_seq, bsz - 1), 0))
    ctx_spec = pl.BlockSpec((tm, nk), lambda t: (n_lat_tiles + batch_of(t), 0))
    return pl.pallas_call(
        functools.partial(_attn_kernel, n_lat_tiles=n_lat_tiles),
        grid=(st.tiles(tm, with_ctx),),
        in_specs=[_tok_spec(tm, nq), lat_spec, lat_spec, ctx_spec, ctx_spec,
                  _tok_spec(tm, d), st.mod_spec(tm), _full_spec((nq, d))],
        out_specs=_tok_spec(tm, d),
        out_shape=jax.ShapeDtypeStruct((st.tiles(tm, with_ctx) * tm, d), F32),
        compiler_params=_cparams("parallel"),
        name="attn_core",
    )(q, k, v, k, v, xs, mods, w_o.astype(BF16))


_INT_MIN = -2 ** 31


def _sortable(x):
    b = lax.bitcast_convert_type(x, jnp.int32)
    return b ^ ((b >> 31) & 0x7FFFFFFF)


def _unsortable(k):
    return lax.bitcast_convert_type(k ^ ((k >> 31) & 0x7FFFFFFF), F32)


def _extract16(k, iota, exact):
    n = k.shape[0]
    tops = []
    for a in range(PEER_TOPK):
        m = jnp.max(k, axis=0, keepdims=True)
        hit = k == m
        if exact:
            first = jnp.min(jnp.where(hit, iota, n), axis=0, keepdims=True)
            hit = iota == first
        k = jnp.where(hit, _INT_MIN + a, k)
        tops.append(m)
    return k, tops


def _rows16(tops, iota16):
    out = jnp.zeros(iota16.shape, tops[0].dtype)
    for a, t in enumerate(tops):
        out = jnp.where(iota16 == a, t, out)
    return out


_CAND_HALF = PEER_TOPK // 2


def _peer_select(s1, s2, exact):
    n, width = s1.shape
    topk = float(PEER_TOPK)
    iota = lax.broadcasted_iota(jnp.int32, (n, width), 0)
    iota16 = iota[0:PEER_TOPK]
    mark_end = _INT_MIN + PEER_TOPK
    k1, top1 = _extract16(_sortable(s1), iota, exact)
    k2, top2 = _extract16(_sortable(s2), iota, exact)
    v1 = _unsortable(_rows16(top1, iota16))
    v2 = _unsortable(_rows16(top2, iota16))
    f1 = [_unsortable(t) for t in top1[:_CAND_HALF]]
    f2_0 = _unsortable(top2[0])
    pieces = [v2 + f1[0]]
    pieces += [v2[0:_CAND_HALF] + f1[a] for a in range(1, _CAND_HALF)]
    pieces.append(v1[_CAND_HALF:] + f2_0)
    cand = jnp.concatenate(pieces, axis=0)
    nc = cand.shape[0]
    kc, _ = _extract16(_sortable(cand), iota[0:nc], exact)
    picked = kc < mark_end
    if not exact:
        surplus = jnp.sum(jnp.where(picked, 1.0, 0.0), axis=0, keepdims=True) - topk
        picked = kc < mark_end - surplus.astype(jnp.int32)
    sel = jnp.where(picked, 1.0, 0.0)
    z = jnp.sum(jnp.where(picked, jnp.exp(cand - (f1[0] + f2_0)), 0.0), axis=0, keepdims=True)
    cnt = [jnp.sum(sel[0:PEER_TOPK], axis=0, keepdims=True)]
    for a in range(1, _CAND_HALF):
        row = PEER_TOPK + (a - 1) * _CAND_HALF
        cnt.append(jnp.sum(sel[row:row + _CAND_HALF], axis=0, keepdims=True))
    tail = sel[nc - _CAND_HALF:]
    for a in range(_CAND_HALF, PEER_TOPK):
        pick = iota[0:_CAND_HALF] == a - _CAND_HALF
        cnt.append(jnp.sum(jnp.where(pick, tail, 0.0), axis=0, keepdims=True))
    cut = jnp.zeros(s1.shape, F32)
    for a in range(PEER_TOPK):
        cut = jnp.where(k1 == _INT_MIN + a, cnt[a], cut)
    in1, in2 = k1 < mark_end, k2 < mark_end
    r2 = jnp.where(in2, (k2 - _INT_MIN).astype(F32), topk)
    e1 = jnp.exp(s1 - f1[0])
    e2 = jnp.exp(s2 - f2_0) / z
    n1 = jnp.sum(jnp.where(in1, 1.0, 0.0), axis=0, keepdims=True)
    n2 = jnp.sum(jnp.where(in2, 1.0, 0.0), axis=0, keepdims=True)
    nsel = jnp.sum(sel, axis=0, keepdims=True)
    bad = jnp.where((n1 != topk) | (n2 != topk) | (nsel != topk), 1.0, 0.0)
    return cut, e1, r2, e2, bad


def _pack_rows(x):
    return pltpu.bitcast(x.astype(BF16), jnp.uint32)


def _unpack_rows(x):
    return pltpu.bitcast(x, BF16)


def _pack_twice(x):
    u = lax.bitcast_convert_type(x.astype(BF16).astype(F32), jnp.uint32)
    return u | (u >> 16)


PACK_BLOCK = 1024


def _pack_kernel(w_ref, o_ref, *, transpose):
    if transpose:
        o_ref[0, 0] = _pack_rows(w_ref[0].T)
    else:
        o_ref[0] = _pack_rows(w_ref[0])


def _pack_weights(w, transpose):
    layers, r, c = w.shape
    pb = PACK_BLOCK
    if transpose:
        out_spec = pl.BlockSpec((1, 1, pb // 2, pb), lambda l, i, j: (l, i, j, 0))
        out_shape = (layers, r // pb, c // 2, pb)
    else:
        out_spec = pl.BlockSpec((1, pb // 2, pb), lambda l, i, j: (l, i, j))
        out_shape = (layers, r // 2, c)
    return pl.pallas_call(
        functools.partial(_pack_kernel, transpose=transpose),
        grid=(layers, r // pb, c // pb),
        in_specs=[pl.BlockSpec((1, pb, pb), lambda l, i, j: (l, i, j))],
        out_specs=out_spec,
        out_shape=jax.ShapeDtypeStruct(out_shape, jnp.uint32),
        compiler_params=_cparams("parallel", "parallel", "parallel"),
        name="pack_t" if transpose else "pack",
    )(w)


def _peer_select_kernel(x_ref, mod_ref, g_ref, wq_ref, k1_ref, k2_ref,
                        h_ref, cut_ref, e1_ref, r2_ref, e2_ref, s1_scr, s2_scr):
    mod = mod_ref[0]
    f_t = _modulate(x_ref[...], g_ref[...], mod[3:4], mod[4:5]).T.astype(BF16)
    h_ref[...] = _pack_rows(f_t)
    tm = f_t.shape[1]
    n_heads, n_keys, half = k1_ref.shape
    q_t = _dot(_unpack_rows(wq_ref[0]), f_t)
    for hh in range(n_heads):
        base = hh * 2 * half
        s1_scr[hh] = _dot(k1_ref[hh], q_t[base:base + half].astype(BF16))
        s2_scr[hh] = _dot(k2_ref[hh], q_t[base + half:base + 2 * half].astype(BF16))
    n_sub = tm // LANES

    def step(idx, carry):
        hh = idx // (n_sub // SELECT_UNROLL)
        base = (idx % (n_sub // SELECT_UNROLL)) * SELECT_UNROLL
        bad_any = jnp.zeros((1, LANES), F32)
        work = []
        for sub in range(SELECT_UNROLL):
            lanes = pl.ds(pl.multiple_of((base + sub) * LANES, LANES), LANES)
            s1, s2 = s1_scr[hh, :, lanes], s2_scr[hh, :, lanes]
            *fast, bad = _peer_select(s1, s2, exact=False)
            work.append((lanes, s1, s2, fast))
            bad_any = jnp.maximum(bad_any, bad)

        def store(lanes, cut, e1, r2, e2):
            cut_ref[hh, :, lanes] = _pack_twice(cut)
            e1_ref[hh, :, lanes] = _pack_twice(e1)
            r2_ref[hh, :, lanes] = _pack_rows(r2)
            e2_ref[hh, :, lanes] = _pack_rows(e2)

        for lanes, _, _, fast in work:
            store(lanes, *fast)

        @pl.when(jnp.max(bad_any) > 0.0)
        def _():
            for lanes, s1, s2, _ in work:
                store(lanes, *_peer_select(s1, s2, exact=True)[:4])

        return carry

    lax.fori_loop(0, n_heads * n_sub // SELECT_UNROLL, step, 0)


def _peer_main_kernel(h_ref, u_ref, vt_ref, cut_ref, e1_ref, r2_ref, e2_ref, x_ref, mod_ref, fg_ref,
                      o_ref, acc, act0, act1, w0, w1, *, final_norm, n_chunks):
    g = pl.program_id(0)
    ec, tm = act0.shape
    n_heads, n_keys = r2_ref.shape[0], 2 * r2_ref.shape[1]
    rows_per_chunk = ec // n_keys
    half_keys = n_keys // 2
    packed_half = half_keys // 2

    @pl.when(g == 0)
    def _():
        acc[...] = jnp.zeros_like(acc)
        act1[...] = jnp.zeros_like(act1)
        w0[...] = jnp.zeros_like(w0)

    def gate_chunk(cut_ref, e1_ref, act_ref, w_ref, token_blocks):
        zero = jnp.zeros((half_keys, LANES), BF16)
        for part in range(n_keys // half_keys):
            prow = slice(part * packed_half, (part + 1) * packed_half)
            for tl in token_blocks:
                lanes = slice(tl * LANES, (tl + 1) * LANES)
                for first in range(0, rows_per_chunk, GATE_ROWS):
                    gates = [zero] * GATE_ROWS
                    for hh in range(n_heads):
                        r2 = _unpack_rows(r2_ref[hh, prow, lanes])
                        e2 = _unpack_rows(e2_ref[hh, prow, lanes])
                        for k in range(GATE_ROWS):
                            il = first + k
                            cut = _unpack_rows(jnp.broadcast_to(cut_ref[hh, il:il + 1, lanes], (packed_half, LANES)))
                            e1 = _unpack_rows(jnp.broadcast_to(e1_ref[hh, il:il + 1, lanes], (packed_half, LANES)))
                            gates[k] = gates[k] + jnp.where(r2 < cut, e2, zero) * e1
                    for k in range(GATE_ROWS):
                        il = first + k
                        rows = slice(il * n_keys + part * half_keys, il * n_keys + (part + 1) * half_keys)
                        w_ref[rows, lanes] = _gelu(act_ref[rows, lanes]).astype(BF16) * gates[k]

    def stage(act_new, act_old, w_new, w_old):
        n_blocks = tm // LANES
        gate_chunk(cut_ref, e1_ref, act_old, w_new, range(0, n_blocks // 2))
        act_new[...] = _dot(_unpack_rows(u_ref[0]), _unpack_rows(h_ref[...]))
        gate_chunk(cut_ref, e1_ref, act_old, w_new, range(n_blocks // 2, n_blocks))
        acc[...] += sum(_dot(_unpack_rows(vt_ref[0, k]), w_old[k * PACK_BLOCK:(k + 1) * PACK_BLOCK, :])
                        for k in range(ec // PACK_BLOCK))

    pl.when(g % 2 == 0)(lambda: stage(act0, act1, w1, w0))
    pl.when(g % 2 == 1)(lambda: stage(act1, act0, w0, w1))

    @pl.when((g >= 2) & ((g - 2) % n_chunks == n_chunks - 1))
    def _():
        y = x_ref[...] + mod_ref[0][5:6] * acc[...].T
        if final_norm:
            y = _rms(y) * fg_ref[...]
        o_ref[...] = y
        acc[...] = jnp.zeros_like(acc)


def _peer_layer(st, xs, mods, gain, layer, wq_packed, k1, k2, u_packed, vt_packed, final_gain, with_ctx,
                final_norm):
    tm, d = PEER_TILE, st.d
    n_heads, n_keys, half = k1.shape
    n_exp = 2 * u_packed.shape[1]
    ec = PEER_EXPERT_CHUNK
    n_tiles = st.tiles(tm, with_ctx)
    n_rows = n_tiles * tm
    u32 = jnp.uint32
    h_spec = pl.BlockSpec((d // 2, tm), lambda t, *_: (0, t))
    key_spec = pl.BlockSpec((n_heads, n_keys, tm), lambda t, *_: (0, 0, t))
    key_shape = jax.ShapeDtypeStruct((n_heads, n_keys, n_rows), u32)
    pair_spec = pl.BlockSpec((n_heads, n_keys // 2, tm), lambda t, *_: (0, 0, t))
    pair_shape = jax.ShapeDtypeStruct((n_heads, n_keys // 2, n_rows), u32)
    h, cut, e1, r2, e2 = pl.pallas_call(
        _peer_select_kernel,
        grid=(n_tiles,),
        in_specs=[_tok_spec(tm, d), st.mod_spec(tm), _full_spec((1, d)),
                  pl.BlockSpec((1,) + wq_packed.shape[1:], lambda t: (layer, 0, 0)),
                  _full_spec(k1.shape), _full_spec(k2.shape)],
        out_specs=[h_spec, key_spec, key_spec, pair_spec, pair_spec],
        out_shape=[jax.ShapeDtypeStruct((d // 2, n_rows), u32), key_shape, key_shape, pair_shape, pair_shape],
        scratch_shapes=[pltpu.VMEM((n_heads, n_keys, tm), F32), pltpu.VMEM((n_heads, n_keys, tm), F32)],
        compiler_params=_cparams("parallel"),
        name="peer_select",
    )(xs, mods, gain.reshape(1, d), wq_packed, k1.astype(BF16), k2.astype(BF16))

    n_chunks = n_exp // ec
    n_items = n_tiles * n_chunks
    rows_per_chunk = ec // n_keys
    mod_row = st.mod_row(tm)

    def item(lag):
        def tile(g):
            return jnp.clip(g - lag, 0, n_items - 1) // n_chunks

        def chunk(g):
            return jnp.clip(g - lag, 0, n_items - 1) % n_chunks

        return tile, chunk

    (tile0, chunk0), (tile1, chunk1), (tile2, chunk2) = item(0), item(1), item(2)
    sub1_spec = pl.BlockSpec((n_heads, rows_per_chunk, tm), lambda g: (0, chunk1(g), tile1(g)))
    sub2_spec = pl.BlockSpec((n_heads, n_keys // 2, tm), lambda g: (0, 0, tile1(g)))
    return pl.pallas_call(
        functools.partial(_peer_main_kernel, final_norm=final_norm, n_chunks=n_chunks),
        grid=(n_items + 2,),
        in_specs=[
            pl.BlockSpec((d // 2, tm), lambda g: (0, tile0(g))),
            pl.BlockSpec((1, ec // 2, d), lambda g: (layer, chunk0(g), 0)),
            pl.BlockSpec((1, ec // PACK_BLOCK, d // 2, PACK_BLOCK), lambda g: (layer, chunk2(g), 0, 0)),
            sub1_spec, sub1_spec, sub2_spec, sub2_spec,
            pl.BlockSpec((tm, d), lambda g: (tile2(g), 0)),
            pl.BlockSpec((1, MOD_CHUNKS, d), lambda g: (mod_row(tile2(g)), 0, 0)),
            _full_spec((1, d)),
        ],
        out_specs=pl.BlockSpec((tm, d), lambda g: (tile2(g), 0)),
        out_shape=jax.ShapeDtypeStruct((n_rows, d), F32),
        scratch_shapes=[pltpu.VMEM((d, tm), F32),
                        pltpu.VMEM((ec, tm), F32), pltpu.VMEM((ec, tm), F32),
                        pltpu.VMEM((ec, tm), BF16), pltpu.VMEM((ec, tm), BF16)],
        compiler_params=_cparams("arbitrary"),
        name="peer_main",
    )(h, u_packed, vt_packed, cut, e1, r2, e2, xs, mods, final_gain.reshape(1, d))


def _peer_fused_kernel(xn_ref, modn_ref, g_ref, wq_ref, k1_ref, k2_ref, u_ref, vt_ref, x_ref, mod_ref, fg_ref,
                       o_ref, acc, act0, act1, w0, w1, ht_buf, q_buf, cut_buf, e1_buf, r2_buf, e2_buf,
                       *, final_norm, n_chunks, n_items):
    g = pl.program_id(0)
    ec, tm = act0.shape
    n_heads, n_keys, half = k1_ref.shape
    rows_per_chunk = ec // n_keys
    half_keys = n_keys // 2
    packed_half = half_keys // 2
    subs_per_head = tm // LANES // SELECT_UNROLL
    assert n_heads * subs_per_head == n_chunks

    last = n_items - 1
    sel_item = jnp.clip(g - 1, 0, last)
    sel_slot, sel_iter = (sel_item // n_chunks) % 2, sel_item % n_chunks
    slot_a = (jnp.clip(g - n_chunks, 0, last) // n_chunks) % 2
    item_b = jnp.clip(g - n_chunks - 1, 0, last)
    slot_b, chunk_b = (item_b // n_chunks) % 2, item_b % n_chunks

    @pl.when(g == 0)
    def _():
        acc[...] = jnp.zeros_like(acc)
        act1[...] = jnp.zeros_like(act1)
        w0[...] = jnp.zeros_like(w0)

    @pl.when((g % n_chunks == 0) & (g < n_items))
    def _():
        slot = (g // n_chunks) % 2
        mod = modn_ref[0]
        f_t = _modulate(xn_ref[...], g_ref[...], mod[3:4], mod[4:5]).T.astype(BF16)
        ht_buf[slot] = _pack_rows(f_t)
        q_buf[slot] = _dot(_unpack_rows(wq_ref[...]), f_t).astype(BF16)

    def gate_chunk(act_ref, w_ref):
        zero = jnp.zeros((half_keys, LANES), BF16)
        for part in range(n_keys // half_keys):
            prow = slice(part * packed_half, (part + 1) * packed_half)
            for tl in range(tm // LANES):
                lanes = slice(tl * LANES, (tl + 1) * LANES)
                gates = [zero] * rows_per_chunk
                for hh in range(n_heads):
                    r2 = _unpack_rows(r2_buf[slot_b, hh, prow, lanes])
                    e2 = _unpack_rows(e2_buf[slot_b, hh, prow, lanes])
                    for il in range(rows_per_chunk):
                        cut = cut_buf[slot_b, chunk_b, hh, il:il + 1, lanes]
                        e1 = e1_buf[slot_b, chunk_b, hh, il:il + 1, lanes]
                        cut = _unpack_rows(jnp.broadcast_to(cut, (packed_half, LANES)))
                        e1 = _unpack_rows(jnp.broadcast_to(e1, (packed_half, LANES)))
                        gates[il] = gates[il] + jnp.where(r2 < cut, e2, zero) * e1
                for il in range(rows_per_chunk):
                    rows = slice(il * n_keys + part * half_keys, il * n_keys + (part + 1) * half_keys)
                    w_ref[rows, lanes] = _gelu(act_ref[rows, lanes]).astype(BF16) * gates[il]

    def select_iteration(dense=None):
        hh = sel_iter // subs_per_head
        width = SELECT_UNROLL * LANES
        lanes = pl.ds(pl.multiple_of((sel_iter % subs_per_head) * width, width), width)
        qrow = pl.multiple_of(hh * 2 * half, 2 * half)
        s1_all = _dot(k1_ref[hh], q_buf[sel_slot, pl.ds(qrow, half), lanes])
        s2_all = _dot(k2_ref[hh], q_buf[sel_slot, pl.ds(qrow + half, half), lanes])
        groups = n_keys // rows_per_chunk

        def store(results):
            cut, e1, r2, e2 = (jnp.concatenate(parts, axis=1) for parts in zip(*results))
            cut_buf[sel_slot, :, hh, :, lanes] = _pack_twice(cut).reshape(groups, rows_per_chunk, width)
            e1_buf[sel_slot, :, hh, :, lanes] = _pack_twice(e1).reshape(groups, rows_per_chunk, width)
            r2_buf[sel_slot, hh, :, lanes] = _pack_rows(r2)
            e2_buf[sel_slot, hh, :, lanes] = _pack_rows(e2)

        scores = [(s1_all[:, sub * LANES:(sub + 1) * LANES], s2_all[:, sub * LANES:(sub + 1) * LANES])
                  for sub in range(SELECT_UNROLL)]
        fast = [_peer_select(s1, s2, exact=False) for s1, s2 in scores]
        store([f[:4] for f in fast])
        bad = functools.reduce(jnp.maximum, [f[4] for f in fast])
        if dense is not None:
            dense()

        @pl.when(jnp.max(bad) > 0.0)
        def _():
            store([_peer_select(s1, s2, exact=True)[:4] for s1, s2 in scores])

    def dense_stage(act_new, act_old, w_new, w_old):
        act_new[...] = _dot(_unpack_rows(u_ref[...]), _unpack_rows(ht_buf[slot_a]))
        gate_chunk(act_old, w_new)
        acc[...] += _dot(_unpack_rows(vt_ref[...]), w_old[...])

    @pl.when(g < n_chunks)
    def _():
        select_iteration()

    @pl.when((g >= n_chunks) & (g % 2 == 0))
    def _():
        select_iteration(lambda: dense_stage(act0, act1, w1, w0))

    @pl.when((g >= n_chunks) & (g % 2 == 1))
    def _():
        select_iteration(lambda: dense_stage(act1, act0, w0, w1))

    done = g - n_chunks - 2
    @pl.when((done >= 0) & (done % n_chunks == n_chunks - 1))
    def _():
        y = x_ref[...] + mod_ref[0][5:6] * acc[...].T
        if final_norm:
            y = _rms(y) * fg_ref[...]
        o_ref[...] = y
        acc[...] = jnp.zeros_like(acc)


def _peer_fused_layer(st, xs, mods, gain, wq_packed, k1, k2, u_packed, vt_packed, final_gain, with_ctx,
                      final_norm):
    tm, d = PEER_TILE, st.d
    n_heads, n_keys, half = k1.shape
    n_exp = 2 * u_packed.shape[0]
    ec = PEER_EXPERT_CHUNK
    n_tiles = st.tiles(tm, with_ctx)
    n_chunks = n_exp // ec
    n_items = n_tiles * n_chunks
    rows_per_chunk = ec // n_keys
    mod_row = st.mod_row(tm)
    u32 = jnp.uint32

    def lagged(lag):
        def tile(g):
            return jnp.clip(g - lag, 0, n_items - 1) // n_chunks

        def chunk(g):
            return jnp.clip(g - lag, 0, n_items - 1) % n_chunks

        return tile, chunk

    sel_tile, _ = lagged(0)
    _, chunk_a = lagged(n_chunks)
    tile_c, chunk_c = lagged(n_chunks + 2)
    return pl.pallas_call(
        functools.partial(_peer_fused_kernel, final_norm=final_norm, n_chunks=n_chunks, n_items=n_items),
        grid=(n_items + n_chunks + 2,),
        in_specs=[
            pl.BlockSpec((tm, d), lambda g: (sel_tile(g), 0)),
            pl.BlockSpec((1, MOD_CHUNKS, d), lambda g: (mod_row(sel_tile(g)), 0, 0)),
            _full_spec((1, d)), _full_spec(wq_packed.shape), _full_spec(k1.shape), _full_spec(k2.shape),
            pl.BlockSpec((ec // 2, d), lambda g: (chunk_a(g), 0)),
            pl.BlockSpec((d // 2, ec), lambda g: (0, chunk_c(g))),
            pl.BlockSpec((tm, d), lambda g: (tile_c(g), 0)),
            pl.BlockSpec((1, MOD_CHUNKS, d), lambda g: (mod_row(tile_c(g)), 0, 0)),
            _full_spec((1, d)),
        ],
        out_specs=pl.BlockSpec((tm, d), lambda g: (tile_c(g), 0)),
        out_shape=jax.ShapeDtypeStruct((n_tiles * tm, d), F32),
        scratch_shapes=[
            pltpu.VMEM((d, tm), F32),
            pltpu.VMEM((ec, tm), F32), pltpu.VMEM((ec, tm), F32),
            pltpu.VMEM((ec, tm), BF16), pltpu.VMEM((ec, tm), BF16),
            pltpu.VMEM((2, d // 2, tm), u32),
            pltpu.VMEM((2, 2 * half * n_heads, tm), BF16),
            pltpu.VMEM((2, n_keys // rows_per_chunk, n_heads, rows_per_chunk, tm), u32),
            pltpu.VMEM((2, n_keys // rows_per_chunk, n_heads, rows_per_chunk, tm), u32),
            pltpu.VMEM((2, n_heads, n_keys // 2, tm), u32),
            pltpu.VMEM((2, n_heads, n_keys // 2, tm), u32),
        ],
        compiler_params=pltpu.CompilerParams(dimension_semantics=("arbitrary",),
                                             vmem_limit_bytes=FUSED_VMEM_LIMIT),
        name="peer_fused",
    )(xs, mods, gain.reshape(1, d), wq_packed, k1.astype(BF16), k2.astype(BF16), u_packed, vt_packed,
      xs, mods, final_gain.reshape(1, d))


def kernel(x, c, ctx, c_ctx, ada_w, ada_b, norm1_g, norm2_g, sgu_w_in, sgu_v_gain, sgu_w_s, sgu_b_s, sgu_w_out, pool_w_in, pool_w_grp, pool_scale, pool_w_out, attn_w_qkv, attn_q_gain, attn_k_gain, attn_w_o, peer_w_q, peer_k1, peer_k2, peer_u, peer_v, final_gain):
    bsz, seq, d = x.shape
    ctx_len = ctx.shape[1]
    depth = ada_w.shape[0]
    st = _Stream(bsz, seq, ctx_len, d)
    assert bsz + 1 <= MOD_ROWS and seq % PEER_TILE == 0 and (bsz * ctx_len) % PEER_TILE == 0
    assert seq % TOKEN_TILE == 0 and ctx_len % TOKEN_TILE == 0

    cond = jnp.concatenate([c, c_ctx[None, :], jnp.zeros((MOD_ROWS - bsz - 1, d), F32)], axis=0)
    mods = _ada_mods(cond, ada_w, ada_b)
    assert peer_w_q.shape[1] == PACK_BLOCK and PEER_EXPERT_CHUNK % PACK_BLOCK == 0
    wq_packed = _pack_weights(peer_w_q, transpose=True)[:, 0]
    u_packed = _pack_weights(peer_u, transpose=False)
    vt_packed = _pack_weights(peer_v, transpose=True)
    xs = jnp.concatenate([x.reshape(bsz * seq, d), ctx.reshape(bsz * ctx_len, d)], axis=0)

    for i in range(depth):
        kind, j = i % N_MIXERS, i // N_MIXERS
        with_ctx = i < depth - 1
        if kind == 0:
            xs = _sgu_layer(st, xs, mods[i], norm1_g[i], sgu_w_in[j], sgu_v_gain[j], sgu_w_s[j], sgu_b_s[j],
                            sgu_w_out[j], with_ctx)
        elif kind == 1:
            xs = _pool_layer(st, xs, mods[i], norm1_g[i], pool_w_in[j], pool_w_grp[j], pool_scale[j],
                             pool_w_out[j], with_ctx)
        else:
            xs = _attn_layer(st, xs, mods[i], norm1_g[i], attn_w_qkv[j], attn_q_gain[j], attn_k_gain[j],
                             attn_w_o[j], with_ctx)
        xs = _peer_layer(st, xs, mods[i], norm2_g[i], i, wq_packed, peer_k1[i], peer_k2[i], u_packed, vt_packed,
                         final_gain, with_ctx, final_norm=(i == depth - 1))
    return xs[:bsz * seq].reshape(bsz, seq, d)
```

```python
import functools
import math

import jax
import jax.numpy as jnp
from jax import lax
from jax.experimental import pallas as pl
from jax.experimental.pallas import tpu as pltpu

F32 = jnp.float32
BF16 = jnp.bfloat16

NORM_EPS = 1e-6
MOD_CHUNKS = 6
GRID_W = 64
SGU_CHUNK = 128
SGU_GROUPS = 8
POOL_WINDOWS = (2, 4, 8, 16)
HEAD_DIM = 128
N_KV_HEADS = 2
ROPE_THETA = 10000.0
PEER_HEADS = 8
PEER_KEYS = 128
PEER_TOPK = 16
N_MIXERS = 3

LANES = 128
MOD_ROWS = 16
VMEM_LIMIT = 56 * 1024 * 1024

TOKEN_TILE = 256
PEER_TILE = 512
PEER_EXPERT_CHUNK = 1024
GATE_ROWS = 8
SELECT_UNROLL = 2


def _cparams(*sem):
    return pltpu.CompilerParams(dimension_semantics=sem, vmem_limit_bytes=VMEM_LIMIT)


def _rms(x):
    return x * lax.rsqrt(jnp.mean(x * x, axis=-1, keepdims=True) + NORM_EPS)


def _modulate(x, gain, shift, scale):
    return _rms(x) * gain * (1.0 + scale) + shift


def _gelu(x):
    return 0.5 * x * (1.0 + lax.erf(x * (1.0 / math.sqrt(2.0))))


def _dot(a, b):
    return jnp.dot(a, b, preferred_element_type=F32)


def _dot_nt(a, b):
    return lax.dot_general(a, b, (((1,), (1,)), ((), ())), preferred_element_type=F32)


def _ada_kernel(c_ref, w_ref, b_ref, o_ref):
    c = c_ref[...]
    a = c * (1.0 / (1.0 + jnp.exp(-c)))
    o_ref[0] = _dot(a.astype(BF16), w_ref[0].astype(BF16)) + b_ref[0]


def _ada_mods(cond, ada_w, ada_b):
    depth, d, n = ada_w.shape
    tn = n // 4
    out = pl.pallas_call(
        _ada_kernel,
        grid=(depth, n // tn),
        in_specs=[
            pl.BlockSpec((MOD_ROWS, d), lambda l, j: (0, 0)),
            pl.BlockSpec((1, d, tn), lambda l, j: (l, 0, j)),
            pl.BlockSpec((1, 1, tn), lambda l, j: (l, 0, j)),
        ],
        out_specs=pl.BlockSpec((1, MOD_ROWS, tn), lambda l, j: (l, 0, j)),
        out_shape=jax.ShapeDtypeStruct((depth, MOD_ROWS, n), F32),
        compiler_params=_cparams("parallel", "parallel"),
        name="ada_mods",
    )(cond, ada_w, ada_b.reshape(depth, 1, n))
    return out.reshape(depth, MOD_ROWS, MOD_CHUNKS, d)


class _Stream:
    def __init__(self, bsz, seq, ctx_len, d):
        self.bsz, self.seq, self.ctx_len, self.d = bsz, seq, ctx_len, d
        self.n_lat = bsz * seq
        self.n_tok = bsz * (seq + ctx_len)

    def tiles(self, tm, with_ctx):
        return (self.n_tok if with_ctx else self.n_lat) // tm

    def mod_row(self, tm):
        n_lat_tiles, per_seq, bsz = self.n_lat // tm, self.seq // tm, self.bsz
        return lambda t: jnp.where(t < n_lat_tiles, t // per_seq, bsz)

    def mod_spec(self, tm):
        row = self.mod_row(tm)
        return pl.BlockSpec((1, MOD_CHUNKS, self.d), lambda t, *_: (row(t), 0, 0))


def _tok_spec(tm, width):
    return pl.BlockSpec((tm, width), lambda t, *_: (t, 0))


def _full_spec(shape):
    zeros = (0,) * len(shape)
    return pl.BlockSpec(shape, lambda *_: zeros)


def _sgu_kernel(x_ref, mod_ref, g_ref, win_ref, vg_ref, ws_ref, bs_ref, wout_ref, o_ref):
    x = x_ref[...]
    mod = mod_ref[0]
    tm = x.shape[0]
    width = vg_ref.shape[1]
    gdim = width // SGU_GROUPS
    h = _modulate(x, g_ref[...], mod[0:1], mod[1:2]).astype(BF16)
    u = _gelu(_dot(h, win_ref[:, :width]))
    v = _gelu(_dot(h, win_ref[:, width:]))
    vn = (_rms(v) * vg_ref[...]).astype(BF16)
    rows = []
    for c in range(tm // SGU_CHUNK):
        cols = []
        for g in range(SGU_GROUPS):
            vv = vn[c * SGU_CHUNK:(c + 1) * SGU_CHUNK, g * gdim:(g + 1) * gdim]
            cols.append(_dot(ws_ref[g], vv) + bs_ref[:, g:g + 1])
        rows.append(jnp.concatenate(cols, axis=1))
    sv = jnp.concatenate(rows, axis=0)
    y = _dot((u * sv).astype(BF16), wout_ref[...])
    o_ref[...] = x + mod[2:3] * y


def _sgu_layer(st, xs, mods, gain, w_in, v_gain, w_s, b_s, w_out, with_ctx):
    tm, d = TOKEN_TILE, st.d
    width = v_gain.shape[0]
    return pl.pallas_call(
        _sgu_kernel,
        grid=(st.tiles(tm, with_ctx),),
        in_specs=[
            _tok_spec(tm, d), st.mod_spec(tm), _full_spec((1, d)),
            _full_spec((d, 2 * width)), _full_spec((1, width)),
            _full_spec((SGU_GROUPS, SGU_CHUNK, SGU_CHUNK)), _full_spec((SGU_CHUNK, SGU_GROUPS)),
            _full_spec((width, d)),
        ],
        out_specs=_tok_spec(tm, d),
        out_shape=jax.ShapeDtypeStruct((st.tiles(tm, with_ctx) * tm, d), F32),
        compiler_params=_cparams("parallel"),
        name="sgu_mixer",
    )(xs, mods, gain.reshape(1, d), w_in.astype(BF16), v_gain.reshape(1, width),
      w_s.astype(BF16), b_s.T, w_out.astype(BF16))


POOL_HALO = 8


def _pool_in_kernel(x_ref, mod_ref, g_ref, win_ref, z_ref):
    mod = mod_ref[0]
    h = _modulate(x_ref[...], g_ref[...], mod[0:1], mod[1:2]).astype(BF16)
    z_ref[...] = _dot(h, win_ref[...])


def _pool_out_kernel(z_ref, zp_ref, zn_ref, x_ref, mod_ref, wg_ref, sc_ref, wout_ref, o_ref,
                     *, n_lat_tiles, lat_tiles_per_seq, ctx_tiles_per_seq, seq, ctx_len):
    t = pl.program_id(0)
    tm = z_ref.shape[0]
    is_lat = t < n_lat_tiles
    per_seq = jnp.where(is_lat, lat_tiles_per_seq, ctx_tiles_per_seq)
    pos_tile = jnp.where(is_lat, t, t - n_lat_tiles) % per_seq
    length = jnp.where(is_lat, seq, ctx_len)
    has_prev = (pos_tile > 0).astype(F32)
    has_next = (pos_tile < per_seq - 1).astype(F32)
    z = z_ref[...]
    zext = jnp.concatenate([zp_ref[...] * has_prev, z, zn_ref[...] * has_next], axis=0)
    pos = (pos_tile * tm + lax.broadcasted_iota(jnp.int32, (tm, 1), 0))
    gdim = wg_ref.shape[1]
    n_ext = tm + 2 * POOL_HALO

    def shift_up(a, k):
        return a if k == 0 else pltpu.roll(a, n_ext - k, 0)

    outs = []
    for g, w in enumerate(POOL_WINDOWS):
        half = w // 2
        acc = zext[:, g * gdim:(g + 1) * gdim]
        span = 1
        while span < w:
            acc = acc + shift_up(acc, span)
            span *= 2
        win_sum = shift_up(acc, POOL_HALO - half)[:tm]
        cnt = (jnp.minimum(pos + half, length) - jnp.maximum(pos - half, 0)).astype(F32)
        pooled = win_sum / cnt - z[:, g * gdim:(g + 1) * gdim]
        outs.append(_dot(pooled.astype(BF16), wg_ref[g]))
    y = (jnp.concatenate(outs, axis=1) * sc_ref[...]).astype(BF16)
    o_ref[...] = x_ref[...] + mod_ref[0][2:3] * _dot(y, wout_ref[...])


def _pool_layer(st, xs, mods, gain, w_in, w_grp, scale, w_out, with_ctx):
    tm, d = TOKEN_TILE, st.d
    width = w_in.shape[1]
    n_tiles = st.tiles(tm, with_ctx)
    z = pl.pallas_call(
        _pool_in_kernel,
        grid=(n_tiles,),
        in_specs=[_tok_spec(tm, d), st.mod_spec(tm), _full_spec((1, d)), _full_spec((d, width))],
        out_specs=_tok_spec(tm, width),
        out_shape=jax.ShapeDtypeStruct((n_tiles * tm, width), F32),
        compiler_params=_cparams("parallel"),
        name="pool_in",
    )(xs, mods, gain.reshape(1, d), w_in.astype(BF16))
    halo_per_tile = tm // POOL_HALO
    last_halo = n_tiles * halo_per_tile - 1
    kern = functools.partial(
        _pool_out_kernel, n_lat_tiles=st.n_lat // tm, lat_tiles_per_seq=st.seq // tm,
        ctx_tiles_per_seq=st.ctx_len // tm, seq=st.seq, ctx_len=st.ctx_len)
    return pl.pallas_call(
        kern,
        grid=(n_tiles,),
        in_specs=[
            _tok_spec(tm, width),
            pl.BlockSpec((POOL_HALO, width), lambda t: (jnp.maximum(t * halo_per_tile - 1, 0), 0)),
            pl.BlockSpec((POOL_HALO, width), lambda t: (jnp.minimum((t + 1) * halo_per_tile, last_halo), 0)),
            _tok_spec(tm, d), st.mod_spec(tm),
            _full_spec(w_grp.shape), _full_spec((1, width)), _full_spec((width, d)),
        ],
        out_specs=_tok_spec(tm, d),
        out_shape=jax.ShapeDtypeStruct((st.tiles(tm, with_ctx) * tm, d), F32),
        compiler_params=_cparams("parallel"),
        name="pool_out",
    )(z, z, z, xs, mods, w_grp.astype(BF16), scale.reshape(1, width), w_out.astype(BF16))


def _rope_tables(seq, tm):
    pos = jnp.arange(seq)
    axis_dim = HEAD_DIM // 2
    freqs = ROPE_THETA ** (-jnp.arange(0, axis_dim, 2, dtype=F32) / axis_dim)
    ang = jnp.stack([pos // GRID_W, pos % GRID_W], axis=-1).astype(F32)[:, :, None] * freqs
    cos, sin = jnp.cos(ang), jnp.sin(ang)
    zero = jnp.zeros_like(sin)
    c = jnp.concatenate([cos, cos], axis=-1).reshape(seq, HEAD_DIM)
    s_lo = jnp.concatenate([-sin, zero], axis=-1).reshape(seq, HEAD_DIM)
    s_hi = jnp.concatenate([zero, sin], axis=-1).reshape(seq, HEAD_DIM)
    pad = jnp.zeros((tm, HEAD_DIM), F32)
    return (jnp.concatenate([c, pad + 1.0]), jnp.concatenate([s_lo, pad]), jnp.concatenate([s_hi, pad]))


def _qkv_kernel(x_ref, mod_ref, g_ref, w_ref, qg_ref, kg_ref, c_ref, slo_ref, shi_ref, q_ref, k_ref, v_ref):
    mod = mod_ref[0]
    h = _modulate(x_ref[...], g_ref[...], mod[0:1], mod[1:2]).astype(BF16)
    qkv = _dot(h, w_ref[...])
    nq, nk = q_ref.shape[1], k_ref.shape[1]
    cos, s_lo, s_hi = c_ref[...], slo_ref[...], shi_ref[...]
    quarter = HEAD_DIM // 4

    def norm_rope(t, gain):
        t = _rms(t) * gain
        return (t * cos + pltpu.roll(t, HEAD_DIM - quarter, 1) * s_lo + pltpu.roll(t, quarter, 1) * s_hi)

    for hh in range(nq // HEAD_DIM):
        sl = slice(hh * HEAD_DIM, (hh + 1) * HEAD_DIM)
        q_ref[:, sl] = norm_rope(qkv[:, sl], qg_ref[...]).astype(BF16)
    for hh in range(nk // HEAD_DIM):
        sl = slice(hh * HEAD_DIM, (hh + 1) * HEAD_DIM)
        k_ref[:, sl] = norm_rope(qkv[:, nq + hh * HEAD_DIM:nq + (hh + 1) * HEAD_DIM], kg_ref[...]).astype(BF16)
    v_ref[...] = qkv[:, nq + nk:].astype(BF16)


def _attn_kernel(q_ref, kl_ref, vl_ref, kc_ref, vc_ref, x_ref, mod_ref, wo_ref, o_ref, *, n_lat_tiles):
    n_heads = q_ref.shape[1] // HEAD_DIM
    q_per_kv = n_heads // N_KV_HEADS
    scale = HEAD_DIM ** -0.5

    def body(with_lat):
        def kv_of(hh):
            return slice((hh // q_per_kv) * HEAD_DIM, (hh // q_per_kv + 1) * HEAD_DIM)

        def scores(hh):
            q = q_ref[:, hh * HEAD_DIM:(hh + 1) * HEAD_DIM]
            s_c = _dot_nt(q, kc_ref[:, kv_of(hh)]) * scale
            s_l = _dot_nt(q, kl_ref[:, kv_of(hh)]) * scale if with_lat else None
            return s_c, s_l

        def attend(hh, s_c, s_l):
            m = jnp.max(s_c, axis=-1, keepdims=True)
            if with_lat:
                m = jnp.maximum(m, jnp.max(s_l, axis=-1, keepdims=True))
            p_c = jnp.exp(s_c - m)
            den = jnp.sum(p_c, axis=-1, keepdims=True)
            o = _dot(p_c.astype(BF16), vc_ref[:, kv_of(hh)])
            if with_lat:
                p_l = jnp.exp(s_l - m)
                den = den + jnp.sum(p_l, axis=-1, keepdims=True)
                o = o + _dot(p_l.astype(BF16), vl_ref[:, kv_of(hh)])
            return o / den

        outs = []
        pending = scores(0)
        for hh in range(n_heads):
            upcoming = scores(hh + 1) if hh + 1 < n_heads else None
            outs.append(attend(hh, *pending))
            pending = upcoming
        y = _dot(jnp.concatenate(outs, axis=1).astype(BF16), wo_ref[...])
        o_ref[...] = x_ref[...] + mod_ref[0][2:3] * y

    t = pl.program_id(0)
    pl.when(t < n_lat_tiles)(lambda: body(True))
    pl.when(t >= n_lat_tiles)(lambda: body(False))


def _attn_layer(st, xs, mods, gain, w_qkv, q_gain, k_gain, w_o, with_ctx):
    d = st.d
    tm = st.ctx_len
    assert st.seq % tm == 0
    n_all = st.n_tok // tm
    n_lat_tiles, per_seq = st.n_lat // tm, st.seq // tm
    nq = w_o.shape[0]
    nk = (w_qkv.shape[1] - nq) // 2
    cos, s_lo, s_hi = _rope_tables(st.seq, tm)
    rope_spec = pl.BlockSpec((tm, HEAD_DIM), lambda t: (jnp.where(t < n_lat_tiles, t % per_seq, per_seq), 0))
    q, k, v = pl.pallas_call(
        _qkv_kernel,
        grid=(n_all,),
        in_specs=[
            _tok_spec(tm, d), st.mod_spec(tm), _full_spec((1, d)), _full_spec(w_qkv.shape),
            _full_spec((1, HEAD_DIM)), _full_spec((1, HEAD_DIM)), rope_spec, rope_spec, rope_spec,
        ],
        out_specs=[_tok_spec(tm, nq), _tok_spec(tm, nk), _tok_spec(tm, nk)],
        out_shape=[jax.ShapeDtypeStruct((st.n_tok, nq), BF16), jax.ShapeDtypeStruct((st.n_tok, nk), BF16),
                   jax.ShapeDtypeStruct((st.n_tok, nk), BF16)],
        compiler_params=_cparams("parallel"),
        name="attn_qkv",
    )(xs, mods, gain.reshape(1, d), w_qkv.astype(BF16), q_gain.reshape(1, HEAD_DIM),
      k_gain.reshape(1, HEAD_DIM), cos, s_lo, s_hi)

    bsz = st.bsz
    batch_of = lambda t: jnp.where(t < n_lat_tiles, t // per_seq, t - n_lat_tiles)
    lat_spec = pl.BlockSpec((st.seq, nk), lambda t: (jnp.minimum(t // per_seq, bsz - 1), 0))
    ctx_spec = pl.BlockSpec((tm, nk), lambda t: (n_lat_tiles + batch_of(t), 0))
    return pl.pallas_call(
        functools.partial(_attn_kernel, n_lat_tiles=n_lat_tiles),
        grid=(st.tiles(tm, with_ctx),),
        in_specs=[_tok_spec(tm, nq), lat_spec, lat_spec, ctx_spec, ctx_spec,
                  _tok_spec(tm, d), st.mod_spec(tm), _full_spec((nq, d))],
        out_specs=_tok_spec(tm, d),
        out_shape=jax.ShapeDtypeStruct((st.tiles(tm, with_ctx) * tm, d), F32),
        compiler_params=_cparams("parallel"),
        name="attn_core",
    )(q, k, v, k, v, xs, mods, w_o.astype(BF16))


_INT_MIN = -2 ** 31


def _sortable(x):
    b = lax.bitcast_convert_type(x, jnp.int32)
    return b ^ ((b >> 31) & 0x7FFFFFFF)


def _unsortable(k):
    return lax.bitcast_convert_type(k ^ ((k >> 31) & 0x7FFFFFFF), F32)


def _extract16(k, iota, exact):
    n = k.shape[0]
    tops = []
    for a in range(PEER_TOPK):
        m = jnp.max(k, axis=0, keepdims=True)
        hit = k == m
        if exact:
            first = jnp.min(jnp.where(hit, iota, n), axis=0, keepdims=True)
            hit = iota == first
        k = jnp.where(hit, _INT_MIN + a, k)
        tops.append(m)
    return k, tops


def _rows16(tops, iota16):
    out = jnp.zeros(iota16.shape, tops[0].dtype)
    for a, t in enumerate(tops):
        out = jnp.where(iota16 == a, t, out)
    return out


_CAND_HALF = PEER_TOPK // 2


def _peer_select(s1, s2, exact):
    n, width = s1.shape
    topk = float(PEER_TOPK)
    iota = lax.broadcasted_iota(jnp.int32, (n, width), 0)
    iota16 = iota[0:PEER_TOPK]
    mark_end = _INT_MIN + PEER_TOPK
    k1, top1 = _extract16(_sortable(s1), iota, exact)
    k2, top2 = _extract16(_sortable(s2), iota, exact)
    v1 = _unsortable(_rows16(top1, iota16))
    v2 = _unsortable(_rows16(top2, iota16))
    f1 = [_unsortable(t) for t in top1[:_CAND_HALF]]
    f2_0 = _unsortable(top2[0])
    pieces = [v2 + f1[0]]
    pieces += [v2[0:_CAND_HALF] + f1[a] for a in range(1, _CAND_HALF)]
    pieces.append(v1[_CAND_HALF:] + f2_0)
    cand = jnp.concatenate(pieces, axis=0)
    nc = cand.shape[0]
    kc, _ = _extract16(_sortable(cand), iota[0:nc], exact)
    picked = kc < mark_end
    if not exact:
        surplus = jnp.sum(jnp.where(picked, 1.0, 0.0), axis=0, keepdims=True) - topk
        picked = kc < mark_end - surplus.astype(jnp.int32)
    sel = jnp.where(picked, 1.0, 0.0)
    z = jnp.sum(jnp.where(picked, jnp.exp(cand - (f1[0] + f2_0)), 0.0), axis=0, keepdims=True)
    cnt = [jnp.sum(sel[0:PEER_TOPK], axis=0, keepdims=True)]
    for a in range(1, _CAND_HALF):
        row = PEER_TOPK + (a - 1) * _CAND_HALF
        cnt.append(jnp.sum(sel[row:row + _CAND_HALF], axis=0, keepdims=True))
    tail = sel[nc - _CAND_HALF:]
    for a in range(_CAND_HALF, PEER_TOPK):
        pick = iota[0:_CAND_HALF] == a - _CAND_HALF
        cnt.append(jnp.sum(jnp.where(pick, tail, 0.0), axis=0, keepdims=True))
    cut = jnp.zeros(s1.shape, F32)
    for a in range(PEER_TOPK):
        cut = jnp.where(k1 == _INT_MIN + a, cnt[a], cut)
    in1, in2 = k1 < mark_end, k2 < mark_end
    r2 = jnp.where(in2, (k2 - _INT_MIN).astype(F32), topk)
    e1 = jnp.exp(s1 - f1[0])
    e2 = jnp.exp(s2 - f2_0) / z
    n1 = jnp.sum(jnp.where(in1, 1.0, 0.0), axis=0, keepdims=True)
    n2 = jnp.sum(jnp.where(in2, 1.0, 0.0), axis=0, keepdims=True)
    nsel = jnp.sum(sel, axis=0, keepdims=True)
    bad = jnp.where((n1 != topk) | (n2 != topk) | (nsel != topk), 1.0, 0.0)
    return cut, e1, r2, e2, bad


def _pack_rows(x):
    return pltpu.bitcast(x.astype(BF16), jnp.uint32)


def _unpack_rows(x):
    return pltpu.bitcast(x, BF16)


def _pack_twice(x):
    u = lax.bitcast_convert_type(x.astype(BF16).astype(F32), jnp.uint32)
    return u | (u >> 16)


PACK_BLOCK = 1024


def _pack_kernel(w_ref, o_ref, *, transpose):
    if transpose:
        o_ref[0, 0] = _pack_rows(w_ref[0].T)
    else:
        o_ref[0] = _pack_rows(w_ref[0])


def _pack_weights(w, transpose):
    layers, r, c = w.shape
    pb = PACK_BLOCK
    if transpose:
        out_spec = pl.BlockSpec((1, 1, pb // 2, pb), lambda l, i, j: (l, i, j, 0))
        out_shape = (layers, r // pb, c // 2, pb)
    else:
        out_spec = pl.BlockSpec((1, pb // 2, pb), lambda l, i, j: (l, i, j))
        out_shape = (layers, r // 2, c)
    return pl.pallas_call(
        functools.partial(_pack_kernel, transpose=transpose),
        grid=(layers, r // pb, c // pb),
        in_specs=[pl.BlockSpec((1, pb, pb), lambda l, i, j: (l, i, j))],
        out_specs=out_spec,
        out_shape=jax.ShapeDtypeStruct(out_shape, jnp.uint32),
        compiler_params=_cparams("parallel", "parallel", "parallel"),
        name="pack_t" if transpose else "pack",
    )(w)


def _peer_select_kernel(x_ref, mod_ref, g_ref, wq_ref, k1_ref, k2_ref,
                        h_ref, cut_ref, e1_ref, r2_ref, e2_ref, s1_scr, s2_scr):
    mod = mod_ref[0]
    f_t = _modulate(x_ref[...], g_ref[...], mod[3:4], mod[4:5]).T.astype(BF16)
    h_ref[...] = _pack_rows(f_t)
    tm = f_t.shape[1]
    n_heads, n_keys, half = k1_ref.shape
    q_t = _dot(_unpack_rows(wq_ref[0]), f_t)
    for hh in range(n_heads):
        base = hh * 2 * half
        s1_scr[hh] = _dot(k1_ref[hh], q_t[base:base + half].astype(BF16))
        s2_scr[hh] = _dot(k2_ref[hh], q_t[base + half:base + 2 * half].astype(BF16))
    n_sub = tm // LANES

    def step(idx, carry):
        hh = idx // (n_sub // SELECT_UNROLL)
        base = (idx % (n_sub // SELECT_UNROLL)) * SELECT_UNROLL
        bad_any = jnp.zeros((1, LANES), F32)
        work = []
        for sub in range(SELECT_UNROLL):
            lanes = pl.ds(pl.multiple_of((base + sub) * LANES, LANES), LANES)
            s1, s2 = s1_scr[hh, :, lanes], s2_scr[hh, :, lanes]
            *fast, bad = _peer_select(s1, s2, exact=False)
            work.append((lanes, s1, s2, fast))
            bad_any = jnp.maximum(bad_any, bad)

        def store(lanes, cut, e1, r2, e2):
            cut_ref[hh, :, lanes] = _pack_twice(cut)
            e1_ref[hh, :, lanes] = _pack_twice(e1)
            r2_ref[hh, :, lanes] = _pack_rows(r2)
            e2_ref[hh, :, lanes] = _pack_rows(e2)

        for lanes, _, _, fast in work:
            store(lanes, *fast)

        @pl.when(jnp.max(bad_any) > 0.0)
        def _():
            for lanes, s1, s2, _ in work:
                store(lanes, *_peer_select(s1, s2, exact=True)[:4])

        return carry

    lax.fori_loop(0, n_heads * n_sub // SELECT_UNROLL, step, 0)


def _peer_main_kernel(h_ref, u_ref, vt_ref, cut_ref, e1_ref, r2_ref, e2_ref, x_ref, mod_ref, fg_ref,
                      o_ref, acc, act0, act1, w0, w1, *, final_norm, n_chunks):
    g = pl.program_id(0)
    ec, tm = act0.shape
    n_heads, n_keys = r2_ref.shape[0], 2 * r2_ref.shape[1]
    rows_per_chunk = ec // n_keys
    half_keys = n_keys // 2
    packed_half = half_keys // 2

    @pl.when(g == 0)
    def _():
        acc[...] = jnp.zeros_like(acc)
        act1[...] = jnp.zeros_like(act1)
        w0[...] = jnp.zeros_like(w0)

    def gate_chunk(cut_ref, e1_ref, act_ref, w_ref, token_blocks):
        zero = jnp.zeros((half_keys, LANES), BF16)
        for part in range(n_keys // half_keys):
            prow = slice(part * packed_half, (part + 1) * packed_half)
            for tl in token_blocks:
                lanes = slice(tl * LANES, (tl + 1) * LANES)
                for first in range(0, rows_per_chunk, GATE_ROWS):
                    gates = [zero] * GATE_ROWS
                    for hh in range(n_heads):
                        r2 = _unpack_rows(r2_ref[hh, prow, lanes])
                        e2 = _unpack_rows(e2_ref[hh, prow, lanes])
                        for k in range(GATE_ROWS):
                            il = first + k
                            cut = _unpack_rows(jnp.broadcast_to(cut_ref[hh, il:il + 1, lanes], (packed_half, LANES)))
                            e1 = _unpack_rows(jnp.broadcast_to(e1_ref[hh, il:il + 1, lanes], (packed_half, LANES)))
                            gates[k] = gates[k] + jnp.where(r2 < cut, e2, zero) * e1
                    for k in range(GATE_ROWS):
                        il = first + k
                        rows = slice(il * n_keys + part * half_keys, il * n_keys + (part + 1) * half_keys)
                        w_ref[rows, lanes] = _gelu(act_ref[rows, lanes]).astype(BF16) * gates[k]

    def stage(act_new, act_old, w_new, w_old):
        n_blocks = tm // LANES
        gate_chunk(cut_ref, e1_ref, act_old, w_new, range(0, n_blocks // 2))
        act_new[...] = _dot(_unpack_rows(u_ref[0]), _unpack_rows(h_ref[...]))
        gate_chunk(cut_ref, e1_ref, act_old, w_new, range(n_blocks // 2, n_blocks))
        acc[...] += sum(_dot(_unpack_rows(vt_ref[0, k]), w_old[k * PACK_BLOCK:(k + 1) * PACK_BLOCK, :])
                        for k in range(ec // PACK_BLOCK))

    pl.when(g % 2 == 0)(lambda: stage(act0, act1, w1, w0))
    pl.when(g % 2 == 1)(lambda: stage(act1, act0, w0, w1))

    tile_done = (g >= 2) & ((g - 2) % n_chunks == n_chunks - 1)

    @pl.when(tile_done)
    def _():
        y = x_ref[...] + mod_ref[0][5:6] * acc[...].T
        if final_norm:
            y = _rms(y) * fg_ref[...]
        o_ref[...] = y

    @pl.when(tile_done)
    def _():
        acc[...] = jnp.zeros_like(acc)


def _peer_layer(st, xs, mods, gain, layer, wq_packed, k1, k2, u_packed, vt_packed, final_gain, with_ctx,
                final_norm):
    tm, d = PEER_TILE, st.d
    n_heads, n_keys, half = k1.shape
    n_exp = 2 * u_packed.shape[1]
    ec = PEER_EXPERT_CHUNK
    n_tiles = st.tiles(tm, with_ctx)
    n_rows = n_tiles * tm
    u32 = jnp.uint32
    h_spec = pl.BlockSpec((d // 2, tm), lambda t, *_: (0, t))
    key_spec = pl.BlockSpec((n_heads, n_keys, tm), lambda t, *_: (0, 0, t))
    key_shape = jax.ShapeDtypeStruct((n_heads, n_keys, n_rows), u32)
    pair_spec = pl.BlockSpec((n_heads, n_keys // 2, tm), lambda t, *_: (0, 0, t))
    pair_shape = jax.ShapeDtypeStruct((n_heads, n_keys // 2, n_rows), u32)
    h, cut, e1, r2, e2 = pl.pallas_call(
        _peer_select_kernel,
        grid=(n_tiles,),
        in_specs=[_tok_spec(tm, d), st.mod_spec(tm), _full_spec((1, d)),
                  pl.BlockSpec((1,) + wq_packed.shape[1:], lambda t: (layer, 0, 0)),
                  _full_spec(k1.shape), _full_spec(k2.shape)],
        out_specs=[h_spec, key_spec, key_spec, pair_spec, pair_spec],
        out_shape=[jax.ShapeDtypeStruct((d // 2, n_rows), u32), key_shape, key_shape, pair_shape, pair_shape],
        scratch_shapes=[pltpu.VMEM((n_heads, n_keys, tm), F32), pltpu.VMEM((n_heads, n_keys, tm), F32)],
        compiler_params=_cparams("parallel"),
        name="peer_select",
    )(xs, mods, gain.reshape(1, d), wq_packed, k1.astype(BF16), k2.astype(BF16))

    n_chunks = n_exp // ec
    n_items = n_tiles * n_chunks
    rows_per_chunk = ec // n_keys
    mod_row = st.mod_row(tm)

    def item(lag):
        def tile(g):
            return jnp.clip(g - lag, 0, n_items - 1) // n_chunks

        def chunk(g):
            return jnp.clip(g - lag, 0, n_items - 1) % n_chunks

        return tile, chunk

    (tile0, chunk0), (tile1, chunk1), (tile2, chunk2) = item(0), item(1), item(2)
    sub1_spec = pl.BlockSpec((n_heads, rows_per_chunk, tm), lambda g: (0, chunk1(g), tile1(g)))
    sub2_spec = pl.BlockSpec((n_heads, n_keys // 2, tm), lambda g: (0, 0, tile1(g)))
    return pl.pallas_call(
        functools.partial(_peer_main_kernel, final_norm=final_norm, n_chunks=n_chunks),
        grid=(n_items + 2,),
        in_specs=[
            pl.BlockSpec((d // 2, tm), lambda g: (0, tile0(g))),
            pl.BlockSpec((1, ec // 2, d), lambda g: (layer, chunk0(g), 0)),
            pl.BlockSpec((1, ec // PACK_BLOCK, d // 2, PACK_BLOCK), lambda g: (layer, chunk2(g), 0, 0)),
            sub1_spec, sub1_spec, sub2_spec, sub2_spec,
            pl.BlockSpec((tm, d), lambda g: (tile2(g), 0)),
            pl.BlockSpec((1, MOD_CHUNKS, d), lambda g: (mod_row(tile2(g)), 0, 0)),
            _full_spec((1, d)),
        ],
        out_specs=pl.BlockSpec((tm, d), lambda g: (tile2(g), 0)),
        out_shape=jax.ShapeDtypeStruct((n_rows, d), F32),
        scratch_shapes=[pltpu.VMEM((d, tm), F32),
                        pltpu.VMEM((ec, tm), F32), pltpu.VMEM((ec, tm), F32),
                        pltpu.VMEM((ec, tm), BF16), pltpu.VMEM((ec, tm), BF16)],
        compiler_params=_cparams("arbitrary"),
        name="peer_main",
    )(h, u_packed, vt_packed, cut, e1, r2, e2, xs, mods, final_gain.reshape(1, d))


def kernel(x, c, ctx, c_ctx, ada_w, ada_b, norm1_g, norm2_g, sgu_w_in, sgu_v_gain, sgu_w_s, sgu_b_s, sgu_w_out, pool_w_in, pool_w_grp, pool_scale, pool_w_out, attn_w_qkv, attn_q_gain, attn_k_gain, attn_w_o, peer_w_q, peer_k1, peer_k2, peer_u, peer_v, final_gain):
    bsz, seq, d = x.shape
    ctx_len = ctx.shape[1]
    depth = ada_w.shape[0]
    st = _Stream(bsz, seq, ctx_len, d)
    assert bsz + 1 <= MOD_ROWS and seq % PEER_TILE == 0 and (bsz * ctx_len) % PEER_TILE == 0
    assert seq % TOKEN_TILE == 0 and ctx_len % TOKEN_TILE == 0

    cond = jnp.concatenate([c, c_ctx[None, :], jnp.zeros((MOD_ROWS - bsz - 1, d), F32)], axis=0)
    mods = _ada_mods(cond, ada_w, ada_b)
    assert peer_w_q.shape[1] == PACK_BLOCK and PEER_EXPERT_CHUNK % PACK_BLOCK == 0
    wq_packed = _pack_weights(peer_w_q, transpose=True)[:, 0]
    u_packed = _pack_weights(peer_u, transpose=False)
    vt_packed = _pack_weights(peer_v, transpose=True)
    xs = jnp.concatenate([x.reshape(bsz * seq, d), ctx.reshape(bsz * ctx_len, d)], axis=0)

    for i in range(depth):
        kind, j = i % N_MIXERS, i // N_MIXERS
        with_ctx = i < depth - 1
        if kind == 0:
            xs = _sgu_layer(st, xs, mods[i], norm1_g[i], sgu_w_in[j], sgu_v_gain[j], sgu_w_s[j], sgu_b_s[j],
                            sgu_w_out[j], with_ctx)
        elif kind == 1:
            xs = _pool_layer(st, xs, mods[i], norm1_g[i], pool_w_in[j], pool_w_grp[j], pool_scale[j],
                             pool_w_out[j], with_ctx)
        else:
            xs = _attn_layer(st, xs, mods[i], norm1_g[i], attn_w_qkv[j], attn_q_gain[j], attn_k_gain[j],
                             attn_w_o[j], with_ctx)
        xs = _peer_layer(st, xs, mods[i], norm2_g[i], i, wq_packed, peer_k1[i], peer_k2[i], u_packed, vt_packed,
                         final_gain, with_ctx, final_norm=(i == depth - 1))
    return xs[:bsz * seq].reshape(bsz, seq, d)
```

```python
import functools
import math

import jax
import jax.numpy as jnp
from jax import lax
from jax.experimental import pallas as pl
from jax.experimental.pallas import tpu as pltpu

F32 = jnp.float32
BF16 = jnp.bfloat16

NORM_EPS = 1e-6
MOD_CHUNKS = 6
GRID_W = 64
SGU_CHUNK = 128
SGU_GROUPS = 8
POOL_WINDOWS = (2, 4, 8, 16)
HEAD_DIM = 128
N_KV_HEADS = 2
ROPE_THETA = 10000.0
PEER_HEADS = 8
PEER_KEYS = 128
PEER_TOPK = 16
N_MIXERS = 3

LANES = 128
MOD_ROWS = 16
VMEM_LIMIT = 56 * 1024 * 1024

TOKEN_TILE = 256
PEER_TILE = 512
PEER_EXPERT_CHUNK = 1024
GATE_ROWS = 8
SELECT_UNROLL = 4


def _cparams(*sem):
    return pltpu.CompilerParams(dimension_semantics=sem, vmem_limit_bytes=VMEM_LIMIT)


def _rms(x):
    return x * lax.rsqrt(jnp.mean(x * x, axis=-1, keepdims=True) + NORM_EPS)


def _modulate(x, gain, shift, scale):
    return _rms(x) * gain * (1.0 + scale) + shift


def _gelu(x):
    return 0.5 * x * (1.0 + lax.erf(x * (1.0 / math.sqrt(2.0))))


def _dot(a, b):
    return jnp.dot(a, b, preferred_element_type=F32)


def _dot_nt(a, b):
    return lax.dot_general(a, b, (((1,), (1,)), ((), ())), preferred_element_type=F32)


def _ada_kernel(c_ref, w_ref, b_ref, o_ref):
    c = c_ref[...]
    a = c * (1.0 / (1.0 + jnp.exp(-c)))
    o_ref[0] = _dot(a.astype(BF16), w_ref[0].astype(BF16)) + b_ref[0]


def _ada_mods(cond, ada_w, ada_b):
    depth, d, n = ada_w.shape
    tn = n // 4
    out = pl.pallas_call(
        _ada_kernel,
        grid=(depth, n // tn),
        in_specs=[
            pl.BlockSpec((MOD_ROWS, d), lambda l, j: (0, 0)),
            pl.BlockSpec((1, d, tn), lambda l, j: (l, 0, j)),
            pl.BlockSpec((1, 1, tn), lambda l, j: (l, 0, j)),
        ],
        out_specs=pl.BlockSpec((1, MOD_ROWS, tn), lambda l, j: (l, 0, j)),
        out_shape=jax.ShapeDtypeStruct((depth, MOD_ROWS, n), F32),
        compiler_params=_cparams("parallel", "parallel"),
        name="ada_mods",
    )(cond, ada_w, ada_b.reshape(depth, 1, n))
    return out.reshape(depth, MOD_ROWS, MOD_CHUNKS, d)


class _Stream:
    def __init__(self, bsz, seq, ctx_len, d):
        self.bsz, self.seq, self.ctx_len, self.d = bsz, seq, ctx_len, d
        self.n_lat = bsz * seq
        self.n_tok = bsz * (seq + ctx_len)

    def tiles(self, tm, with_ctx):
        return (self.n_tok if with_ctx else self.n_lat) // tm

    def mod_row(self, tm):
        n_lat_tiles, per_seq, bsz = self.n_lat // tm, self.seq // tm, self.bsz
        return lambda t: jnp.where(t < n_lat_tiles, t // per_seq, bsz)

    def mod_spec(self, tm):
        row = self.mod_row(tm)
        return pl.BlockSpec((1, MOD_CHUNKS, self.d), lambda t, *_: (row(t), 0, 0))


def _tok_spec(tm, width):
    return pl.BlockSpec((tm, width), lambda t, *_: (t, 0))


def _full_spec(shape):
    zeros = (0,) * len(shape)
    return pl.BlockSpec(shape, lambda *_: zeros)


def _sgu_kernel(x_ref, mod_ref, g_ref, win_ref, vg_ref, ws_ref, bs_ref, wout_ref, o_ref):
    x = x_ref[...]
    mod = mod_ref[0]
    tm = x.shape[0]
    width = vg_ref.shape[1]
    gdim = width // SGU_GROUPS
    h = _modulate(x, g_ref[...], mod[0:1], mod[1:2]).astype(BF16)
    u = _gelu(_dot(h, win_ref[:, :width]))
    v = _gelu(_dot(h, win_ref[:, width:]))
    vn = (_rms(v) * vg_ref[...]).astype(BF16)
    rows = []
    for c in range(tm // SGU_CHUNK):
        cols = []
        for g in range(SGU_GROUPS):
            vv = vn[c * SGU_CHUNK:(c + 1) * SGU_CHUNK, g * gdim:(g + 1) * gdim]
            cols.append(_dot(ws_ref[g], vv) + bs_ref[:, g:g + 1])
        rows.append(jnp.concatenate(cols, axis=1))
    sv = jnp.concatenate(rows, axis=0)
    y = _dot((u * sv).astype(BF16), wout_ref[...])
    o_ref[...] = x + mod[2:3] * y


def _sgu_layer(st, xs, mods, gain, w_in, v_gain, w_s, b_s, w_out, with_ctx):
    tm, d = TOKEN_TILE, st.d
    width = v_gain.shape[0]
    return pl.pallas_call(
        _sgu_kernel,
        grid=(st.tiles(tm, with_ctx),),
        in_specs=[
            _tok_spec(tm, d), st.mod_spec(tm), _full_spec((1, d)),
            _full_spec((d, 2 * width)), _full_spec((1, width)),
            _full_spec((SGU_GROUPS, SGU_CHUNK, SGU_CHUNK)), _full_spec((SGU_CHUNK, SGU_GROUPS)),
            _full_spec((width, d)),
        ],
        out_specs=_tok_spec(tm, d),
        out_shape=jax.ShapeDtypeStruct((st.tiles(tm, with_ctx) * tm, d), F32),
        compiler_params=_cparams("parallel"),
        name="sgu_mixer",
    )(xs, mods, gain.reshape(1, d), w_in.astype(BF16), v_gain.reshape(1, width),
      w_s.astype(BF16), b_s.T, w_out.astype(BF16))


POOL_HALO = 8


def _pool_in_kernel(x_ref, mod_ref, g_ref, win_ref, z_ref):
    mod = mod_ref[0]
    h = _modulate(x_ref[...], g_ref[...], mod[0:1], mod[1:2]).astype(BF16)
    z_ref[...] = _dot(h, win_ref[...])


def _pool_out_kernel(z_ref, zp_ref, zn_ref, x_ref, mod_ref, wg_ref, sc_ref, wout_ref, o_ref,
                     *, n_lat_tiles, lat_tiles_per_seq, ctx_tiles_per_seq, seq, ctx_len):
    t = pl.program_id(0)
    tm = z_ref.shape[0]
    is_lat = t < n_lat_tiles
    per_seq = jnp.where(is_lat, lat_tiles_per_seq, ctx_tiles_per_seq)
    pos_tile = jnp.where(is_lat, t, t - n_lat_tiles) % per_seq
    length = jnp.where(is_lat, seq, ctx_len)
    has_prev = (pos_tile > 0).astype(F32)
    has_next = (pos_tile < per_seq - 1).astype(F32)
    z = z_ref[...]
    zext = jnp.concatenate([zp_ref[...] * has_prev, z, zn_ref[...] * has_next], axis=0)
    pos = (pos_tile * tm + lax.broadcasted_iota(jnp.int32, (tm, 1), 0))
    gdim = wg_ref.shape[1]
    n_ext = tm + 2 * POOL_HALO

    def shift_up(a, k):
        return a if k == 0 else pltpu.roll(a, n_ext - k, 0)

    outs = []
    for g, w in enumerate(POOL_WINDOWS):
        half = w // 2
        acc = zext[:, g * gdim:(g + 1) * gdim]
        span = 1
        while span < w:
            acc = acc + shift_up(acc, span)
            span *= 2
        win_sum = shift_up(acc, POOL_HALO - half)[:tm]
        cnt = (jnp.minimum(pos + half, length) - jnp.maximum(pos - half, 0)).astype(F32)
        pooled = win_sum / cnt - z[:, g * gdim:(g + 1) * gdim]
        outs.append(_dot(pooled.astype(BF16), wg_ref[g]))
    y = (jnp.concatenate(outs, axis=1) * sc_ref[...]).astype(BF16)
    o_ref[...] = x_ref[...] + mod_ref[0][2:3] * _dot(y, wout_ref[...])


def _pool_layer(st, xs, mods, gain, w_in, w_grp, scale, w_out, with_ctx):
    tm, d = TOKEN_TILE, st.d
    width = w_in.shape[1]
    n_tiles = st.tiles(tm, with_ctx)
    z = pl.pallas_call(
        _pool_in_kernel,
        grid=(n_tiles,),
        in_specs=[_tok_spec(tm, d), st.mod_spec(tm), _full_spec((1, d)), _full_spec((d, width))],
        out_specs=_tok_spec(tm, width),
        out_shape=jax.ShapeDtypeStruct((n_tiles * tm, width), F32),
        compiler_params=_cparams("parallel"),
        name="pool_in",
    )(xs, mods, gain.reshape(1, d), w_in.astype(BF16))
    halo_per_tile = tm // POOL_HALO
    last_halo = n_tiles * halo_per_tile - 1
    kern = functools.partial(
        _pool_out_kernel, n_lat_tiles=st.n_lat // tm, lat_tiles_per_seq=st.seq // tm,
        ctx_tiles_per_seq=st.ctx_len // tm, seq=st.seq, ctx_len=st.ctx_len)
    return pl.pallas_call(
        kern,
        grid=(n_tiles,),
        in_specs=[
            _tok_spec(tm, width),
            pl.BlockSpec((POOL_HALO, width), lambda t: (jnp.maximum(t * halo_per_tile - 1, 0), 0)),
            pl.BlockSpec((POOL_HALO, width), lambda t: (jnp.minimum((t + 1) * halo_per_tile, last_halo), 0)),
            _tok_spec(tm, d), st.mod_spec(tm),
            _full_spec(w_grp.shape), _full_spec((1, width)), _full_spec((width, d)),
        ],
        out_specs=_tok_spec(tm, d),
        out_shape=jax.ShapeDtypeStruct((st.tiles(tm, with_ctx) * tm, d), F32),
        compiler_params=_cparams("parallel"),
        name="pool_out",
    )(z, z, z, xs, mods, w_grp.astype(BF16), scale.reshape(1, width), w_out.astype(BF16))


def _rope_tables(seq, tm):
    pos = jnp.arange(seq)
    axis_dim = HEAD_DIM // 2
    freqs = ROPE_THETA ** (-jnp.arange(0, axis_dim, 2, dtype=F32) / axis_dim)
    ang = jnp.stack([pos // GRID_W, pos % GRID_W], axis=-1).astype(F32)[:, :, None] * freqs
    cos, sin = jnp.cos(ang), jnp.sin(ang)
    zero = jnp.zeros_like(sin)
    c = jnp.concatenate([cos, cos], axis=-1).reshape(seq, HEAD_DIM)
    s_lo = jnp.concatenate([-sin, zero], axis=-1).reshape(seq, HEAD_DIM)
    s_hi = jnp.concatenate([zero, sin], axis=-1).reshape(seq, HEAD_DIM)
    pad = jnp.zeros((tm, HEAD_DIM), F32)
    return (jnp.concatenate([c, pad + 1.0]), jnp.concatenate([s_lo, pad]), jnp.concatenate([s_hi, pad]))


def _qkv_kernel(x_ref, mod_ref, g_ref, w_ref, qg_ref, kg_ref, c_ref, slo_ref, shi_ref, q_ref, k_ref, v_ref):
    mod = mod_ref[0]
    h = _modulate(x_ref[...], g_ref[...], mod[0:1], mod[1:2]).astype(BF16)
    qkv = _dot(h, w_ref[...])
    nq, nk = q_ref.shape[1], k_ref.shape[1]
    cos, s_lo, s_hi = c_ref[...], slo_ref[...], shi_ref[...]
    quarter = HEAD_DIM // 4

    def norm_rope(t, gain):
        t = _rms(t) * gain
        return (t * cos + pltpu.roll(t, HEAD_DIM - quarter, 1) * s_lo + pltpu.roll(t, quarter, 1) * s_hi)

    for hh in range(nq // HEAD_DIM):
        sl = slice(hh * HEAD_DIM, (hh + 1) * HEAD_DIM)
        q_ref[:, sl] = norm_rope(qkv[:, sl], qg_ref[...]).astype(BF16)
    for hh in range(nk // HEAD_DIM):
        sl = slice(hh * HEAD_DIM, (hh + 1) * HEAD_DIM)
        k_ref[:, sl] = norm_rope(qkv[:, nq + hh * HEAD_DIM:nq + (hh + 1) * HEAD_DIM], kg_ref[...]).astype(BF16)
    v_ref[...] = qkv[:, nq + nk:].astype(BF16)


def _attn_kernel(q_ref, kl_ref, vl_ref, kc_ref, vc_ref, x_ref, mod_ref, wo_ref, o_ref, *, n_lat_tiles):
    n_heads = q_ref.shape[1] // HEAD_DIM
    q_per_kv = n_heads // N_KV_HEADS
    scale = HEAD_DIM ** -0.5

    def body(with_lat):
        def kv_of(hh):
            return slice((hh // q_per_kv) * HEAD_DIM, (hh // q_per_kv + 1) * HEAD_DIM)

        def scores(hh):
            q = q_ref[:, hh * HEAD_DIM:(hh + 1) * HEAD_DIM]
            s_c = _dot_nt(q, kc_ref[:, kv_of(hh)]) * scale
            s_l = _dot_nt(q, kl_ref[:, kv_of(hh)]) * scale if with_lat else None
            return s_c, s_l

        def attend(hh, s_c, s_l):
            m = jnp.max(s_c, axis=-1, keepdims=True)
            if with_lat:
                m = jnp.maximum(m, jnp.max(s_l, axis=-1, keepdims=True))
            p_c = jnp.exp(s_c - m)
            den = jnp.sum(p_c, axis=-1, keepdims=True)
            o = _dot(p_c.astype(BF16), vc_ref[:, kv_of(hh)])
            if with_lat:
                p_l = jnp.exp(s_l - m)
                den = den + jnp.sum(p_l, axis=-1, keepdims=True)
                o = o + _dot(p_l.astype(BF16), vl_ref[:, kv_of(hh)])
            return o / den

        outs = []
        pending = scores(0)
        for hh in range(n_heads):
            upcoming = scores(hh + 1) if hh + 1 < n_heads else None
            outs.append(attend(hh, *pending))
            pending = upcoming
        y = _dot(jnp.concatenate(outs, axis=1).astype(BF16), wo_ref[...])
        o_ref[...] = x_ref[...] + mod_ref[0][2:3] * y

    t = pl.program_id(0)
    pl.when(t < n_lat_tiles)(lambda: body(True))
    pl.when(t >= n_lat_tiles)(lambda: body(False))


def _attn_layer(st, xs, mods, gain, w_qkv, q_gain, k_gain, w_o, with_ctx):
    d = st.d
    tm = st.ctx_len
    assert st.seq % tm == 0
    n_all = st.n_tok // tm
    n_lat_tiles, per_seq = st.n_lat // tm, st.seq // tm
    nq = w_o.shape[0]
    nk = (w_qkv.shape[1] - nq) // 2
    cos, s_lo, s_hi = _rope_tables(st.seq, tm)
    rope_spec = pl.BlockSpec((tm, HEAD_DIM), lambda t: (jnp.where(t < n_lat_tiles, t % per_seq, per_seq), 0))
    q, k, v = pl.pallas_call(
        _qkv_kernel,
        grid=(n_all,),
        in_specs=[
            _tok_spec(tm, d), st.mod_spec(tm), _full_spec((1, d)), _full_spec(w_qkv.shape),
            _full_spec((1, HEAD_DIM)), _full_spec((1, HEAD_DIM)), rope_spec, rope_spec, rope_spec,
        ],
        out_specs=[_tok_spec(tm, nq), _tok_spec(tm, nk), _tok_spec(tm, nk)],
        out_shape=[jax.ShapeDtypeStruct((st.n_tok, nq), BF16), jax.ShapeDtypeStruct((st.n_tok, nk), BF16),
                   jax.ShapeDtypeStruct((st.n_tok, nk), BF16)],
        compiler_params=_cparams("parallel"),
        name="attn_qkv",
    )(xs, mods, gain.reshape(1, d), w_qkv.astype(BF16), q_gain.reshape(1, HEAD_DIM),
      k_gain.reshape(1, HEAD_DIM), cos, s_lo, s_hi)

    bsz = st.bsz
    batch_of = lambda t: jnp.where(t < n_lat_tiles, t // per_seq, t - n_lat_tiles)
    lat_spec = pl.BlockSpec((st.seq, nk), lambda t: (jnp.minimum(t // per_seq, bsz - 1), 0))
    ctx_spec = pl.BlockSpec((tm, nk), lambda t: (n_lat_tiles + batch_of(t), 0))
    return pl.pallas_call(
        functools.partial(_attn_kernel, n_lat_tiles=n_lat_tiles),
        grid=(st.tiles(tm, with_ctx),),
        in_specs=[_tok_spec(tm, nq), lat_spec, lat_spec, ctx_spec, ctx_spec,
                  _tok_spec(tm, d), st.mod_spec(tm), _full_spec((nq, d))],
        out_specs=_tok_spec(tm, d),
        out_shape=jax.ShapeDtypeStruct((st.tiles(tm, with_ctx) * tm, d), F32),
        compiler_params=_cparams("parallel"),
        name="attn_core",
    )(q, k, v, k, v, xs, mods, w_o.astype(BF16))


_INT_MIN = -2 ** 31


def _sortable(x):
    b = lax.bitcast_convert_type(x, jnp.int32)
    return b ^ ((b >> 31) & 0x7FFFFFFF)


def _unsortable(k):
    return lax.bitcast_convert_type(k ^ ((k >> 31) & 0x7FFFFFFF), F32)


def _extract16(k, iota, exact):
    n = k.shape[0]
    tops = []
    for a in range(PEER_TOPK):
        m = jnp.max(k, axis=0, keepdims=True)
        hit = k == m
        if exact:
            first = jnp.min(jnp.where(hit, iota, n), axis=0, keepdims=True)
            hit = iota == first
        k = jnp.where(hit, _INT_MIN + a, k)
        tops.append(m)
    return k, tops


def _rows16(tops, iota16):
    out = jnp.zeros(iota16.shape, tops[0].dtype)
    for a, t in enumerate(tops):
        out = jnp.where(iota16 == a, t, out)
    return out


_CAND_HALF = PEER_TOPK // 2


def _peer_select(s1, s2, exact):
    n, width = s1.shape
    topk = float(PEER_TOPK)
    iota = lax.broadcasted_iota(jnp.int32, (n, width), 0)
    iota16 = iota[0:PEER_TOPK]
    mark_end = _INT_MIN + PEER_TOPK
    k1, top1 = _extract16(_sortable(s1), iota, exact)
    k2, top2 = _extract16(_sortable(s2), iota, exact)
    v1 = _unsortable(_rows16(top1, iota16))
    v2 = _unsortable(_rows16(top2, iota16))
    f1 = [_unsortable(t) for t in top1[:_CAND_HALF]]
    f2_0 = _unsortable(top2[0])
    pieces = [v2 + f1[0]]
    pieces += [v2[0:_CAND_HALF] + f1[a] for a in range(1, _CAND_HALF)]
    pieces.append(v1[_CAND_HALF:] + f2_0)
    cand = jnp.concatenate(pieces, axis=0)
    nc = cand.shape[0]
    kc, _ = _extract16(_sortable(cand), iota[0:nc], exact)
    picked = kc < mark_end
    if not exact:
        surplus = jnp.sum(jnp.where(picked, 1.0, 0.0), axis=0, keepdims=True) - topk
        picked = kc < mark_end - surplus.astype(jnp.int32)
    sel = jnp.where(picked, 1.0, 0.0)
    z = jnp.sum(jnp.where(picked, jnp.exp(cand - (f1[0] + f2_0)), 0.0), axis=0, keepdims=True)
    cnt = [jnp.sum(sel[0:PEER_TOPK], axis=0, keepdims=True)]
    for a in range(1, _CAND_HALF):
        row = PEER_TOPK + (a - 1) * _CAND_HALF
        cnt.append(jnp.sum(sel[row:row + _CAND_HALF], axis=0, keepdims=True))
    tail = sel[nc - _CAND_HALF:]
    for a in range(_CAND_HALF, PEER_TOPK):
        pick = iota[0:_CAND_HALF] == a - _CAND_HALF
        cnt.append(jnp.sum(jnp.where(pick, tail, 0.0), axis=0, keepdims=True))
    cut = jnp.zeros(s1.shape, F32)
    for a in range(PEER_TOPK):
        cut = jnp.where(k1 == _INT_MIN + a, cnt[a], cut)
    in1, in2 = k1 < mark_end, k2 < mark_end
    r2 = jnp.where(in2, (k2 - _INT_MIN).astype(F32), topk)
    e1 = jnp.exp(s1 - f1[0])
    e2 = jnp.exp(s2 - f2_0) / z
    n1 = jnp.sum(jnp.where(in1, 1.0, 0.0), axis=0, keepdims=True)
    n2 = jnp.sum(jnp.where(in2, 1.0, 0.0), axis=0, keepdims=True)
    nsel = jnp.sum(sel, axis=0, keepdims=True)
    bad = jnp.where((n1 != topk) | (n2 != topk) | (nsel != topk), 1.0, 0.0)
    return cut, e1, r2, e2, bad


def _pack_rows(x):
    return pltpu.bitcast(x.astype(BF16), jnp.uint32)


def _unpack_rows(x):
    return pltpu.bitcast(x, BF16)


def _pack_twice(x):
    u = lax.bitcast_convert_type(x.astype(BF16).astype(F32), jnp.uint32)
    return u | (u >> 16)


PACK_BLOCK = 1024


def _pack_kernel(w_ref, o_ref, *, transpose):
    if transpose:
        o_ref[0, 0] = _pack_rows(w_ref[0].T)
    else:
        o_ref[0] = _pack_rows(w_ref[0])


def _pack_weights(w, transpose):
    layers, r, c = w.shape
    pb = PACK_BLOCK
    if transpose:
        out_spec = pl.BlockSpec((1, 1, pb // 2, pb), lambda l, i, j: (l, i, j, 0))
        out_shape = (layers, r // pb, c // 2, pb)
    else:
        out_spec = pl.BlockSpec((1, pb // 2, pb), lambda l, i, j: (l, i, j))
        out_shape = (layers, r // 2, c)
    return pl.pallas_call(
        functools.partial(_pack_kernel, transpose=transpose),
        grid=(layers, r // pb, c // pb),
        in_specs=[pl.BlockSpec((1, pb, pb), lambda l, i, j: (l, i, j))],
        out_specs=out_spec,
        out_shape=jax.ShapeDtypeStruct(out_shape, jnp.uint32),
        compiler_params=_cparams("parallel", "parallel", "parallel"),
        name="pack_t" if transpose else "pack",
    )(w)


def _peer_select_kernel(x_ref, mod_ref, g_ref, wq_ref, k1_ref, k2_ref,
                        h_ref, cut_ref, e1_ref, r2_ref, e2_ref, s1_scr, s2_scr):
    mod = mod_ref[0]
    f_t = _modulate(x_ref[...], g_ref[...], mod[3:4], mod[4:5]).T.astype(BF16)
    h_ref[...] = _pack_rows(f_t)
    tm = f_t.shape[1]
    n_heads, n_keys, half = k1_ref.shape
    q_t = _dot(_unpack_rows(wq_ref[0]), f_t)
    for hh in range(n_heads):
        base = hh * 2 * half
        s1_scr[hh] = _dot(k1_ref[hh], q_t[base:base + half].astype(BF16))
        s2_scr[hh] = _dot(k2_ref[hh], q_t[base + half:base + 2 * half].astype(BF16))
    n_sub = tm // LANES

    def step(idx, carry):
        hh = idx // (n_sub // SELECT_UNROLL)
        base = (idx % (n_sub // SELECT_UNROLL)) * SELECT_UNROLL
        bad_any = jnp.zeros((1, LANES), F32)
        work = []
        for sub in range(SELECT_UNROLL):
            lanes = pl.ds(pl.multiple_of((base + sub) * LANES, LANES), LANES)
            s1, s2 = s1_scr[hh, :, lanes], s2_scr[hh, :, lanes]
            *fast, bad = _peer_select(s1, s2, exact=False)
            work.append((lanes, s1, s2, fast))
            bad_any = jnp.maximum(bad_any, bad)

        def store(lanes, cut, e1, r2, e2):
            cut_ref[hh, :, lanes] = _pack_twice(cut)
            e1_ref[hh, :, lanes] = _pack_twice(e1)
            r2_ref[hh, :, lanes] = _pack_rows(r2)
            e2_ref[hh, :, lanes] = _pack_rows(e2)

        for lanes, _, _, fast in work:
            store(lanes, *fast)

        @pl.when(jnp.max(bad_any) > 0.0)
        def _():
            for lanes, s1, s2, _ in work:
                store(lanes, *_peer_select(s1, s2, exact=True)[:4])

        return carry

    lax.fori_loop(0, n_heads * n_sub // SELECT_UNROLL, step, 0)


def _peer_main_kernel(h_ref, u_ref, vt_ref, cut_ref, e1_ref, r2_ref, e2_ref, x_ref, mod_ref, fg_ref,
                      o_ref, acc, act0, act1, w0, w1, *, final_norm, n_chunks):
    g = pl.program_id(0)
    ec, tm = act0.shape
    n_heads, n_keys = r2_ref.shape[0], 2 * r2_ref.shape[1]
    rows_per_chunk = ec // n_keys
    half_keys = n_keys // 2
    packed_half = half_keys // 2

    @pl.when(g == 0)
    def _():
        acc[...] = jnp.zeros_like(acc)
        act1[...] = jnp.zeros_like(act1)
        w0[...] = jnp.zeros_like(w0)

    def gate_chunk(cut_ref, e1_ref, act_ref, w_ref, token_blocks):
        zero = jnp.zeros((half_keys, LANES), BF16)
        for part in range(n_keys // half_keys):
            prow = slice(part * packed_half, (part + 1) * packed_half)
            for tl in token_blocks:
                lanes = slice(tl * LANES, (tl + 1) * LANES)
                for first in range(0, rows_per_chunk, GATE_ROWS):
                    gates = [zero] * GATE_ROWS
                    for hh in range(n_heads):
                        r2 = _unpack_rows(r2_ref[hh, prow, lanes])
                        e2 = _unpack_rows(e2_ref[hh, prow, lanes])
                        for k in range(GATE_ROWS):
                            il = first + k
                            cut = _unpack_rows(jnp.broadcast_to(cut_ref[hh, il:il + 1, lanes], (packed_half, LANES)))
                            e1 = _unpack_rows(jnp.broadcast_to(e1_ref[hh, il:il + 1, lanes], (packed_half, LANES)))
                            gates[k] = gates[k] + jnp.where(r2 < cut, e2, zero) * e1
                    for k in range(GATE_ROWS):
                        il = first + k
                        rows = slice(il * n_keys + part * half_keys, il * n_keys + (part + 1) * half_keys)
                        w_ref[rows, lanes] = _gelu(act_ref[rows, lanes]).astype(BF16) * gates[k]

    def stage(act_new, act_old, w_new, w_old):
        n_blocks = tm // LANES
        gate_chunk(cut_ref, e1_ref, act_old, w_new, range(0, n_blocks // 2))
        act_new[...] = _dot(_unpack_rows(u_ref[0]), _unpack_rows(h_ref[...]))
        gate_chunk(cut_ref, e1_ref, act_old, w_new, range(n_blocks // 2, n_blocks))
        acc[...] += sum(_dot(_unpack_rows(vt_ref[0, k]), w_old[k * PACK_BLOCK:(k + 1) * PACK_BLOCK, :])
                        for k in range(ec // PACK_BLOCK))

    pl.when(g % 2 == 0)(lambda: stage(act0, act1, w1, w0))
    pl.when(g % 2 == 1)(lambda: stage(act1, act0, w0, w1))

    tile_done = (g >= 2) & ((g - 2) % n_chunks == n_chunks - 1)

    @pl.when(tile_done)
    def _():
        y = x_ref[...] + mod_ref[0][5:6] * acc[...].T
        if final_norm:
            y = _rms(y) * fg_ref[...]
        o_ref[...] = y

    @pl.when(tile_done)
    def _():
        acc[...] = jnp.zeros_like(acc)


def _peer_layer(st, xs, mods, gain, layer, wq_packed, k1, k2, u_packed, vt_packed, final_gain, with_ctx,
                final_norm):
    tm, d = PEER_TILE, st.d
    n_heads, n_keys, half = k1.shape
    n_exp = 2 * u_packed.shape[1]
    ec = PEER_EXPERT_CHUNK
    n_tiles = st.tiles(tm, with_ctx)
    n_rows = n_tiles * tm
    u32 = jnp.uint32
    h_spec = pl.BlockSpec((d // 2, tm), lambda t, *_: (0, t))
    key_spec = pl.BlockSpec((n_heads, n_keys, tm), lambda t, *_: (0, 0, t))
    key_shape = jax.ShapeDtypeStruct((n_heads, n_keys, n_rows), u32)
    pair_spec = pl.BlockSpec((n_heads, n_keys // 2, tm), lambda t, *_: (0, 0, t))
    pair_shape = jax.ShapeDtypeStruct((n_heads, n_keys // 2, n_rows), u32)
    h, cut, e1, r2, e2 = pl.pallas_call(
        _peer_select_kernel,
        grid=(n_tiles,),
        in_specs=[_tok_spec(tm, d), st.mod_spec(tm), _full_spec((1, d)),
                  pl.BlockSpec((1,) + wq_packed.shape[1:], lambda t: (layer, 0, 0)),
                  _full_spec(k1.shape), _full_spec(k2.shape)],
        out_specs=[h_spec, key_spec, key_spec, pair_spec, pair_spec],
        out_shape=[jax.ShapeDtypeStruct((d // 2, n_rows), u32), key_shape, key_shape, pair_shape, pair_shape],
        scratch_shapes=[pltpu.VMEM((n_heads, n_keys, tm), F32), pltpu.VMEM((n_heads, n_keys, tm), F32)],
        compiler_params=_cparams("parallel"),
        name="peer_select",
    )(xs, mods, gain.reshape(1, d), wq_packed, k1.astype(BF16), k2.astype(BF16))

    n_chunks = n_exp // ec
    n_items = n_tiles * n_chunks
    rows_per_chunk = ec // n_keys
    mod_row = st.mod_row(tm)

    def item(lag):
        def tile(g):
            return jnp.clip(g - lag, 0, n_items - 1) // n_chunks

        def chunk(g):
            return jnp.clip(g - lag, 0, n_items - 1) % n_chunks

        return tile, chunk

    (tile0, chunk0), (tile1, chunk1), (tile2, chunk2) = item(0), item(1), item(2)
    sub1_spec = pl.BlockSpec((n_heads, rows_per_chunk, tm), lambda g: (0, chunk1(g), tile1(g)))
    sub2_spec = pl.BlockSpec((n_heads, n_keys // 2, tm), lambda g: (0, 0, tile1(g)))
    return pl.pallas_call(
        functools.partial(_peer_main_kernel, final_norm=final_norm, n_chunks=n_chunks),
        grid=(n_items + 2,),
        in_specs=[
            pl.BlockSpec((d // 2, tm), lambda g: (0, tile0(g))),
            pl.BlockSpec((1, ec // 2, d), lambda g: (layer, chunk0(g), 0)),
            pl.BlockSpec((1, ec // PACK_BLOCK, d // 2, PACK_BLOCK), lambda g: (layer, chunk2(g), 0, 0)),
            sub1_spec, sub1_spec, sub2_spec, sub2_spec,
            pl.BlockSpec((tm, d), lambda g: (tile2(g), 0)),
            pl.BlockSpec((1, MOD_CHUNKS, d), lambda g: (mod_row(tile2(g)), 0, 0)),
            _full_spec((1, d)),
        ],
        out_specs=pl.BlockSpec((tm, d), lambda g: (tile2(g), 0)),
        out_shape=jax.ShapeDtypeStruct((n_rows, d), F32),
        scratch_shapes=[pltpu.VMEM((d, tm), F32),
                        pltpu.VMEM((ec, tm), F32), pltpu.VMEM((ec, tm), F32),
                        pltpu.VMEM((ec, tm), BF16), pltpu.VMEM((ec, tm), BF16)],
        compiler_params=_cparams("arbitrary"),
        name="peer_main",
    )(h, u_packed, vt_packed, cut, e1, r2, e2, xs, mods, final_gain.reshape(1, d))


def kernel(x, c, ctx, c_ctx, ada_w, ada_b, norm1_g, norm2_g, sgu_w_in, sgu_v_gain, sgu_w_s, sgu_b_s, sgu_w_out, pool_w_in, pool_w_grp, pool_scale, pool_w_out, attn_w_qkv, attn_q_gain, attn_k_gain, attn_w_o, peer_w_q, peer_k1, peer_k2, peer_u, peer_v, final_gain):
    bsz, seq, d = x.shape
    ctx_len = ctx.shape[1]
    depth = ada_w.shape[0]
    st = _Stream(bsz, seq, ctx_len, d)
    assert bsz + 1 <= MOD_ROWS and seq % PEER_TILE == 0 and (bsz * ctx_len) % PEER_TILE == 0
    assert seq % TOKEN_TILE == 0 and ctx_len % TOKEN_TILE == 0

    cond = jnp.concatenate([c, c_ctx[None, :], jnp.zeros((MOD_ROWS - bsz - 1, d), F32)], axis=0)
    mods = _ada_mods(cond, ada_w, ada_b)
    assert peer_w_q.shape[1] == PACK_BLOCK and PEER_EXPERT_CHUNK % PACK_BLOCK == 0
    wq_packed = _pack_weights(peer_w_q, transpose=True)[:, 0]
    u_packed = _pack_weights(peer_u, transpose=False)
    vt_packed = _pack_weights(peer_v, transpose=True)
    xs = jnp.concatenate([x.reshape(bsz * seq, d), ctx.reshape(bsz * ctx_len, d)], axis=0)

    for i in range(depth):
        kind, j = i % N_MIXERS, i // N_MIXERS
        with_ctx = i < depth - 1
        if kind == 0:
            xs = _sgu_layer(st, xs, mods[i], norm1_g[i], sgu_w_in[j], sgu_v_gain[j], sgu_w_s[j], sgu_b_s[j],
                            sgu_w_out[j], with_ctx)
        elif kind == 1:
            xs = _pool_layer(st, xs, mods[i], norm1_g[i], pool_w_in[j], pool_w_grp[j], pool_scale[j],
                             pool_w_out[j], with_ctx)
        else:
            xs = _attn_layer(st, xs, mods[i], norm1_g[i], attn_w_qkv[j], attn_q_gain[j], attn_k_gain[j],
                             attn_w_o[j], with_ctx)
        xs = _peer_layer(st, xs, mods[i], norm2_g[i], i, wq_packed, peer_k1[i], peer_k2[i], u_packed, vt_packed,
                         final_gain, with_ctx, final_norm=(i == depth - 1))
    return xs[:bsz * seq].reshape(bsz, seq, d)
```

```python
import functools
import math

import jax
import jax.numpy as jnp
from jax import lax
from jax.experimental import pallas as pl
from jax.experimental.pallas import tpu as pltpu

F32 = jnp.float32
BF16 = jnp.bfloat16

NORM_EPS = 1e-6
MOD_CHUNKS = 6
GRID_W = 64
SGU_CHUNK = 128
SGU_GROUPS = 8
POOL_WINDOWS = (2, 4, 8, 16)
HEAD_DIM = 128
N_KV_HEADS = 2
ROPE_THETA = 10000.0
PEER_TOPK = 16
N_MIXERS = 3

LANES = 128
MOD_ROWS = 16
VMEM_LIMIT = 56 * 1024 * 1024

TOKEN_TILE = 256
PEER_TILE = 512
PEER_EXPERT_CHUNK = 1024
GATE_ROWS = 8
SELECT_UNROLL = 4


def _cparams(*sem):
    return pltpu.CompilerParams(dimension_semantics=sem, vmem_limit_bytes=VMEM_LIMIT)


def _rms(x):
    return x * lax.rsqrt(jnp.mean(x * x, axis=-1, keepdims=True) + NORM_EPS)


def _modulate(x, gain, shift, scale):
    return _rms(x) * gain * (1.0 + scale) + shift


def _gelu(x):
    return 0.5 * x * (1.0 + lax.erf(x * (1.0 / math.sqrt(2.0))))


def _dot(a, b):
    return jnp.dot(a, b, preferred_element_type=F32)


def _dot_nt(a, b):
    return lax.dot_general(a, b, (((1,), (1,)), ((), ())), preferred_element_type=F32)


def _ada_kernel(c_ref, w_ref, b_ref, o_ref):
    c = c_ref[...]
    a = c * (1.0 / (1.0 + jnp.exp(-c)))
    o_ref[0] = _dot(a.astype(BF16), w_ref[0].astype(BF16)) + b_ref[0]


def _ada_mods(cond, ada_w, ada_b):
    depth, d, n = ada_w.shape
    tn = n // 4
    out = pl.pallas_call(
        _ada_kernel,
        grid=(depth, n // tn),
        in_specs=[
            pl.BlockSpec((MOD_ROWS, d), lambda l, j: (0, 0)),
            pl.BlockSpec((1, d, tn), lambda l, j: (l, 0, j)),
            pl.BlockSpec((1, 1, tn), lambda l, j: (l, 0, j)),
        ],
        out_specs=pl.BlockSpec((1, MOD_ROWS, tn), lambda l, j: (l, 0, j)),
        out_shape=jax.ShapeDtypeStruct((depth, MOD_ROWS, n), F32),
        compiler_params=_cparams("parallel", "parallel"),
        name="ada_mods",
    )(cond, ada_w, ada_b.reshape(depth, 1, n))
    return out.reshape(depth, MOD_ROWS, MOD_CHUNKS, d)


class _Stream:
    def __init__(self, bsz, seq, ctx_len, d):
        self.bsz, self.seq, self.ctx_len, self.d = bsz, seq, ctx_len, d
        self.n_lat = bsz * seq
        self.n_tok = bsz * (seq + ctx_len)

    def tiles(self, tm, with_ctx):
        return (self.n_tok if with_ctx else self.n_lat) // tm

    def mod_row(self, tm):
        n_lat_tiles, per_seq, bsz = self.n_lat // tm, self.seq // tm, self.bsz
        return lambda t: jnp.where(t < n_lat_tiles, t // per_seq, bsz)

    def mod_spec(self, tm):
        row = self.mod_row(tm)
        return pl.BlockSpec((1, MOD_CHUNKS, self.d), lambda t, *_: (row(t), 0, 0))


def _tok_spec(tm, width):
    return pl.BlockSpec((tm, width), lambda t, *_: (t, 0))


def _full_spec(shape):
    zeros = (0,) * len(shape)
    return pl.BlockSpec(shape, lambda *_: zeros)


def _sgu_kernel(x_ref, mod_ref, g_ref, win_ref, vg_ref, ws_ref, bs_ref, wout_ref, o_ref):
    x = x_ref[...]
    mod = mod_ref[0]
    tm = x.shape[0]
    width = vg_ref.shape[1]
    gdim = width // SGU_GROUPS
    h = _modulate(x, g_ref[...], mod[0:1], mod[1:2]).astype(BF16)
    v = _gelu(_dot(h, win_ref[:, width:]))
    vn = (_rms(v) * vg_ref[...]).astype(BF16)
    u = _gelu(_dot(h, win_ref[:, :width]))
    rows = []
    for c in range(tm // SGU_CHUNK):
        cols = []
        for g in range(SGU_GROUPS):
            vv = vn[c * SGU_CHUNK:(c + 1) * SGU_CHUNK, g * gdim:(g + 1) * gdim]
            cols.append(_dot(ws_ref[g], vv) + bs_ref[:, g:g + 1])
        rows.append(jnp.concatenate(cols, axis=1))
    sv = jnp.concatenate(rows, axis=0)
    y = _dot((u * sv).astype(BF16), wout_ref[...])
    o_ref[...] = x + mod[2:3] * y


def _sgu_layer(st, xs, mods, gain, w_in, v_gain, w_s, b_s, w_out, with_ctx):
    tm, d = TOKEN_TILE, st.d
    width = v_gain.shape[0]
    return pl.pallas_call(
        _sgu_kernel,
        grid=(st.tiles(tm, with_ctx),),
        in_specs=[
            _tok_spec(tm, d), st.mod_spec(tm), _full_spec((1, d)),
            _full_spec((d, 2 * width)), _full_spec((1, width)),
            _full_spec((SGU_GROUPS, SGU_CHUNK, SGU_CHUNK)), _full_spec((SGU_CHUNK, SGU_GROUPS)),
            _full_spec((width, d)),
        ],
        out_specs=_tok_spec(tm, d),
        out_shape=jax.ShapeDtypeStruct((st.tiles(tm, with_ctx) * tm, d), F32),
        compiler_params=_cparams("parallel"),
        name="sgu_mixer",
    )(xs, mods, gain.reshape(1, d), w_in.astype(BF16), v_gain.reshape(1, width),
      w_s.astype(BF16), b_s.T, w_out.astype(BF16))


POOL_HALO = 8


def _pool_in_kernel(x_ref, mod_ref, g_ref, win_ref, z_ref):
    mod = mod_ref[0]
    h = _modulate(x_ref[...], g_ref[...], mod[0:1], mod[1:2]).astype(BF16)
    z_ref[...] = _dot(h, win_ref[...])


def _pool_out_kernel(z_ref, zp_ref, zn_ref, x_ref, mod_ref, wg_ref, sc_ref, wout_ref, o_ref,
                     *, n_lat_tiles, lat_tiles_per_seq, ctx_tiles_per_seq, seq, ctx_len):
    t = pl.program_id(0)
    tm = z_ref.shape[0]
    is_lat = t < n_lat_tiles
    per_seq = jnp.where(is_lat, lat_tiles_per_seq, ctx_tiles_per_seq)
    pos_tile = jnp.where(is_lat, t, t - n_lat_tiles) % per_seq
    length = jnp.where(is_lat, seq, ctx_len)
    has_prev = (pos_tile > 0).astype(F32)
    has_next = (pos_tile < per_seq - 1).astype(F32)
    z = z_ref[...]
    zext = jnp.concatenate([zp_ref[...] * has_prev, z, zn_ref[...] * has_next], axis=0)
    pos = (pos_tile * tm + lax.broadcasted_iota(jnp.int32, (tm, 1), 0))
    gdim = wg_ref.shape[1]
    n_ext = tm + 2 * POOL_HALO

    def shift_up(a, k):
        return a if k == 0 else pltpu.roll(a, n_ext - k, 0)

    outs = []
    for g, w in enumerate(POOL_WINDOWS):
        half = w // 2
        acc = zext[:, g * gdim:(g + 1) * gdim]
        span = 1
        while span < w:
            acc = acc + shift_up(acc, span)
            span *= 2
        win_sum = shift_up(acc, POOL_HALO - half)[:tm]
        cnt = (jnp.minimum(pos + half, length) - jnp.maximum(pos - half, 0)).astype(F32)
        pooled = win_sum / cnt - z[:, g * gdim:(g + 1) * gdim]
        outs.append(_dot(pooled.astype(BF16), wg_ref[g]))
    y = (jnp.concatenate(outs, axis=1) * sc_ref[...]).astype(BF16)
    o_ref[...] = x_ref[...] + mod_ref[0][2:3] * _dot(y, wout_ref[...])


def _pool_layer(st, xs, mods, gain, w_in, w_grp, scale, w_out, with_ctx):
    tm, d = TOKEN_TILE, st.d
    width = w_in.shape[1]
    n_tiles = st.tiles(tm, with_ctx)
    z = pl.pallas_call(
        _pool_in_kernel,
        grid=(n_tiles,),
        in_specs=[_tok_spec(tm, d), st.mod_spec(tm), _full_spec((1, d)), _full_spec((d, width))],
        out_specs=_tok_spec(tm, width),
        out_shape=jax.ShapeDtypeStruct((n_tiles * tm, width), F32),
        compiler_params=_cparams("parallel"),
        name="pool_in",
    )(xs, mods, gain.reshape(1, d), w_in.astype(BF16))
    halo_per_tile = tm // POOL_HALO
    last_halo = n_tiles * halo_per_tile - 1
    kern = functools.partial(
        _pool_out_kernel, n_lat_tiles=st.n_lat // tm, lat_tiles_per_seq=st.seq // tm,
        ctx_tiles_per_seq=st.ctx_len // tm, seq=st.seq, ctx_len=st.ctx_len)
    return pl.pallas_call(
        kern,
        grid=(n_tiles,),
        in_specs=[
            _tok_spec(tm, width),
            pl.BlockSpec((POOL_HALO, width), lambda t: (jnp.maximum(t * halo_per_tile - 1, 0), 0)),
            pl.BlockSpec((POOL_HALO, width), lambda t: (jnp.minimum((t + 1) * halo_per_tile, last_halo), 0)),
            _tok_spec(tm, d), st.mod_spec(tm),
            _full_spec(w_grp.shape), _full_spec((1, width)), _full_spec((width, d)),
        ],
        out_specs=_tok_spec(tm, d),
        out_shape=jax.ShapeDtypeStruct((st.tiles(tm, with_ctx) * tm, d), F32),
        compiler_params=_cparams("parallel"),
        name="pool_out",
    )(z, z, z, xs, mods, w_grp.astype(BF16), scale.reshape(1, width), w_out.astype(BF16))


def _rope_tables(seq, tm):
    pos = jnp.arange(seq)
    axis_dim = HEAD_DIM // 2
    freqs = ROPE_THETA ** (-jnp.arange(0, axis_dim, 2, dtype=F32) / axis_dim)
    ang = jnp.stack([pos // GRID_W, pos % GRID_W], axis=-1).astype(F32)[:, :, None] * freqs
    cos, sin = jnp.cos(ang), jnp.sin(ang)
    zero = jnp.zeros_like(sin)
    c = jnp.concatenate([cos, cos], axis=-1).reshape(seq, HEAD_DIM)
    s_lo = jnp.concatenate([-sin, zero], axis=-1).reshape(seq, HEAD_DIM)
    s_hi = jnp.concatenate([zero, sin], axis=-1).reshape(seq, HEAD_DIM)
    pad = jnp.zeros((tm, HEAD_DIM), F32)
    return (jnp.concatenate([c, pad + 1.0]), jnp.concatenate([s_lo, pad]), jnp.concatenate([s_hi, pad]))


def _qkv_kernel(x_ref, mod_ref, g_ref, w_ref, qg_ref, kg_ref, c_ref, slo_ref, shi_ref, q_ref, k_ref, v_ref):
    mod = mod_ref[0]
    h = _modulate(x_ref[...], g_ref[...], mod[0:1], mod[1:2]).astype(BF16)
    qkv = _dot(h, w_ref[...])
    nq, nk = q_ref.shape[1], k_ref.shape[1]
    cos, s_lo, s_hi = c_ref[...], slo_ref[...], shi_ref[...]
    quarter = HEAD_DIM // 4

    def norm_rope(t, gain):
        t = _rms(t) * gain
        return (t * cos + pltpu.roll(t, HEAD_DIM - quarter, 1) * s_lo + pltpu.roll(t, quarter, 1) * s_hi)

    for hh in range(nq // HEAD_DIM):
        sl = slice(hh * HEAD_DIM, (hh + 1) * HEAD_DIM)
        q_ref[:, sl] = norm_rope(qkv[:, sl], qg_ref[...]).astype(BF16)
    for hh in range(nk // HEAD_DIM):
        sl = slice(hh * HEAD_DIM, (hh + 1) * HEAD_DIM)
        k_ref[:, sl] = norm_rope(qkv[:, nq + hh * HEAD_DIM:nq + (hh + 1) * HEAD_DIM], kg_ref[...]).astype(BF16)
    v_ref[...] = qkv[:, nq + nk:].astype(BF16)


def _attn_kernel(q_ref, kl_ref, vl_ref, kc_ref, vc_ref, x_ref, mod_ref, wo_ref, o_ref, *, n_lat_tiles):
    n_heads = q_ref.shape[1] // HEAD_DIM
    q_per_kv = n_heads // N_KV_HEADS
    scale = HEAD_DIM ** -0.5

    def body(with_lat):
        def kv_of(hh):
            return slice((hh // q_per_kv) * HEAD_DIM, (hh // q_per_kv + 1) * HEAD_DIM)

        def scores(hh):
            q = q_ref[:, hh * HEAD_DIM:(hh + 1) * HEAD_DIM]
            s_c = _dot_nt(q, kc_ref[:, kv_of(hh)]) * scale
            s_l = _dot_nt(q, kl_ref[:, kv_of(hh)]) * scale if with_lat else None
            return s_c, s_l

        def attend(hh, s_c, s_l):
            m = jnp.max(s_c, axis=-1, keepdims=True)
            if with_lat:
                m = jnp.maximum(m, jnp.max(s_l, axis=-1, keepdims=True))
            p_c = jnp.exp(s_c - m)
            den = jnp.sum(p_c, axis=-1, keepdims=True)
            o = _dot(p_c.astype(BF16), vc_ref[:, kv_of(hh)])
            if with_lat:
                p_l = jnp.exp(s_l - m)
                den = den + jnp.sum(p_l, axis=-1, keepdims=True)
                o = o + _dot(p_l.astype(BF16), vl_ref[:, kv_of(hh)])
            return o / den

        outs = []
        pending = scores(0)
        for hh in range(n_heads):
            upcoming = scores(hh + 1) if hh + 1 < n_heads else None
            outs.append(attend(hh, *pending))
            pending = upcoming
        y = _dot(jnp.concatenate(outs, axis=1).astype(BF16), wo_ref[...])
        o_ref[...] = x_ref[...] + mod_ref[0][2:3] * y

    t = pl.program_id(0)
    pl.when(t < n_lat_tiles)(lambda: body(True))
    pl.when(t >= n_lat_tiles)(lambda: body(False))


def _attn_layer(st, xs, mods, gain, w_qkv, q_gain, k_gain, w_o, with_ctx):
    d = st.d
    tm = st.ctx_len
    assert st.seq % tm == 0
    n_all = st.n_tok // tm
    n_lat_tiles, per_seq = st.n_lat // tm, st.seq // tm
    nq = w_o.shape[0]
    nk = (w_qkv.shape[1] - nq) // 2
    cos, s_lo, s_hi = _rope_tables(st.seq, tm)
    rope_spec = pl.BlockSpec((tm, HEAD_DIM), lambda t: (jnp.where(t < n_lat_tiles, t % per_seq, per_seq), 0))
    q, k, v = pl.pallas_call(
        _qkv_kernel,
        grid=(n_all,),
        in_specs=[
            _tok_spec(tm, d), st.mod_spec(tm), _full_spec((1, d)), _full_spec(w_qkv.shape),
            _full_spec((1, HEAD_DIM)), _full_spec((1, HEAD_DIM)), rope_spec, rope_spec, rope_spec,
        ],
        out_specs=[_tok_spec(tm, nq), _tok_spec(tm, nk), _tok_spec(tm, nk)],
        out_shape=[jax.ShapeDtypeStruct((st.n_tok, nq), BF16), jax.ShapeDtypeStruct((st.n_tok, nk), BF16),
                   jax.ShapeDtypeStruct((st.n_tok, nk), BF16)],
        compiler_params=_cparams("parallel"),
        name="attn_qkv",
    )(xs, mods, gain.reshape(1, d), w_qkv.astype(BF16), q_gain.reshape(1, HEAD_DIM),
      k_gain.reshape(1, HEAD_DIM), cos, s_lo, s_hi)

    bsz = st.bsz
    batch_of = lambda t: jnp.where(t < n_lat_tiles, t // per_seq, t - n_lat_tiles)
    lat_spec = pl.BlockSpec((st.seq, nk), lambda t: (jnp.minimum(t // per_seq, bsz - 1), 0))
    ctx_spec = pl.BlockSpec((tm, nk), lambda t: (n_lat_tiles + batch_of(t), 0))
    return pl.pallas_call(
        functools.partial(_attn_kernel, n_lat_tiles=n_lat_tiles),
        grid=(st.tiles(tm, with_ctx),),
        in_specs=[_tok_spec(tm, nq), lat_spec, lat_spec, ctx_spec, ctx_spec,
                  _tok_spec(tm, d), st.mod_spec(tm), _full_spec((nq, d))],
        out_specs=_tok_spec(tm, d),
        out_shape=jax.ShapeDtypeStruct((st.tiles(tm, with_ctx) * tm, d), F32),
        compiler_params=_cparams("parallel"),
        name="attn_core",
    )(q, k, v, k, v, xs, mods, w_o.astype(BF16))


_INT_MIN = -2 ** 31


def _sortable(x):
    b = lax.bitcast_convert_type(x, jnp.int32)
    return b ^ ((b >> 31) & 0x7FFFFFFF)


def _unsortable(k):
    return lax.bitcast_convert_type(k ^ ((k >> 31) & 0x7FFFFFFF), F32)


def _extract16(k, iota, exact):
    n = k.shape[0]
    tops = []
    for a in range(PEER_TOPK):
        m = jnp.max(k, axis=0, keepdims=True)
        hit = k == m
        if exact:
            first = jnp.min(jnp.where(hit, iota, n), axis=0, keepdims=True)
            hit = iota == first
        k = jnp.where(hit, _INT_MIN + a, k)
        tops.append(m)
    return k, tops


def _rows16(tops, iota16):
    out = jnp.zeros(iota16.shape, tops[0].dtype)
    for a, t in enumerate(tops):
        out = jnp.where(iota16 == a, t, out)
    return out


_CAND_HALF = PEER_TOPK // 2


def _peer_select(s1, s2, exact):
    n, width = s1.shape
    topk = float(PEER_TOPK)
    iota = lax.broadcasted_iota(jnp.int32, (n, width), 0)
    iota16 = iota[0:PEER_TOPK]
    mark_end = _INT_MIN + PEER_TOPK
    k1, top1 = _extract16(_sortable(s1), iota, exact)
    k2, top2 = _extract16(_sortable(s2), iota, exact)
    v1 = _unsortable(_rows16(top1, iota16))
    v2 = _unsortable(_rows16(top2, iota16))
    f1 = [_unsortable(t) for t in top1[:_CAND_HALF]]
    f2_0 = _unsortable(top2[0])
    pieces = [v2 + f1[0]]
    pieces += [v2[0:_CAND_HALF] + f1[a] for a in range(1, _CAND_HALF)]
    pieces.append(v1[_CAND_HALF:] + f2_0)
    cand = jnp.concatenate(pieces, axis=0)
    nc = cand.shape[0]
    kc, _ = _extract16(_sortable(cand), iota[0:nc], exact)
    picked = kc < mark_end
    if not exact:
        surplus = jnp.sum(jnp.where(picked, 1.0, 0.0), axis=0, keepdims=True) - topk
        picked = kc < mark_end - surplus.astype(jnp.int32)
    sel = jnp.where(picked, 1.0, 0.0)
    z = jnp.sum(jnp.where(picked, jnp.exp(cand - (f1[0] + f2_0)), 0.0), axis=0, keepdims=True)
    cnt = [jnp.sum(sel[0:PEER_TOPK], axis=0, keepdims=True)]
    for a in range(1, _CAND_HALF):
        row = PEER_TOPK + (a - 1) * _CAND_HALF
        cnt.append(jnp.sum(sel[row:row + _CAND_HALF], axis=0, keepdims=True))
    tail = sel[nc - _CAND_HALF:]
    for a in range(_CAND_HALF, PEER_TOPK):
        pick = iota[0:_CAND_HALF] == a - _CAND_HALF
        cnt.append(jnp.sum(jnp.where(pick, tail, 0.0), axis=0, keepdims=True))
    cut = jnp.zeros(s1.shape, F32)
    for a in range(PEER_TOPK):
        cut = jnp.where(k1 == _INT_MIN + a, cnt[a], cut)
    in1, in2 = k1 < mark_end, k2 < mark_end
    r2 = jnp.where(in2, (k2 - _INT_MIN).astype(F32), topk)
    e1 = jnp.exp(s1 - f1[0])
    e2 = jnp.exp(s2 - f2_0) / z
    n1 = jnp.sum(jnp.where(in1, 1.0, 0.0), axis=0, keepdims=True)
    n2 = jnp.sum(jnp.where(in2, 1.0, 0.0), axis=0, keepdims=True)
    nsel = jnp.sum(sel, axis=0, keepdims=True)
    bad = jnp.where((n1 != topk) | (n2 != topk) | (nsel != topk), 1.0, 0.0)
    return cut, e1, r2, e2, bad


def _pack_rows(x):
    return pltpu.bitcast(x.astype(BF16), jnp.uint32)


def _unpack_rows(x):
    return pltpu.bitcast(x, BF16)


def _pack_twice(x):
    u = lax.bitcast_convert_type(x.astype(BF16).astype(F32), jnp.uint32)
    return u | (u >> 16)


PACK_BLOCK = 1024


def _pack_kernel(w_ref, o_ref, *, transpose):
    if transpose:
        o_ref[0, 0] = _pack_rows(w_ref[0].T)
    else:
        o_ref[0] = _pack_rows(w_ref[0])


def _pack_weights(w, transpose):
    layers, r, c = w.shape
    pb = PACK_BLOCK
    if transpose:
        out_spec = pl.BlockSpec((1, 1, pb // 2, pb), lambda l, i, j: (l, i, j, 0))
        out_shape = (layers, r // pb, c // 2, pb)
    else:
        out_spec = pl.BlockSpec((1, pb // 2, pb), lambda l, i, j: (l, i, j))
        out_shape = (layers, r // 2, c)
    return pl.pallas_call(
        functools.partial(_pack_kernel, transpose=transpose),
        grid=(layers, r // pb, c // pb),
        in_specs=[pl.BlockSpec((1, pb, pb), lambda l, i, j: (l, i, j))],
        out_specs=out_spec,
        out_shape=jax.ShapeDtypeStruct(out_shape, jnp.uint32),
        compiler_params=_cparams("parallel", "parallel", "parallel"),
        name="pack_t" if transpose else "pack",
    )(w)


def _peer_select_kernel(x_ref, mod_ref, g_ref, wq_ref, k1_ref, k2_ref,
                        h_ref, cut_ref, e1_ref, r2_ref, e2_ref, s1_scr, s2_scr):
    mod = mod_ref[0]
    f_t = _modulate(x_ref[...], g_ref[...], mod[3:4], mod[4:5]).T.astype(BF16)
    h_ref[...] = _pack_rows(f_t)
    tm = f_t.shape[1]
    n_heads, n_keys, half = k1_ref.shape
    q_t = _dot(_unpack_rows(wq_ref[0]), f_t)
    for hh in range(n_heads):
        base = hh * 2 * half
        s1_scr[hh] = _dot(k1_ref[hh], q_t[base:base + half].astype(BF16))
        s2_scr[hh] = _dot(k2_ref[hh], q_t[base + half:base + 2 * half].astype(BF16))
    n_sub = tm // LANES

    def step(idx, carry):
        hh = idx // (n_sub // SELECT_UNROLL)
        base = (idx % (n_sub // SELECT_UNROLL)) * SELECT_UNROLL
        bad_any = jnp.zeros((1, LANES), F32)
        work = []
        for sub in range(SELECT_UNROLL):
            lanes = pl.ds(pl.multiple_of((base + sub) * LANES, LANES), LANES)
            s1, s2 = s1_scr[hh, :, lanes], s2_scr[hh, :, lanes]
            *fast, bad = _peer_select(s1, s2, exact=False)
            work.append((lanes, s1, s2, fast))
            bad_any = jnp.maximum(bad_any, bad)

        def store(lanes, cut, e1, r2, e2):
            cut_ref[hh, :, lanes] = _pack_twice(cut)
            e1_ref[hh, :, lanes] = _pack_twice(e1)
            r2_ref[hh, :, lanes] = _pack_rows(r2)
            e2_ref[hh, :, lanes] = _pack_rows(e2)

        for lanes, _, _, fast in work:
            store(lanes, *fast)

        @pl.when(jnp.max(bad_any) > 0.0)
        def _():
            for lanes, s1, s2, _ in work:
                store(lanes, *_peer_select(s1, s2, exact=True)[:4])

        return carry

    lax.fori_loop(0, n_heads * n_sub // SELECT_UNROLL, step, 0)


def _peer_main_kernel(h_ref, u_ref, vt_ref, cut_ref, e1_ref, r2_ref, e2_ref, x_ref, mod_ref, fg_ref,
                      o_ref, acc, act0, act1, w0, w1, *, final_norm, n_chunks):
    g = pl.program_id(0)
    ec, tm = act0.shape
    n_heads, n_keys = r2_ref.shape[0], 2 * r2_ref.shape[1]
    rows_per_chunk = ec // n_keys
    half_keys = n_keys // 2
    packed_half = half_keys // 2

    @pl.when(g == 0)
    def _():
        acc[...] = jnp.zeros_like(acc)
        act1[...] = jnp.zeros_like(act1)
        w0[...] = jnp.zeros_like(w0)

    def gate_chunk(cut_ref, e1_ref, act_ref, w_ref, token_blocks):
        zero = jnp.zeros((half_keys, LANES), BF16)
        for part in range(n_keys // half_keys):
            prow = slice(part * packed_half, (part + 1) * packed_half)
            for tl in token_blocks:
                lanes = slice(tl * LANES, (tl + 1) * LANES)
                for first in range(0, rows_per_chunk, GATE_ROWS):
                    gates = [zero] * GATE_ROWS
                    for hh in range(n_heads):
                        r2 = _unpack_rows(r2_ref[hh, prow, lanes])
                        e2 = _unpack_rows(e2_ref[hh, prow, lanes])
                        for k in range(GATE_ROWS):
                            il = first + k
                            cut = _unpack_rows(jnp.broadcast_to(cut_ref[hh, il:il + 1, lanes], (packed_half, LANES)))
                            e1 = _unpack_rows(jnp.broadcast_to(e1_ref[hh, il:il + 1, lanes], (packed_half, LANES)))
                            gates[k] = gates[k] + jnp.where(r2 < cut, e2, zero) * e1
                    for k in range(GATE_ROWS):
                        il = first + k
                        rows = slice(il * n_keys + part * half_keys, il * n_keys + (part + 1) * half_keys)
                        w_ref[rows, lanes] = _gelu(act_ref[rows, lanes]).astype(BF16) * gates[k]

    def stage(act_new, act_old, w_new, w_old):
        n_blocks = tm // LANES
        gate_chunk(cut_ref, e1_ref, act_old, w_new, range(0, n_blocks // 2))
        act_new[...] = _dot(_unpack_rows(u_ref[0]), _unpack_rows(h_ref[...]))
        gate_chunk(cut_ref, e1_ref, act_old, w_new, range(n_blocks // 2, n_blocks))
        acc[...] += sum(_dot(_unpack_rows(vt_ref[0, k]), w_old[k * PACK_BLOCK:(k + 1) * PACK_BLOCK, :])
                        for k in range(ec // PACK_BLOCK))

    pl.when(g % 2 == 0)(lambda: stage(act0, act1, w1, w0))
    pl.when(g % 2 == 1)(lambda: stage(act1, act0, w0, w1))

    tile_done = (g >= 2) & ((g - 2) % n_chunks == n_chunks - 1)

    @pl.when(tile_done)
    def _():
        y = x_ref[...] + mod_ref[0][5:6] * acc[...].T
        if final_norm:
            y = _rms(y) * fg_ref[...]
        o_ref[...] = y

    @pl.when(tile_done)
    def _():
        acc[...] = jnp.zeros_like(acc)


def _peer_layer(st, xs, mods, gain, layer, wq_packed, k1, k2, u_packed, vt_packed, final_gain, with_ctx,
                final_norm):
    tm, d = PEER_TILE, st.d
    n_heads, n_keys, half = k1.shape
    n_exp = 2 * u_packed.shape[1]
    ec = PEER_EXPERT_CHUNK
    n_tiles = st.tiles(tm, with_ctx)
    n_rows = n_tiles * tm
    u32 = jnp.uint32
    h_spec = pl.BlockSpec((d // 2, tm), lambda t, *_: (0, t))
    key_spec = pl.BlockSpec((n_heads, n_keys, tm), lambda t, *_: (0, 0, t))
    key_shape = jax.ShapeDtypeStruct((n_heads, n_keys, n_rows), u32)
    pair_spec = pl.BlockSpec((n_heads, n_keys // 2, tm), lambda t, *_: (0, 0, t))
    pair_shape = jax.ShapeDtypeStruct((n_heads, n_keys // 2, n_rows), u32)
    h, cut, e1, r2, e2 = pl.pallas_call(
        _peer_select_kernel,
        grid=(n_tiles,),
        in_specs=[_tok_spec(tm, d), st.mod_spec(tm), _full_spec((1, d)),
                  pl.BlockSpec((1,) + wq_packed.shape[1:], lambda t: (layer, 0, 0)),
                  _full_spec(k1.shape), _full_spec(k2.shape)],
        out_specs=[h_spec, key_spec, key_spec, pair_spec, pair_spec],
        out_shape=[jax.ShapeDtypeStruct((d // 2, n_rows), u32), key_shape, key_shape, pair_shape, pair_shape],
        scratch_shapes=[pltpu.VMEM((n_heads, n_keys, tm), F32), pltpu.VMEM((n_heads, n_keys, tm), F32)],
        compiler_params=_cparams("parallel"),
        name="peer_select",
    )(xs, mods, gain.reshape(1, d), wq_packed, k1.astype(BF16), k2.astype(BF16))

    n_chunks = n_exp // ec
    n_items = n_tiles * n_chunks
    rows_per_chunk = ec // n_keys
    mod_row = st.mod_row(tm)

    def item(lag):
        def tile(g):
            return jnp.clip(g - lag, 0, n_items - 1) // n_chunks

        def chunk(g):
            return jnp.clip(g - lag, 0, n_items - 1) % n_chunks

        return tile, chunk

    (tile0, chunk0), (tile1, chunk1), (tile2, chunk2) = item(0), item(1), item(2)
    sub1_spec = pl.BlockSpec((n_heads, rows_per_chunk, tm), lambda g: (0, chunk1(g), tile1(g)))
    sub2_spec = pl.BlockSpec((n_heads, n_keys // 2, tm), lambda g: (0, 0, tile1(g)))
    return pl.pallas_call(
        functools.partial(_peer_main_kernel, final_norm=final_norm, n_chunks=n_chunks),
        grid=(n_items + 2,),
        in_specs=[
            pl.BlockSpec((d // 2, tm), lambda g: (0, tile0(g))),
            pl.BlockSpec((1, ec // 2, d), lambda g: (layer, chunk0(g), 0)),
            pl.BlockSpec((1, ec // PACK_BLOCK, d // 2, PACK_BLOCK), lambda g: (layer, chunk2(g), 0, 0)),
            sub1_spec, sub1_spec, sub2_spec, sub2_spec,
            pl.BlockSpec((tm, d), lambda g: (tile2(g), 0)),
            pl.BlockSpec((1, MOD_CHUNKS, d), lambda g: (mod_row(tile2(g)), 0, 0)),
            _full_spec((1, d)),
        ],
        out_specs=pl.BlockSpec((tm, d), lambda g: (tile2(g), 0)),
        out_shape=jax.ShapeDtypeStruct((n_rows, d), F32),
        scratch_shapes=[pltpu.VMEM((d, tm), F32),
                        pltpu.VMEM((ec, tm), F32), pltpu.VMEM((ec, tm), F32),
                        pltpu.VMEM((ec, tm), BF16), pltpu.VMEM((ec, tm), BF16)],
        compiler_params=_cparams("arbitrary"),
        name="peer_main",
    )(h, u_packed, vt_packed, cut, e1, r2, e2, xs, mods, final_gain.reshape(1, d))


def kernel(x, c, ctx, c_ctx, ada_w, ada_b, norm1_g, norm2_g, sgu_w_in, sgu_v_gain, sgu_w_s, sgu_b_s, sgu_w_out, pool_w_in, pool_w_grp, pool_scale, pool_w_out, attn_w_qkv, attn_q_gain, attn_k_gain, attn_w_o, peer_w_q, peer_k1, peer_k2, peer_u, peer_v, final_gain):
    bsz, seq, d = x.shape
    ctx_len = ctx.shape[1]
    depth = ada_w.shape[0]
    st = _Stream(bsz, seq, ctx_len, d)
    assert bsz + 1 <= MOD_ROWS and seq % PEER_TILE == 0 and (bsz * ctx_len) % PEER_TILE == 0
    assert seq % TOKEN_TILE == 0 and ctx_len % TOKEN_TILE == 0

    cond = jnp.concatenate([c, c_ctx[None, :], jnp.zeros((MOD_ROWS - bsz - 1, d), F32)], axis=0)
    mods = _ada_mods(cond, ada_w, ada_b)
    assert peer_w_q.shape[1] == PACK_BLOCK and PEER_EXPERT_CHUNK % PACK_BLOCK == 0
    wq_packed = _pack_weights(peer_w_q, transpose=True)[:, 0]
    u_packed = _pack_weights(peer_u, transpose=False)
    vt_packed = _pack_weights(peer_v, transpose=True)
    xs = jnp.concatenate([x.reshape(bsz * seq, d), ctx.reshape(bsz * ctx_len, d)], axis=0)

    for i in range(depth):
        kind, j = i % N_MIXERS, i // N_MIXERS
        with_ctx = i < depth - 1
        if kind == 0:
            xs = _sgu_layer(st, xs, mods[i], norm1_g[i], sgu_w_in[j], sgu_v_gain[j], sgu_w_s[j], sgu_b_s[j],
                            sgu_w_out[j], with_ctx)
        elif kind == 1:
            xs = _pool_layer(st, xs, mods[i], norm1_g[i], pool_w_in[j], pool_w_grp[j], pool_scale[j],
                             pool_w_out[j], with_ctx)
        else:
            xs = _attn_layer(st, xs, mods[i], norm1_g[i], attn_w_qkv[j], attn_q_gain[j], attn_k_gain[j],
                             attn_w_o[j], with_ctx)
        xs = _peer_layer(st, xs, mods[i], norm2_g[i], i, wq_packed, peer_k1[i], peer_k2[i], u_packed, vt_packed,
                         final_gain, with_ctx, final_norm=(i == depth - 1))
    return xs[:bsz * seq].reshape(bsz, seq, d)
```

```python
import functools
import math

import jax
import jax.numpy as jnp
from jax import lax
from jax.experimental import pallas as pl
from jax.experimental.pallas import tpu as pltpu

F32 = jnp.float32
BF16 = jnp.bfloat16

NORM_EPS = 1e-6
MOD_CHUNKS = 6
GRID_W = 64
SGU_CHUNK = 128
SGU_GROUPS = 8
POOL_WINDOWS = (2, 4, 8, 16)
HEAD_DIM = 128
N_KV_HEADS = 2
ROPE_THETA = 10000.0
PEER_TOPK = 16
N_MIXERS = 3

LANES = 128
MOD_ROWS = 16
VMEM_LIMIT = 56 * 1024 * 1024

TOKEN_TILE = 256
PEER_TILE = 512
PEER_EXPERT_CHUNK = 1024
TABLE_BUFFERS = 3
GATE_ROWS = 8
SELECT_UNROLL = 4


def _cparams(*sem):
    return pltpu.CompilerParams(dimension_semantics=sem, vmem_limit_bytes=VMEM_LIMIT)


def _rms(x):
    return x * lax.rsqrt(jnp.mean(x * x, axis=-1, keepdims=True) + NORM_EPS)


def _modulate(x, gain, shift, scale):
    return _rms(x) * gain * (1.0 + scale) + shift


def _gelu(x):
    return 0.5 * x * (1.0 + lax.erf(x * (1.0 / math.sqrt(2.0))))


def _dot(a, b):
    return jnp.dot(a, b, preferred_element_type=F32)


def _dot_nt(a, b):
    return lax.dot_general(a, b, (((1,), (1,)), ((), ())), preferred_element_type=F32)


def _ada_kernel(c_ref, w_ref, b_ref, o_ref):
    c = c_ref[...]
    a = c * (1.0 / (1.0 + jnp.exp(-c)))
    o_ref[0] = _dot(a.astype(BF16), w_ref[0].astype(BF16)) + b_ref[0]


def _ada_mods(cond, ada_w, ada_b):
    depth, d, n = ada_w.shape
    tn = n // 4
    out = pl.pallas_call(
        _ada_kernel,
        grid=(depth, n // tn),
        in_specs=[
            pl.BlockSpec((MOD_ROWS, d), lambda l, j: (0, 0)),
            pl.BlockSpec((1, d, tn), lambda l, j: (l, 0, j)),
            pl.BlockSpec((1, 1, tn), lambda l, j: (l, 0, j)),
        ],
        out_specs=pl.BlockSpec((1, MOD_ROWS, tn), lambda l, j: (l, 0, j)),
        out_shape=jax.ShapeDtypeStruct((depth, MOD_ROWS, n), F32),
        compiler_params=_cparams("parallel", "parallel"),
        name="ada_mods",
    )(cond, ada_w, ada_b.reshape(depth, 1, n))
    return out.reshape(depth, MOD_ROWS, MOD_CHUNKS, d)


class _Stream:
    def __init__(self, bsz, seq, ctx_len, d):
        self.bsz, self.seq, self.ctx_len, self.d = bsz, seq, ctx_len, d
        self.n_lat = bsz * seq
        self.n_tok = bsz * (seq + ctx_len)

    def tiles(self, tm, with_ctx):
        return (self.n_tok if with_ctx else self.n_lat) // tm

    def mod_row(self, tm):
        n_lat_tiles, per_seq, bsz = self.n_lat // tm, self.seq // tm, self.bsz
        return lambda t: jnp.where(t < n_lat_tiles, t // per_seq, bsz)

    def mod_spec(self, tm):
        row = self.mod_row(tm)
        return pl.BlockSpec((1, MOD_CHUNKS, self.d), lambda t, *_: (row(t), 0, 0))


def _tok_spec(tm, width):
    return pl.BlockSpec((tm, width), lambda t, *_: (t, 0))


def _full_spec(shape):
    zeros = (0,) * len(shape)
    return pl.BlockSpec(shape, lambda *_: zeros)


def _sgu_kernel(x_ref, mod_ref, g_ref, win_ref, vg_ref, ws_ref, bs_ref, wout_ref, o_ref):
    x = x_ref[...]
    mod = mod_ref[0]
    tm = x.shape[0]
    width = vg_ref.shape[1]
    gdim = width // SGU_GROUPS
    h = _modulate(x, g_ref[...], mod[0:1], mod[1:2]).astype(BF16)
    v = _gelu(_dot(h, win_ref[:, width:]))
    vn = (_rms(v) * vg_ref[...]).astype(BF16)
    u = _gelu(_dot(h, win_ref[:, :width]))
    rows = []
    for c in range(tm // SGU_CHUNK):
        cols = []
        for g in range(SGU_GROUPS):
            vv = vn[c * SGU_CHUNK:(c + 1) * SGU_CHUNK, g * gdim:(g + 1) * gdim]
            cols.append(_dot(ws_ref[g], vv) + bs_ref[:, g:g + 1])
        rows.append(jnp.concatenate(cols, axis=1))
    sv = jnp.concatenate(rows, axis=0)
    y = _dot((u * sv).astype(BF16), wout_ref[...])
    o_ref[...] = x + mod[2:3] * y


def _sgu_layer(st, xs, mods, gain, w_in, v_gain, w_s, b_s, w_out, with_ctx):
    tm, d = TOKEN_TILE, st.d
    width = v_gain.shape[0]
    return pl.pallas_call(
        _sgu_kernel,
        grid=(st.tiles(tm, with_ctx),),
        in_specs=[
            _tok_spec(tm, d), st.mod_spec(tm), _full_spec((1, d)),
            _full_spec((d, 2 * width)), _full_spec((1, width)),
            _full_spec((SGU_GROUPS, SGU_CHUNK, SGU_CHUNK)), _full_spec((SGU_CHUNK, SGU_GROUPS)),
            _full_spec((width, d)),
        ],
        out_specs=_tok_spec(tm, d),
        out_shape=jax.ShapeDtypeStruct((st.tiles(tm, with_ctx) * tm, d), F32),
        compiler_params=_cparams("parallel"),
        name="sgu_mixer",
    )(xs, mods, gain.reshape(1, d), w_in.astype(BF16), v_gain.reshape(1, width),
      w_s.astype(BF16), b_s.T, w_out.astype(BF16))


POOL_HALO = 8


def _pool_in_kernel(x_ref, mod_ref, g_ref, win_ref, z_ref):
    mod = mod_ref[0]
    h = _modulate(x_ref[...], g_ref[...], mod[0:1], mod[1:2]).astype(BF16)
    z_ref[...] = _dot(h, win_ref[...])


def _pool_out_kernel(z_ref, zp_ref, zn_ref, x_ref, mod_ref, wg_ref, sc_ref, wout_ref, o_ref,
                     *, n_lat_tiles, lat_tiles_per_seq, ctx_tiles_per_seq, seq, ctx_len):
    t = pl.program_id(0)
    tm = z_ref.shape[0]
    is_lat = t < n_lat_tiles
    per_seq = jnp.where(is_lat, lat_tiles_per_seq, ctx_tiles_per_seq)
    pos_tile = jnp.where(is_lat, t, t - n_lat_tiles) % per_seq
    length = jnp.where(is_lat, seq, ctx_len)
    has_prev = (pos_tile > 0).astype(F32)
    has_next = (pos_tile < per_seq - 1).astype(F32)
    z = z_ref[...]
    zext = jnp.concatenate([zp_ref[...] * has_prev, z, zn_ref[...] * has_next], axis=0)
    pos = (pos_tile * tm + lax.broadcasted_iota(jnp.int32, (tm, 1), 0))
    gdim = wg_ref.shape[1]
    n_ext = tm + 2 * POOL_HALO

    def shift_up(a, k):
        return a if k == 0 else pltpu.roll(a, n_ext - k, 0)

    outs = []
    for g, w in enumerate(POOL_WINDOWS):
        half = w // 2
        acc = zext[:, g * gdim:(g + 1) * gdim]
        span = 1
        while span < w:
            acc = acc + shift_up(acc, span)
            span *= 2
        win_sum = shift_up(acc, POOL_HALO - half)[:tm]
        cnt = (jnp.minimum(pos + half, length) - jnp.maximum(pos - half, 0)).astype(F32)
        pooled = win_sum / cnt - z[:, g * gdim:(g + 1) * gdim]
        outs.append(_dot(pooled.astype(BF16), wg_ref[g]))
    y = (jnp.concatenate(outs, axis=1) * sc_ref[...]).astype(BF16)
    o_ref[...] = x_ref[...] + mod_ref[0][2:3] * _dot(y, wout_ref[...])


def _pool_layer(st, xs, mods, gain, w_in, w_grp, scale, w_out, with_ctx):
    tm, d = TOKEN_TILE, st.d
    width = w_in.shape[1]
    n_tiles = st.tiles(tm, with_ctx)
    z = pl.pallas_call(
        _pool_in_kernel,
        grid=(n_tiles,),
        in_specs=[_tok_spec(tm, d), st.mod_spec(tm), _full_spec((1, d)), _full_spec((d, width))],
        out_specs=_tok_spec(tm, width),
        out_shape=jax.ShapeDtypeStruct((n_tiles * tm, width), F32),
        compiler_params=_cparams("parallel"),
        name="pool_in",
    )(xs, mods, gain.reshape(1, d), w_in.astype(BF16))
    halo_per_tile = tm // POOL_HALO
    last_halo = n_tiles * halo_per_tile - 1
    kern = functools.partial(
        _pool_out_kernel, n_lat_tiles=st.n_lat // tm, lat_tiles_per_seq=st.seq // tm,
        ctx_tiles_per_seq=st.ctx_len // tm, seq=st.seq, ctx_len=st.ctx_len)
    return pl.pallas_call(
        kern,
        grid=(n_tiles,),
        in_specs=[
            _tok_spec(tm, width),
            pl.BlockSpec((POOL_HALO, width), lambda t: (jnp.maximum(t * halo_per_tile - 1, 0), 0)),
            pl.BlockSpec((POOL_HALO, width), lambda t: (jnp.minimum((t + 1) * halo_per_tile, last_halo), 0)),
            _tok_spec(tm, d), st.mod_spec(tm),
            _full_spec(w_grp.shape), _full_spec((1, width)), _full_spec((width, d)),
        ],
        out_specs=_tok_spec(tm, d),
        out_shape=jax.ShapeDtypeStruct((st.tiles(tm, with_ctx) * tm, d), F32),
        compiler_params=_cparams("parallel"),
        name="pool_out",
    )(z, z, z, xs, mods, w_grp.astype(BF16), scale.reshape(1, width), w_out.astype(BF16))


def _rope_tables(seq, tm):
    pos = jnp.arange(seq)
    axis_dim = HEAD_DIM // 2
    freqs = ROPE_THETA ** (-jnp.arange(0, axis_dim, 2, dtype=F32) / axis_dim)
    ang = jnp.stack([pos // GRID_W, pos % GRID_W], axis=-1).astype(F32)[:, :, None] * freqs
    cos, sin = jnp.cos(ang), jnp.sin(ang)
    zero = jnp.zeros_like(sin)
    c = jnp.concatenate([cos, cos], axis=-1).reshape(seq, HEAD_DIM)
    s_lo = jnp.concatenate([-sin, zero], axis=-1).reshape(seq, HEAD_DIM)
    s_hi = jnp.concatenate([zero, sin], axis=-1).reshape(seq, HEAD_DIM)
    pad = jnp.zeros((tm, HEAD_DIM), F32)
    return (jnp.concatenate([c, pad + 1.0]), jnp.concatenate([s_lo, pad]), jnp.concatenate([s_hi, pad]))


def _qkv_kernel(x_ref, mod_ref, g_ref, w_ref, qg_ref, kg_ref, c_ref, slo_ref, shi_ref, q_ref, k_ref, v_ref):
    mod = mod_ref[0]
    h = _modulate(x_ref[...], g_ref[...], mod[0:1], mod[1:2]).astype(BF16)
    qkv = _dot(h, w_ref[...])
    nq, nk = q_ref.shape[1], k_ref.shape[1]
    cos, s_lo, s_hi = c_ref[...], slo_ref[...], shi_ref[...]
    quarter = HEAD_DIM // 4

    def norm_rope(t, gain):
        t = _rms(t) * gain
        return (t * cos + pltpu.roll(t, HEAD_DIM - quarter, 1) * s_lo + pltpu.roll(t, quarter, 1) * s_hi)

    for hh in range(nq // HEAD_DIM):
        sl = slice(hh * HEAD_DIM, (hh + 1) * HEAD_DIM)
        q_ref[:, sl] = norm_rope(qkv[:, sl], qg_ref[...]).astype(BF16)
    for hh in range(nk // HEAD_DIM):
        sl = slice(hh * HEAD_DIM, (hh + 1) * HEAD_DIM)
        k_ref[:, sl] = norm_rope(qkv[:, nq + hh * HEAD_DIM:nq + (hh + 1) * HEAD_DIM], kg_ref[...]).astype(BF16)
    v_ref[...] = qkv[:, nq + nk:].astype(BF16)


def _attn_kernel(q_ref, kl_ref, vl_ref, kc_ref, vc_ref, x_ref, mod_ref, wo_ref, o_ref, *, n_lat_tiles):
    n_heads = q_ref.shape[1] // HEAD_DIM
    q_per_kv = n_heads // N_KV_HEADS
    scale = HEAD_DIM ** -0.5

    def body(with_lat):
        def kv_of(hh):
            return slice((hh // q_per_kv) * HEAD_DIM, (hh // q_per_kv + 1) * HEAD_DIM)

        def scores(hh):
            q = q_ref[:, hh * HEAD_DIM:(hh + 1) * HEAD_DIM]
            s_c = _dot_nt(q, kc_ref[:, kv_of(hh)]) * scale
            s_l = _dot_nt(q, kl_ref[:, kv_of(hh)]) * scale if with_lat else None
            return s_c, s_l

        def attend(hh, s_c, s_l):
            m = jnp.max(s_c, axis=-1, keepdims=True)
            if with_lat:
                m = jnp.maximum(m, jnp.max(s_l, axis=-1, keepdims=True))
            p_c = jnp.exp(s_c - m)
            den = jnp.sum(p_c, axis=-1, keepdims=True)
            o = _dot(p_c.astype(BF16), vc_ref[:, kv_of(hh)])
            if with_lat:
                p_l = jnp.exp(s_l - m)
                den = den + jnp.sum(p_l, axis=-1, keepdims=True)
                o = o + _dot(p_l.astype(BF16), vl_ref[:, kv_of(hh)])
            return o / den

        outs = []
        pending = scores(0)
        for hh in range(n_heads):
            upcoming = scores(hh + 1) if hh + 1 < n_heads else None
            outs.append(attend(hh, *pending))
            pending = upcoming
        y = _dot(jnp.concatenate(outs, axis=1).astype(BF16), wo_ref[...])
        o_ref[...] = x_ref[...] + mod_ref[0][2:3] * y

    t = pl.program_id(0)
    pl.when(t < n_lat_tiles)(lambda: body(True))
    pl.when(t >= n_lat_tiles)(lambda: body(False))


def _attn_layer(st, xs, mods, gain, w_qkv, q_gain, k_gain, w_o, with_ctx):
    d = st.d
    tm = st.ctx_len
    assert st.seq % tm == 0
    n_all = st.n_tok // tm
    n_lat_tiles, per_seq = st.n_lat // tm, st.seq // tm
    nq = w_o.shape[0]
    nk = (w_qkv.shape[1] - nq) // 2
    cos, s_lo, s_hi = _rope_tables(st.seq, tm)
    rope_spec = pl.BlockSpec((tm, HEAD_DIM), lambda t: (jnp.where(t < n_lat_tiles, t % per_seq, per_seq), 0))
    q, k, v = pl.pallas_call(
        _qkv_kernel,
        grid=(n_all,),
        in_specs=[
            _tok_spec(tm, d), st.mod_spec(tm), _full_spec((1, d)), _full_spec(w_qkv.shape),
            _full_spec((1, HEAD_DIM)), _full_spec((1, HEAD_DIM)), rope_spec, rope_spec, rope_spec,
        ],
        out_specs=[_tok_spec(tm, nq), _tok_spec(tm, nk), _tok_spec(tm, nk)],
        out_shape=[jax.ShapeDtypeStruct((st.n_tok, nq), BF16), jax.ShapeDtypeStruct((st.n_tok, nk), BF16),
                   jax.ShapeDtypeStruct((st.n_tok, nk), BF16)],
        compiler_params=_cparams("parallel"),
        name="attn_qkv",
    )(xs, mods, gain.reshape(1, d), w_qkv.astype(BF16), q_gain.reshape(1, HEAD_DIM),
      k_gain.reshape(1, HEAD_DIM), cos, s_lo, s_hi)

    bsz = st.bsz
    batch_of = lambda t: jnp.where(t < n_lat_tiles, t // per_seq, t - n_lat_tiles)
    lat_spec = pl.BlockSpec((st.seq, nk), lambda t: (jnp.minimum(t // per_seq, bsz - 1), 0))
    ctx_spec = pl.BlockSpec((tm, nk), lambda t: (n_lat_tiles + batch_of(t), 0))
    return pl.pallas_call(
        functools.partial(_attn_kernel, n_lat_tiles=n_lat_tiles),
        grid=(st.tiles(tm, with_ctx),),
        in_specs=[_tok_spec(tm, nq), lat_spec, lat_spec, ctx_spec, ctx_spec,
                  _tok_spec(tm, d), st.mod_spec(tm), _full_spec((nq, d))],
        out_specs=_tok_spec(tm, d),
        out_shape=jax.ShapeDtypeStruct((st.tiles(tm, with_ctx) * tm, d), F32),
        compiler_params=_cparams("parallel"),
        name="attn_core",
    )(q, k, v, k, v, xs, mods, w_o.astype(BF16))


_INT_MIN = -2 ** 31


def _sortable(x):
    b = lax.bitcast_convert_type(x, jnp.int32)
    return b ^ ((b >> 31) & 0x7FFFFFFF)


def _unsortable(k):
    return lax.bitcast_convert_type(k ^ ((k >> 31) & 0x7FFFFFFF), F32)


def _extract16(k, iota, exact):
    n = k.shape[0]
    tops = []
    for a in range(PEER_TOPK):
        m = jnp.max(k, axis=0, keepdims=True)
        hit = k == m
        if exact:
            first = jnp.min(jnp.where(hit, iota, n), axis=0, keepdims=True)
            hit = iota == first
        k = jnp.where(hit, _INT_MIN + a, k)
        tops.append(m)
    return k, tops


def _rows16(tops, iota16):
    out = jnp.zeros(iota16.shape, tops[0].dtype)
    for a, t in enumerate(tops):
        out = jnp.where(iota16 == a, t, out)
    return out


_CAND_HALF = PEER_TOPK // 2


def _peer_select(s1, s2, exact):
    n, width = s1.shape
    topk = float(PEER_TOPK)
    iota = lax.broadcasted_iota(jnp.int32, (n, width), 0)
    iota16 = iota[0:PEER_TOPK]
    mark_end = _INT_MIN + PEER_TOPK
    k1, top1 = _extract16(_sortable(s1), iota, exact)
    k2, top2 = _extract16(_sortable(s2), iota, exact)
    v1 = _unsortable(_rows16(top1, iota16))
    v2 = _unsortable(_rows16(top2, iota16))
    f1 = [_unsortable(t) for t in top1[:_CAND_HALF]]
    f2_0 = _unsortable(top2[0])
    pieces = [v2 + f1[0]]
    pieces += [v2[0:_CAND_HALF] + f1[a] for a in range(1, _CAND_HALF)]
    pieces.append(v1[_CAND_HALF:] + f2_0)
    cand = jnp.concatenate(pieces, axis=0)
    nc = cand.shape[0]
    kc, _ = _extract16(_sortable(cand), iota[0:nc], exact)
    picked = kc < mark_end
    if not exact:
        surplus = jnp.sum(jnp.where(picked, 1.0, 0.0), axis=0, keepdims=True) - topk
        picked = kc < mark_end - surplus.astype(jnp.int32)
    sel = jnp.where(picked, 1.0, 0.0)
    z = jnp.sum(jnp.where(picked, jnp.exp(cand - (f1[0] + f2_0)), 0.0), axis=0, keepdims=True)
    cnt = [jnp.sum(sel[0:PEER_TOPK], axis=0, keepdims=True)]
    for a in range(1, _CAND_HALF):
        row = PEER_TOPK + (a - 1) * _CAND_HALF
        cnt.append(jnp.sum(sel[row:row + _CAND_HALF], axis=0, keepdims=True))
    tail = sel[nc - _CAND_HALF:]
    for a in range(_CAND_HALF, PEER_TOPK):
        pick = iota[0:_CAND_HALF] == a - _CAND_HALF
        cnt.append(jnp.sum(jnp.where(pick, tail, 0.0), axis=0, keepdims=True))
    cut = jnp.zeros(s1.shape, F32)
    for a in range(PEER_TOPK):
        cut = jnp.where(k1 == _INT_MIN + a, cnt[a], cut)
    in1, in2 = k1 < mark_end, k2 < mark_end
    r2 = jnp.where(in2, (k2 - _INT_MIN).astype(F32), topk)
    e1 = jnp.exp(s1 - f1[0])
    e2 = jnp.exp(s2 - f2_0) / z
    n1 = jnp.sum(jnp.where(in1, 1.0, 0.0), axis=0, keepdims=True)
    n2 = jnp.sum(jnp.where(in2, 1.0, 0.0), axis=0, keepdims=True)
    nsel = jnp.sum(sel, axis=0, keepdims=True)
    bad = jnp.where((n1 != topk) | (n2 != topk) | (nsel != topk), 1.0, 0.0)
    return cut, e1, r2, e2, bad


def _pack_rows(x):
    return pltpu.bitcast(x.astype(BF16), jnp.uint32)


def _unpack_rows(x):
    return pltpu.bitcast(x, BF16)


def _pack_twice(x):
    u = lax.bitcast_convert_type(x.astype(BF16).astype(F32), jnp.uint32)
    return u | (u >> 16)


PACK_BLOCK = 1024


def _pack_kernel(w_ref, o_ref, *, transpose):
    if transpose:
        o_ref[0, 0] = _pack_rows(w_ref[0].T)
    else:
        o_ref[0] = _pack_rows(w_ref[0])


def _pack_weights(w, transpose):
    layers, r, c = w.shape
    pb = PACK_BLOCK
    if transpose:
        out_spec = pl.BlockSpec((1, 1, pb // 2, pb), lambda l, i, j: (l, i, j, 0))
        out_shape = (layers, r // pb, c // 2, pb)
    else:
        out_spec = pl.BlockSpec((1, pb // 2, pb), lambda l, i, j: (l, i, j))
        out_shape = (layers, r // 2, c)
    return pl.pallas_call(
        functools.partial(_pack_kernel, transpose=transpose),
        grid=(layers, r // pb, c // pb),
        in_specs=[pl.BlockSpec((1, pb, pb), lambda l, i, j: (l, i, j))],
        out_specs=out_spec,
        out_shape=jax.ShapeDtypeStruct(out_shape, jnp.uint32),
        compiler_params=_cparams("parallel", "parallel", "parallel"),
        name="pack_t" if transpose else "pack",
    )(w)


def _peer_select_kernel(x_ref, mod_ref, g_ref, wq_ref, k1_ref, k2_ref,
                        h_ref, cut_ref, e1_ref, r2_ref, e2_ref, s1_scr, s2_scr):
    mod = mod_ref[0]
    f_t = _modulate(x_ref[...], g_ref[...], mod[3:4], mod[4:5]).T.astype(BF16)
    h_ref[...] = _pack_rows(f_t)
    tm = f_t.shape[1]
    n_heads, n_keys, half = k1_ref.shape
    q_t = _dot(_unpack_rows(wq_ref[0]), f_t)
    for hh in range(n_heads):
        base = hh * 2 * half
        s1_scr[hh] = _dot(k1_ref[hh], q_t[base:base + half].astype(BF16))
        s2_scr[hh] = _dot(k2_ref[hh], q_t[base + half:base + 2 * half].astype(BF16))
    n_sub = tm // LANES

    def step(idx, carry):
        hh = idx // (n_sub // SELECT_UNROLL)
        base = (idx % (n_sub // SELECT_UNROLL)) * SELECT_UNROLL
        bad_any = jnp.zeros((1, LANES), F32)
        work = []
        for sub in range(SELECT_UNROLL):
            lanes = pl.ds(pl.multiple_of((base + sub) * LANES, LANES), LANES)
            s1, s2 = s1_scr[hh, :, lanes], s2_scr[hh, :, lanes]
            *fast, bad = _peer_select(s1, s2, exact=False)
            work.append((lanes, s1, s2, fast))
            bad_any = jnp.maximum(bad_any, bad)

        def store(lanes, cut, e1, r2, e2):
            cut_ref[hh, :, lanes] = _pack_twice(cut)
            e1_ref[hh, :, lanes] = _pack_twice(e1)
            r2_ref[hh, :, lanes] = _pack_rows(r2)
            e2_ref[hh, :, lanes] = _pack_rows(e2)

        for lanes, _, _, fast in work:
            store(lanes, *fast)

        @pl.when(jnp.max(bad_any) > 0.0)
        def _():
            for lanes, s1, s2, _ in work:
                store(lanes, *_peer_select(s1, s2, exact=True)[:4])

        return carry

    lax.fori_loop(0, n_heads * n_sub // SELECT_UNROLL, step, 0)


def _peer_main_kernel(h_ref, u_hbm, vt_hbm, cut_ref, e1_ref, r2_ref, e2_ref, x_ref, mod_ref, fg_ref,
                      o_ref, acc, act0, act1, w0, w1, u_buf, vt_buf, sems, *, final_norm, n_chunks, n_items, layer):
    g = pl.program_id(0)
    ec, tm = act0.shape
    n_heads, n_keys = r2_ref.shape[0], 2 * r2_ref.shape[1]
    rows_per_chunk = ec // n_keys
    half_keys = n_keys // 2
    packed_half = half_keys // 2

    @pl.when(g == 0)
    def _():
        acc[...] = jnp.zeros_like(acc)
        act1[...] = jnp.zeros_like(act1)
        w0[...] = jnp.zeros_like(w0)

    n_steps = pl.num_programs(0)
    u_rows = u_buf.shape[1]

    def table_copies(step):
        slot = step % TABLE_BUFFERS
        chunk_u = jnp.clip(step, 0, n_items - 1) % n_chunks
        chunk_v = jnp.clip(step - 2, 0, n_items - 1) % n_chunks
        return (pltpu.make_async_copy(u_hbm.at[layer, pl.ds(chunk_u * u_rows, u_rows)], u_buf.at[slot],
                                      sems.at[0, slot]),
                pltpu.make_async_copy(vt_hbm.at[layer, pl.ds(chunk_v * vt_buf.shape[1], vt_buf.shape[1])],
                                      vt_buf.at[slot], sems.at[1, slot]))

    def start_tables(step):
        for copy in table_copies(step):
            copy.start()

    @pl.when(g == 0)
    def _():
        for ahead in range(TABLE_BUFFERS - 1):
            start_tables(ahead)

    @pl.when(g + TABLE_BUFFERS - 1 < n_steps)
    def _():
        start_tables(g + TABLE_BUFFERS - 1)

    for copy in table_copies(g):
        copy.wait()
    u_ref, vt_ref = u_buf.at[g % TABLE_BUFFERS], vt_buf.at[g % TABLE_BUFFERS]

    def gate_chunk(cut_ref, e1_ref, act_ref, w_ref, token_blocks):
        zero = jnp.zeros((half_keys, LANES), BF16)
        for part in range(n_keys // half_keys):
            prow = slice(part * packed_half, (part + 1) * packed_half)
            for tl in token_blocks:
                lanes = slice(tl * LANES, (tl + 1) * LANES)
                for first in range(0, rows_per_chunk, GATE_ROWS):
                    gates = [zero] * GATE_ROWS
                    for hh in range(n_heads):
                        r2 = _unpack_rows(r2_ref[hh, prow, lanes])
                        e2 = _unpack_rows(e2_ref[hh, prow, lanes])
                        for k in range(GATE_ROWS):
                            il = first + k
                            cut = _unpack_rows(jnp.broadcast_to(cut_ref[hh, il:il + 1, lanes], (packed_half, LANES)))
                            e1 = _unpack_rows(jnp.broadcast_to(e1_ref[hh, il:il + 1, lanes], (packed_half, LANES)))
                            gates[k] = gates[k] + jnp.where(r2 < cut, e2, zero) * e1
                    for k in range(GATE_ROWS):
                        il = first + k
                        rows = slice(il * n_keys + part * half_keys, il * n_keys + (part + 1) * half_keys)
                        w_ref[rows, lanes] = _gelu(act_ref[rows, lanes]).astype(BF16) * gates[k]

    def stage(act_new, act_old, w_new, w_old):
        n_blocks = tm // LANES
        gate_chunk(cut_ref, e1_ref, act_old, w_new, range(0, n_blocks // 2))
        act_new[...] = _dot(_unpack_rows(u_ref[...]), _unpack_rows(h_ref[...]))
        gate_chunk(cut_ref, e1_ref, act_old, w_new, range(n_blocks // 2, n_blocks))
        acc[...] += sum(_dot(_unpack_rows(vt_ref[k]), w_old[k * PACK_BLOCK:(k + 1) * PACK_BLOCK, :])
                        for k in range(ec // PACK_BLOCK))

    pl.when(g % 2 == 0)(lambda: stage(act0, act1, w1, w0))
    pl.when(g % 2 == 1)(lambda: stage(act1, act0, w0, w1))

    tile_done = (g >= 2) & ((g - 2) % n_chunks == n_chunks - 1)

    @pl.when(tile_done)
    def _():
        y = x_ref[...] + mod_ref[0][5:6] * acc[...].T
        if final_norm:
            y = _rms(y) * fg_ref[...]
        o_ref[...] = y

    @pl.when(tile_done)
    def _():
        acc[...] = jnp.zeros_like(acc)


def _peer_layer(st, xs, mods, gain, layer, wq_packed, k1, k2, u_packed, vt_packed, final_gain, with_ctx,
                final_norm):
    tm, d = PEER_TILE, st.d
    n_heads, n_keys, half = k1.shape
    n_exp = 2 * u_packed.shape[1]
    ec = PEER_EXPERT_CHUNK
    n_tiles = st.tiles(tm, with_ctx)
    n_rows = n_tiles * tm
    u32 = jnp.uint32
    h_spec = pl.BlockSpec((d // 2, tm), lambda t, *_: (0, t))
    key_spec = pl.BlockSpec((n_heads, n_keys, tm), lambda t, *_: (0, 0, t))
    key_shape = jax.ShapeDtypeStruct((n_heads, n_keys, n_rows), u32)
    pair_spec = pl.BlockSpec((n_heads, n_keys // 2, tm), lambda t, *_: (0, 0, t))
    pair_shape = jax.ShapeDtypeStruct((n_heads, n_keys // 2, n_rows), u32)
    h, cut, e1, r2, e2 = pl.pallas_call(
        _peer_select_kernel,
        grid=(n_tiles,),
        in_specs=[_tok_spec(tm, d), st.mod_spec(tm), _full_spec((1, d)),
                  pl.BlockSpec((1,) + wq_packed.shape[1:], lambda t: (layer, 0, 0)),
                  _full_spec(k1.shape), _full_spec(k2.shape)],
        out_specs=[h_spec, key_spec, key_spec, pair_spec, pair_spec],
        out_shape=[jax.ShapeDtypeStruct((d // 2, n_rows), u32), key_shape, key_shape, pair_shape, pair_shape],
        scratch_shapes=[pltpu.VMEM((n_heads, n_keys, tm), F32), pltpu.VMEM((n_heads, n_keys, tm), F32)],
        compiler_params=_cparams("parallel"),
        name="peer_select",
    )(xs, mods, gain.reshape(1, d), wq_packed, k1.astype(BF16), k2.astype(BF16))

    n_chunks = n_exp // ec
    n_items = n_tiles * n_chunks
    rows_per_chunk = ec // n_keys
    mod_row = st.mod_row(tm)

    def item(lag):
        def tile(g):
            return jnp.clip(g - lag, 0, n_items - 1) // n_chunks

        def chunk(g):
            return jnp.clip(g - lag, 0, n_items - 1) % n_chunks

        return tile, chunk

    (tile0, chunk0), (tile1, chunk1), (tile2, chunk2) = item(0), item(1), item(2)
    sub1_spec = pl.BlockSpec((n_heads, rows_per_chunk, tm), lambda g: (0, chunk1(g), tile1(g)))
    sub2_spec = pl.BlockSpec((n_heads, n_keys // 2, tm), lambda g: (0, 0, tile1(g)))
    return pl.pallas_call(
        functools.partial(_peer_main_kernel, final_norm=final_norm, n_chunks=n_chunks, n_items=n_items,
                          layer=layer),
        grid=(n_items + 2,),
        in_specs=[
            pl.BlockSpec((d // 2, tm), lambda g: (0, tile0(g))),
            pl.BlockSpec(memory_space=pl.ANY), pl.BlockSpec(memory_space=pl.ANY),
            sub1_spec, sub1_spec, sub2_spec, sub2_spec,
            pl.BlockSpec((tm, d), lambda g: (tile2(g), 0)),
            pl.BlockSpec((1, MOD_CHUNKS, d), lambda g: (mod_row(tile2(g)), 0, 0)),
            _full_spec((1, d)),
        ],
        out_specs=pl.BlockSpec((tm, d), lambda g: (tile2(g), 0)),
        out_shape=jax.ShapeDtypeStruct((n_rows, d), F32),
        scratch_shapes=[pltpu.VMEM((d, tm), F32),
                        pltpu.VMEM((ec, tm), F32), pltpu.VMEM((ec, tm), F32),
                        pltpu.VMEM((ec, tm), BF16), pltpu.VMEM((ec, tm), BF16),
                        pltpu.VMEM((TABLE_BUFFERS, ec // 2, d), u32),
                        pltpu.VMEM((TABLE_BUFFERS, ec // PACK_BLOCK, d // 2, PACK_BLOCK), u32),
                        pltpu.SemaphoreType.DMA((2, TABLE_BUFFERS))],
        compiler_params=_cparams("arbitrary"),
        name="peer_main",
    )(h, u_packed, vt_packed, cut, e1, r2, e2, xs, mods, final_gain.reshape(1, d))


def kernel(x, c, ctx, c_ctx, ada_w, ada_b, norm1_g, norm2_g, sgu_w_in, sgu_v_gain, sgu_w_s, sgu_b_s, sgu_w_out, pool_w_in, pool_w_grp, pool_scale, pool_w_out, attn_w_qkv, attn_q_gain, attn_k_gain, attn_w_o, peer_w_q, peer_k1, peer_k2, peer_u, peer_v, final_gain):
    bsz, seq, d = x.shape
    ctx_len = ctx.shape[1]
    depth = ada_w.shape[0]
    st = _Stream(bsz, seq, ctx_len, d)
    assert bsz + 1 <= MOD_ROWS and seq % PEER_TILE == 0 and (bsz * ctx_len) % PEER_TILE == 0
    assert seq % TOKEN_TILE == 0 and ctx_len % TOKEN_TILE == 0

    cond = jnp.concatenate([c, c_ctx[None, :], jnp.zeros((MOD_ROWS - bsz - 1, d), F32)], axis=0)
    mods = _ada_mods(cond, ada_w, ada_b)
    assert peer_w_q.shape[1] == PACK_BLOCK and PEER_EXPERT_CHUNK % PACK_BLOCK == 0
    wq_packed = _pack_weights(peer_w_q, transpose=True)[:, 0]
    u_packed = _pack_weights(peer_u, transpose=False)
    vt_packed = _pack_weights(peer_v, transpose=True)
    xs = jnp.concatenate([x.reshape(bsz * seq, d), ctx.reshape(bsz * ctx_len, d)], axis=0)

    for i in range(depth):
        kind, j = i % N_MIXERS, i // N_MIXERS
        with_ctx = i < depth - 1
        if kind == 0:
            xs = _sgu_layer(st, xs, mods[i], norm1_g[i], sgu_w_in[j], sgu_v_gain[j], sgu_w_s[j], sgu_b_s[j],
                            sgu_w_out[j], with_ctx)
        elif kind == 1:
            xs = _pool_layer(st, xs, mods[i], norm1_g[i], pool_w_in[j], pool_w_grp[j], pool_scale[j],
                             pool_w_out[j], with_ctx)
        else:
            xs = _attn_layer(st, xs, mods[i], norm1_g[i], attn_w_qkv[j], attn_q_gain[j], attn_k_gain[j],
                             attn_w_o[j], with_ctx)
        xs = _peer_layer(st, xs, mods[i], norm2_g[i], i, wq_packed, peer_k1[i], peer_k2[i], u_packed, vt_packed,
                         final_gain, with_ctx, final_norm=(i == depth - 1))
    return xs[:bsz * seq].reshape(bsz, seq, d)
```
